```python
import math
import jax
import jax.numpy as jnp
from jax import lax
import numpy as np

D_MODEL = 4096
BATCH = 2
SEQ = 8192
DEPTH = 2

GRID_W = 64
CTX_LEN = 256
N_MIXERS = 4
MIX_W = D_MODEL
GROUP_W = MIX_W // N_MIXERS
S5_W = GROUP_W
S5_CH = 16
S5_GROUPS = S5_W // S5_CH
S5_STATE = 64
LRU_W = GROUP_W
LRU_HEADS = 16
LRU_HD = LRU_W // LRU_HEADS
LRU_CONV = 4
LRU_C = 8.0
GLA_HEADS = 4
GLA_DV = GROUP_W // GLA_HEADS
GLA_DK = GLA_DV // 2
GLA_QK_W = GLA_HEADS * GLA_DK
GLA_RANK = 16
GLA_GATE_NORM = 16.0
GLA_CHUNK = 64
HY_W = GROUP_W
HY_ORDER = 2
HY_SHORT = 3
HY_BANDS = 16
HY_EMB = 1 + 2 * HY_BANDS
HY_FFN = 64
HY_DECAY_TARGET = 1e-2
HY_FAST_PCT = 0.3
HY_SLOW_PCT = 1.5
D_FF = ((8 * D_MODEL // 3 + 255) // 256) * 256
FFN_CONV = 3
EPS = 1e-6
IN_SIZES = (S5_W, LRU_W, LRU_W, GLA_QK_W, GLA_QK_W, GLA_HEADS * GLA_DV, GROUP_W, GLA_RANK, GLA_RANK, (HY_ORDER + 1) * HY_W)
IN_W = sum(IN_SIZES)
IN_SPLITS = tuple(int(s) for s in np.cumsum(IN_SIZES)[:-1])

kernel_name = "hybrid_prefix_dit_block"


def rmsnorm(x, g):
    xf = x.astype(jnp.float32)
    y = xf * lax.rsqrt(jnp.mean(xf * xf, axis=-1, keepdims=True) + EPS)
    return y.astype(x.dtype) * g


def modulate(h, shift, scale):
    return h * (1.0 + scale) + shift


def dwconv1d(x, w, b, pad_left):
    k = w.shape[0]
    y = lax.conv_general_dilated(x, w[:, None, :].astype(x.dtype), window_strides=(1,),
                                 padding=[(pad_left, k - 1 - pad_left)],
                                 dimension_numbers=("NWC", "WIO", "NWC"),
                                 feature_group_count=x.shape[-1])
    return y + b


def dwconv2d(x, w, b):
    k = w.shape[0]
    y = lax.conv_general_dilated(x, w[:, :, None, :].astype(x.dtype), window_strides=(1, 1),
                                 padding=[(k // 2, k // 2), (k // 2, k // 2)],
                                 dimension_numbers=("NHWC", "HWIO", "NHWC"),
                                 feature_group_count=x.shape[-1])
    return y + b


def real_linear_scan(a, b, h0):
    b = b.at[:, 0].add(a[:, 0] * h0)

    def combine(e1, e2):
        a1, b1 = e1
        a2, b2 = e2
        return a1 * a2, a2 * b1 + b2

    return lax.associative_scan(combine, (a, b), axis=1)[1]


def complex_diag_scan(a_re, a_im, b_re, b_im, h0_re, h0_im):
    b_re = b_re.at[0].add(a_re * h0_re - a_im * h0_im)
    b_im = b_im.at[0].add(a_re * h0_im + a_im * h0_re)
    L = b_re.shape[0]
    ar = jnp.broadcast_to(a_re, (L, 1) + a_re.shape)
    ai = jnp.broadcast_to(a_im, (L, 1) + a_im.shape)

    def combine(e1, e2):
        a1r, a1i, b1r, b1i = e1
        a2r, a2i, b2r, b2i = e2
        return (a2r * a1r - a2i * a1i, a2r * a1i + a2i * a1r,
                a2r * b1r - a2i * b1i + b2r, a2r * b1i + a2i * b1r + b2i)

    _, _, hr, hi = lax.associative_scan(combine, (ar, ai, b_re, b_im), axis=0)
    return hr, hi


def s5_mixer(u, h0, lp):
    f32 = jnp.float32
    bsz, L, _ = u.shape
    uf = u.astype(f32)
    ug = uf.reshape(bsz, L, S5_GROUPS, S5_CH).transpose(1, 0, 2, 3)
    y = uf * lp["s5_d"]
    st_re, st_im = [], []
    for d in range(2):
        lam_re = lp["s5_lam_re"][d].astype(f32)
        lam_im = lp["s5_lam_im"][d].astype(f32)
        step = jnp.exp(lp["s5_log_step"][d].astype(f32))[:, None]
        mag = jnp.exp(lam_re * step)
        ab_re = mag * jnp.cos(lam_im * step)
        ab_im = mag * jnp.sin(lam_im * step)
        den = lam_re * lam_re + lam_im * lam_im
        co_re = ((ab_re - 1.0) * lam_re + ab_im * lam_im) / den
        co_im = (ab_im * lam_re - (ab_re - 1.0) * lam_im) / den
        b_re = lp["s5_b_re"][d].astype(f32)
        b_im = lp["s5_b_im"][d].astype(f32)
        bb_re = co_re[..., None] * b_re - co_im[..., None] * b_im
        bb_im = co_re[..., None] * b_im + co_im[..., None] * b_re
        seq = ug if d == 0 else ug[::-1]
        bu_re = jnp.einsum("lbgc,gpc->lbgp", seq, bb_re)
        bu_im = jnp.einsum("lbgc,gpc->lbgp", seq, bb_im)
        hr, hi = complex_diag_scan(ab_re, ab_im, bu_re, bu_im, h0[0][d], h0[1][d])
        st_re.append(hr[-1])
        st_im.append(hi[-1])
        yd = (jnp.einsum("lbgp,gcp->lbgc", hr, lp["s5_c_re"][d].astype(f32))
              - jnp.einsum("lbgp,gcp->lbgc", hi, lp["s5_c_im"][d].astype(f32)))
        if d == 1:
            yd = yd[::-1]
        y = y + yd.transpose(1, 0, 2, 3).reshape(bsz, L, S5_W)
    yg = jax.nn.gelu(y).astype(u.dtype)
    out = yg * jax.nn.sigmoid(yg @ lp["s5_w_glu"] + lp["s5_b_glu"])
    return out, (jnp.stack(st_re), jnp.stack(st_im))


def rglru_mixer(xb, gate, h0, lp):
    f32 = jnp.float32
    bsz, L, _ = xb.shape
    xc = dwconv1d(xb, lp["lru_conv_w"], lp["lru_conv_b"], pad_left=LRU_CONV // 2).astype(f32)
    xh = xc.reshape(bsz, L, LRU_HEADS, LRU_HD)
    h_sum = jnp.zeros_like(xc)
    states = []
    for d in range(2):
        r = jax.nn.sigmoid(jnp.einsum("blhi,hij->blhj", xh, lp["lru_w_a"][d]).reshape(bsz, L, LRU_W) + lp["lru_b_a"][d])
        i = jax.nn.sigmoid(jnp.einsum("blhi,hij->blhj", xh, lp["lru_w_x"][d]).reshape(bsz, L, LRU_W) + lp["lru_b_x"][d])
        log_a = -LRU_C * jax.nn.softplus(-lp["lru_lam"][d]) * r
        a = jnp.exp(log_a)
        b = jnp.sqrt(-jnp.expm1(2.0 * log_a)) * (i * xc)
        if d == 1:
            a, b = a[:, ::-1], b[:, ::-1]
        h = real_linear_scan(a, b, h0[d])
        states.append(h[:, -1])
        h_sum = h_sum + (h if d == 0 else h[:, ::-1])
    return (h_sum * jax.nn.gelu(gate.astype(f32))).astype(xb.dtype), jnp.stack(states)


def gla_chunked(q, k, v, log_a, s0):
    bsz, L, H, _ = q.shape
    V = v.shape[-1]
    C = GLA_CHUNK
    N = L // C

    def chunks(t):
        return t.reshape(bsz, N, C, H, t.shape[-1]).transpose(1, 0, 3, 2, 4)

    qc, kc, vc, ac = chunks(q), chunks(k), chunks(v), chunks(log_a)
    b = jnp.cumsum(ac, axis=3)
    b_ref = b[:, :, :, C // 2 - 1:C // 2]
    b_last = b[:, :, :, -1:]
    scores = jnp.einsum("nbhik,nbhjk->nbhij", qc * jnp.exp(b - b_ref), kc * jnp.exp(b_ref - b))
    mask = jnp.tril(jnp.ones((C, C), dtype=bool))
    scores = jnp.where(mask, scores, 0.0)
    intra = jnp.einsum("nbhij,nbhjv->nbhiv", scores, vc)
    q_dec = qc * jnp.exp(b)
    k_dec = kc * jnp.exp(b_last - b)
    decay = jnp.exp(b_last[:, :, :, 0])

    def step(s, inp):
        qn, kn, vn, dn = inp
        o = jnp.einsum("bhck,bhkv->bhcv", qn, s)
        s = dn[..., None] * s + jnp.einsum("bhck,bhcv->bhkv", kn, vn)
        return s, o

    s_fin, inter = lax.scan(step, s0, (q_dec, k_dec, vc, decay))
    o = (intra + inter).transpose(1, 0, 3, 2, 4).reshape(bsz, L, H, V)
    return o, s_fin


def gla_mixer(q, k, v, g, lr_f, lr_b, s0, lp):
    f32 = jnp.float32
    bsz, L, _ = q.shape
    qh = (q.astype(f32) * GLA_DK ** -0.5).reshape(bsz, L, GLA_HEADS, GLA_DK)
    kh = k.astype(f32).reshape(bsz, L, GLA_HEADS, GLA_DK)
    vh = v.astype(f32).reshape(bsz, L, GLA_HEADS, GLA_DV)
    o = jnp.zeros_like(vh)
    states = []
    for d, lr in enumerate((lr_f, lr_b)):
        la = jax.nn.log_sigmoid(lr.astype(f32) @ lp["gla_w_alpha"][d] + lp["gla_b_alpha"][d]) / GLA_GATE_NORM
        la = la.reshape(bsz, L, GLA_HEADS, GLA_DK)
        if d == 0:
            od, sd = gla_chunked(qh, kh, vh, la, s0[d])
        else:
            od, sd = gla_chunked(qh[:, ::-1], kh[:, ::-1], vh[:, ::-1], la[:, ::-1], s0[d])
            od = od[:, ::-1]
        o = o + od
        states.append(sd)
    o = o * lax.rsqrt(jnp.mean(o * o, axis=-1, keepdims=True) + EPS) * lp["gla_norm_g"]
    out = o.reshape(bsz, L, GLA_HEADS * GLA_DV).astype(q.dtype) * jax.nn.silu(g)
    return out, jnp.stack(states)


def hyena_filters(L, lp):
    f32 = jnp.float32
    pos = jnp.arange(L, dtype=f32)[:, None]
    t = pos / L
    bands = jnp.linspace(1e-4, HY_BANDS - 1, HY_BANDS, dtype=f32)
    w = 2.0 * math.pi * pos / L
    feats = jnp.concatenate([t, jnp.cos(w * bands), -jnp.sin(w * bands)], axis=-1)
    h = jnp.sin(lp["hy_freq"][0] * (feats @ lp["hy_w1"] + lp["hy_b1"]))
    h = jnp.sin(lp["hy_freq"][1] * (h @ lp["hy_w2"] + lp["hy_b2"]))
    h = (h @ lp["hy_w3"]).reshape(L, 2, HY_ORDER, HY_W)
    deltas = jnp.abs(jnp.linspace(math.log(HY_DECAY_TARGET) / HY_FAST_PCT,
                                  math.log(HY_DECAY_TARGET) / HY_SLOW_PCT, HY_W, dtype=f32))
    decay = jnp.exp(-t * deltas)
    return h * decay[:, None, None, :]


def bidir_long_conv(z, h_fwd, h_bwd):
    L = z.shape[1]
    circ = jnp.concatenate([h_fwd, jnp.zeros_like(h_fwd[:1]), h_bwd[1:][::-1]], axis=0)
    zf = jnp.fft.rfft(z.astype(jnp.float32), n=2 * L, axis=1)
    cf = jnp.fft.rfft(circ.astype(jnp.float32), n=2 * L, axis=0)
    return jnp.fft.irfft(zf * cf[None], n=2 * L, axis=1)[:, :L]


def hyena_mixer(hz, lp):
    dtype = hz.dtype
    L = hz.shape[1]
    hz = dwconv1d(hz, lp["hy_conv_w"], lp["hy_conv_b"], pad_left=HY_SHORT // 2)
    v, x1, x2 = jnp.split(hz.astype(jnp.float32), 3, axis=-1)
    filt = hyena_filters(L, lp)
    u = v
    for o, gate in enumerate((x1, x2)):
        u = gate * (bidir_long_conv(u, filt[:, 0, o], filt[:, 1, o]) + lp["hy_bias"][o] * u)
    return u.astype(dtype)


def token_mixers(p, states, lp, need_output):
    s5_u, lru_x, lru_g, q, k, v, g, lr_f, lr_b, hy = jnp.split(p, IN_SPLITS, axis=-1)
    y_s5, st_s5 = s5_mixer(s5_u, states[0], lp)
    y_lru, st_lru = rglru_mixer(lru_x, lru_g, states[1], lp)
    y_gla, st_gla = gla_mixer(q, k, v, g, lr_f, lr_b, states[2], lp)
    new_states = (st_s5, st_lru, st_gla)
    if not need_output:
        return None, new_states
    y_hy = hyena_mixer(hy, lp)
    y = jnp.stack([y_s5, y_lru, y_gla, y_hy], axis=-2)
    y = rmsnorm(y, lp["mix_norm_g"].reshape(N_MIXERS, GROUP_W)).reshape(p.shape[:-1] + (MIX_W,))
    return y @ lp["w_out"], new_states


def conv_ffn(h, rows, lp):
    bsz, L, _ = h.shape
    gate = (h @ lp["mlp_w_gate"]).reshape(bsz, rows, L // rows, D_FF)
    gate = dwconv2d(gate, lp["mlp_conv_w"], lp["mlp_conv_b"]).reshape(bsz, L, D_FF)
    return (jax.nn.silu(gate) * (h @ lp["mlp_w_up"])) @ lp["mlp_w_down"]


def setup_inputs(seed: int = 0) -> dict:
    key = jax.random.key(seed)
    ks = iter(jax.random.split(key, 64))
    f32 = jnp.float32

    def nrm(shape, scale):
        return jax.random.normal(next(ks), shape, f32) * scale

    def gain(shape):
        return 1.0 + nrm(shape, 0.02)

    lru_a = jax.random.uniform(next(ks), (DEPTH, 2, LRU_W), f32, 0.9, 0.999)
    lru_s = lru_a ** (1.0 / LRU_C)
    return {
        "x": nrm((BATCH, SEQ, D_MODEL), 1.0),
        "c": nrm((BATCH, D_MODEL), 1.0),
        "ctx": nrm((BATCH, CTX_LEN, D_MODEL), 1.0),
        "c_ctx": nrm((D_MODEL,), 1.0),
        "w_ada": nrm((DEPTH, D_MODEL, 6 * D_MODEL), 0.5 * D_MODEL ** -0.5),
        "b_ada": nrm((DEPTH, 6 * D_MODEL), 0.02),
        "norm_mix_g": gain((DEPTH, D_MODEL)),
        "norm_mlp_g": gain((DEPTH, D_MODEL)),
        "w_in": nrm((DEPTH, D_MODEL, IN_W), D_MODEL ** -0.5),
        "s5_lam_re": -0.5 + nrm((DEPTH, 2, S5_GROUPS, S5_STATE), 0.01),
        "s5_lam_im": math.pi * jnp.arange(S5_STATE, dtype=f32) + nrm((DEPTH, 2, S5_GROUPS, S5_STATE), 0.01),
        "s5_log_step": jax.random.uniform(next(ks), (DEPTH, 2, S5_GROUPS), f32, math.log(1e-3), math.log(1e-1)),
        "s5_b_re": nrm((DEPTH, 2, S5_GROUPS, S5_STATE, S5_CH), (2 * S5_CH) ** -0.5),
        "s5_b_im": nrm((DEPTH, 2, S5_GROUPS, S5_STATE, S5_CH), (2 * S5_CH) ** -0.5),
        "s5_c_re": nrm((DEPTH, 2, S5_GROUPS, S5_CH, S5_STATE), (2 * S5_STATE) ** -0.5),
        "s5_c_im": nrm((DEPTH, 2, S5_GROUPS, S5_CH, S5_STATE), (2 * S5_STATE) ** -0.5),
        "s5_d": nrm((DEPTH, S5_W), 1.0),
        "s5_w_glu": nrm((DEPTH, S5_W, S5_W), S5_W ** -0.5),
        "s5_b_glu": nrm((DEPTH, S5_W), 0.02),
        "lru_conv_w": nrm((DEPTH, LRU_CONV, LRU_W), LRU_CONV ** -0.5),
        "lru_conv_b": nrm((DEPTH, LRU_W), 0.02),
        "lru_w_a": nrm((DEPTH, 2, LRU_HEADS, LRU_HD, LRU_HD), LRU_HD ** -0.5),
        "lru_b_a": nrm((DEPTH, 2, LRU_W), 0.02),
        "lru_w_x": nrm((DEPTH, 2, LRU_HEADS, LRU_HD, LRU_HD), LRU_HD ** -0.5),
        "lru_b_x": nrm((DEPTH, 2, LRU_W), 0.02),
        "lru_lam": jnp.log(lru_s) - jnp.log1p(-lru_s),
        "gla_w_alpha": nrm((DEPTH, 2, GLA_RANK, GLA_QK_W), GLA_RANK ** -0.5),
        "gla_b_alpha": nrm((DEPTH, 2, GLA_QK_W), 0.02),
        "gla_norm_g": gain((DEPTH, GLA_DV)),
        "hy_conv_w": nrm((DEPTH, HY_SHORT, (HY_ORDER + 1) * HY_W), HY_SHORT ** -0.5),
        "hy_conv_b": nrm((DEPTH, (HY_ORDER + 1) * HY_W), 0.02),
        "hy_w1": nrm((DEPTH, HY_EMB, HY_FFN), HY_EMB ** -0.5),
        "hy_b1": nrm((DEPTH, HY_FFN), 0.02),
        "hy_w2": nrm((DEPTH, HY_FFN, HY_FFN), HY_FFN ** -0.5),
        "hy_b2": nrm((DEPTH, HY_FFN), 0.02),
        "hy_w3": nrm((DEPTH, HY_FFN, 2 * HY_ORDER * HY_W), HY_FFN ** -0.5),
        "hy_freq": gain((DEPTH, 2, HY_FFN)),
        "hy_bias": nrm((DEPTH, HY_ORDER, HY_W), 1.0),
        "mix_norm_g": gain((DEPTH, MIX_W)),
        "w_out": nrm((DEPTH, MIX_W, D_MODEL), MIX_W ** -0.5),
        "mlp_w_gate": nrm((DEPTH, D_MODEL, D_FF), D_MODEL ** -0.5),
        "mlp_w_up": nrm((DEPTH, D_MODEL, D_FF), D_MODEL ** -0.5),
        "mlp_conv_w": nrm((DEPTH, FFN_CONV, FFN_CONV, D_FF), 1.0 / FFN_CONV),
        "mlp_conv_b": nrm((DEPTH, D_FF), 0.02),
        "mlp_w_down": nrm((DEPTH, D_FF, D_MODEL), D_FF ** -0.5),
        "final_norm_g": gain((D_MODEL,)),
    }


def reference(x, c, ctx, c_ctx, w_ada, b_ada, norm_mix_g, norm_mlp_g, w_in,
              s5_lam_re, s5_lam_im, s5_log_step, s5_b_re, s5_b_im, s5_c_re, s5_c_im, s5_d, s5_w_glu, s5_b_glu,
              lru_conv_w, lru_conv_b, lru_w_a, lru_b_a, lru_w_x, lru_b_x, lru_lam,
              gla_w_alpha, gla_b_alpha, gla_norm_g,
              hy_conv_w, hy_conv_b, hy_w1, hy_b1, hy_w2, hy_b2, hy_w3, hy_freq, hy_bias,
              mix_norm_g, w_out, mlp_w_gate, mlp_w_up, mlp_conv_w, mlp_conv_b, mlp_w_down, final_norm_g):
    f32 = jnp.float32
    bsz, seq, _ = x.shape
    rows = seq // GRID_W
    for l in range(DEPTH):
        last = l == DEPTH - 1
        lp = {
            "s5_lam_re": s5_lam_re[l], "s5_lam_im": s5_lam_im[l], "s5_log_step": s5_log_step[l],
            "s5_b_re": s5_b_re[l], "s5_b_im": s5_b_im[l], "s5_c_re": s5_c_re[l], "s5_c_im": s5_c_im[l],
            "s5_d": s5_d[l], "s5_w_glu": s5_w_glu[l], "s5_b_glu": s5_b_glu[l],
            "lru_conv_w": lru_conv_w[l], "lru_conv_b": lru_conv_b[l], "lru_w_a": lru_w_a[l], "lru_b_a": lru_b_a[l],
            "lru_w_x": lru_w_x[l], "lru_b_x": lru_b_x[l], "lru_lam": lru_lam[l],
            "gla_w_alpha": gla_w_alpha[l], "gla_b_alpha": gla_b_alpha[l], "gla_norm_g": gla_norm_g[l],
            "hy_conv_w": hy_conv_w[l], "hy_conv_b": hy_conv_b[l], "hy_w1": hy_w1[l], "hy_b1": hy_b1[l],
            "hy_w2": hy_w2[l], "hy_b2": hy_b2[l], "hy_w3": hy_w3[l], "hy_freq": hy_freq[l], "hy_bias": hy_bias[l],
            "mix_norm_g": mix_norm_g[l], "w_out": w_out[l],
            "mlp_w_gate": mlp_w_gate[l], "mlp_w_up": mlp_w_up[l], "mlp_conv_w": mlp_conv_w[l],
            "mlp_conv_b": mlp_conv_b[l], "mlp_w_down": mlp_w_down[l],
        }
        mod_x = (jax.nn.silu(c) @ w_ada[l] + b_ada[l])[:, None, :]
        mod_c = jax.nn.silu(c_ctx) @ w_ada[l] + b_ada[l]
        sh1, sc1, g1, sh2, sc2, g2 = jnp.split(mod_x, 6, axis=-1)
        csh1, csc1, cg1, csh2, csc2, cg2 = jnp.split(mod_c, 6, axis=-1)
        zero_s5 = jnp.zeros((2, bsz, S5_GROUPS, S5_STATE), f32)
        zero_states = ((zero_s5, zero_s5),
                       jnp.zeros((2, bsz, LRU_W), f32),
                       jnp.zeros((2, bsz, GLA_HEADS, GLA_DK, GLA_DV), f32))
        pc = modulate(rmsnorm(ctx, norm_mix_g[l]), csh1, csc1) @ w_in[l]
        yc, ctx_states = token_mixers(pc, zero_states, lp, need_output=not last)
        px = modulate(rmsnorm(x, norm_mix_g[l]), sh1, sc1) @ w_in[l]
        yx, _ = token_mixers(px, ctx_states, lp, need_output=True)
        x = x + g1 * yx
        x = x + g2 * conv_ffn(modulate(rmsnorm(x, norm_mlp_g[l]), sh2, sc2), rows, lp)
        if not last:
            ctx = ctx + cg1 * yc
            ctx = ctx + cg2 * conv_ffn(modulate(rmsnorm(ctx, norm_mlp_g[l]), csh2, csc2), 1, lp)
    return rmsnorm(x, final_norm_g)
```

```python
import functools
import math

import numpy as np
import jax
import jax.numpy as jnp
from jax import lax
from jax.experimental import pallas as pl
from jax.experimental.pallas import tpu as pltpu

F32 = jnp.float32
BF16 = jnp.bfloat16
HI = lax.Precision.HIGHEST

EPS = 1e-6
GROUP_W = 1024
N_COL_PAD = 9728
S5_GROUPS = 64
S5_CH = 16
S5_STATE = 64
S5_SEG = 8
S5_JB = 8
LRU_HEADS = 16
LRU_C = 8.0
GLA_HEADS = 4
GLA_DK = 128
GLA_DV = 256
GLA_CHUNK = 64
GLA_GATE_NORM = 16.0
HY_BANDS = 16
HY_FFN = 64
D_FF = 11008
V7X_VMEM_LIMIT = 56 * 1024 * 1024


def _cp(*sem):
    return pltpu.CompilerParams(dimension_semantics=sem, vmem_limit_bytes=V7X_VMEM_LIMIT)


def _dot(a, b):
    return jnp.dot(a, b, preferred_element_type=F32)


def _split(x):
    hi = x.astype(BF16)
    lo = (x - hi.astype(F32)).astype(BF16)
    return hi, lo


def _dot3(fh, fl, x):
    xh, xl = _split(x)
    return _dot(fh, xh) + _dot(fh, xl) + _dot(fl, xh)


def _np_split(a):
    bf = jnp.dtype(BF16)
    hi = np.asarray(a, np.float64).astype(bf)
    lo = (np.asarray(a, np.float64) - hi.astype(np.float64)).astype(bf)
    return jnp.asarray(hi), jnp.asarray(lo)


def _ada_body(c_ref, w_ref, b_ref, o_ref):
    c = c_ref[...]
    a = (c * jax.nn.sigmoid(c)).astype(BF16)
    o_ref[...] = _dot(a, w_ref[...].astype(BF16)) + b_ref[...]


def ada_mod(cvec, w_ada, b_ada, tn=512):
    depth, d, n = w_ada.shape
    return pl.pallas_call(
        _ada_body,
        grid=(depth, n // tn),
        in_specs=[pl.BlockSpec((8, d), lambda l, j: (0, 0)),
                  pl.BlockSpec((None, d, tn), lambda l, j: (l, 0, j)),
                  pl.BlockSpec((None, 1, tn), lambda l, j: (l, 0, j))],
        out_specs=pl.BlockSpec((None, 8, tn), lambda l, j: (l, 0, j)),
        out_shape=jax.ShapeDtypeStruct((depth, 8, n), F32),
        compiler_params=_cp("arbitrary", "arbitrary"),
        name="ada_mod",
    )(cvec, w_ada, b_ada.reshape(depth, 1, n))


def _normmod_body(x_ref, g_ref, sh_ref, sc_ref, o_ref):
    x = x_ref[...]
    y = x * lax.rsqrt(jnp.mean(x * x, axis=-1, keepdims=True) + EPS) * g_ref[...]
    o_ref[...] = (y * (1.0 + sc_ref[0]) + sh_ref[0]).astype(o_ref.dtype)


def normmod(x2d, g, sh, sc, rows_per_mod, out_dtype, tm=256):
    m, d = x2d.shape
    tpm = rows_per_mod // tm
    return pl.pallas_call(
        _normmod_body,
        grid=(m // tm,),
        in_specs=[pl.BlockSpec((tm, d), lambda i: (i, 0)),
                  pl.BlockSpec((1, d), lambda i: (0, 0)),
                  pl.BlockSpec((1, 1, d), lambda i: (i // tpm, 0, 0)),
                  pl.BlockSpec((1, 1, d), lambda i: (i // tpm, 0, 0))],
        out_specs=pl.BlockSpec((tm, d), lambda i: (i, 0)),
        out_shape=jax.ShapeDtypeStruct((m, d), out_dtype),
        compiler_params=_cp("arbitrary"),
        name="normmod",
    )(x2d, g.reshape(1, d), sh, sc)


def _mm_body(*refs, nk, has_res):
    if has_res:
        a_ref, w_ref, res_ref, gate_ref, o_ref = refs[:5]
        scr = refs[5:]
    else:
        a_ref, w_ref, o_ref = refs[:3]
        scr = refs[3:]

    def epilogue(acc):
        if has_res:
            o_ref[...] = res_ref[...] + gate_ref[0] * acc
        else:
            o_ref[...] = acc.astype(o_ref.dtype)

    if nk == 1:
        epilogue(_dot(a_ref[...], w_ref[...]))
    else:
        acc_ref = scr[0]
        k = pl.program_id(2)

        @pl.when(k == 0)
        def _():
            acc_ref[...] = jnp.zeros_like(acc_ref)

        acc_ref[...] += _dot(a_ref[...], w_ref[...])

        @pl.when(k == nk - 1)
        def _():
            epilogue(acc_ref[...])


def matmul(a, w, layer, *, tm, tn, tk=None, res=None, gate=None, rows_per_gate=None, name="matmul"):
    m, kdim = a.shape
    n = w.shape[-1]
    tk = kdim if tk is None else tk
    nk = kdim // tk
    has_res = res is not None
    in_specs = [pl.BlockSpec((tm, tk), lambda i, j, k: (i, k)),
                pl.BlockSpec((None, tk, tn), lambda i, j, k: (layer, k, j))]
    args = [a, w]
    if has_res:
        tpg = rows_per_gate // tm
        in_specs += [pl.BlockSpec((tm, tn), lambda i, j, k: (i, j)),
                     pl.BlockSpec((1, 1, tn), lambda i, j, k: (i // tpg, 0, j))]
        args += [res, gate]
    return pl.pallas_call(
        functools.partial(_mm_body, nk=nk, has_res=has_res),
        grid=(m // tm, n // tn, nk),
        in_specs=in_specs,
        out_specs=pl.BlockSpec((tm, tn), lambda i, j, k: (i, j)),
        out_shape=jax.ShapeDtypeStruct((m, n), F32),
        scratch_shapes=[pltpu.VMEM((tm, tn), F32)] if nk > 1 else [],
        compiler_params=_cp("arbitrary", "arbitrary", "arbitrary"),
        name=name,
    )(*args)


def _s5_params(lp, lseg):
    lam_re, lam_im = lp["s5_lam_re"], lp["s5_lam_im"]
    step = jnp.exp(lp["s5_log_step"])[:, :, None]
    mag = jnp.exp(lam_re * step)
    ab_re = mag * jnp.cos(lam_im * step)
    ab_im = mag * jnp.sin(lam_im * step)
    den = lam_re * lam_re + lam_im * lam_im
    co_re = ((ab_re - 1.0) * lam_re + ab_im * lam_im) / den
    co_im = (ab_im * lam_re - (ab_re - 1.0) * lam_im) / den
    b_re, b_im = lp["s5_b_re"], lp["s5_b_im"]
    bb_re = co_re[..., None] * b_re - co_im[..., None] * b_im
    bb_im = co_re[..., None] * b_im + co_im[..., None] * b_re
    eye = jnp.eye(8, dtype=F32)

    def in_blocks(bb):
        t = bb.reshape(2, S5_JB, 8, S5_STATE, S5_CH)
        t = jnp.einsum("djgpc,gh->djgchp", t, eye)
        return t.reshape(2, S5_JB, 8 * S5_CH, 8 * S5_STATE)

    def out_blocks(cc):
        t = cc.reshape(2, S5_JB, 8, S5_CH, S5_STATE)
        t = jnp.einsum("djgcp,gh->djgphc", t, eye)
        return t.reshape(2, S5_JB, 8 * S5_STATE, 8 * S5_CH)

    wb = jnp.concatenate([in_blocks(bb_re), in_blocks(bb_im)], axis=-1).astype(BF16)
    wc = jnp.concatenate([out_blocks(lp["s5_c_re"]), -out_blocks(lp["s5_c_im"])], axis=-2).astype(BF16)

    def lanes(t):
        return t.reshape(2, S5_JB, 1, 8 * S5_STATE)

    a = jnp.concatenate([lanes(ab_re), lanes(ab_im)], axis=-1)
    pr, pi = ab_re, ab_im
    for _ in range(int(round(math.log2(lseg)))):
        pr, pi = pr * pr - pi * pi, 2.0 * pr * pi
    al = jnp.concatenate([lanes(pr), lanes(pi)], axis=-1)
    return wb, wc, a, al


def _s5_scan_tile(d, t_steps, bu_scr, a_ref, h_scr, store):
    ar = jnp.broadcast_to(a_ref[0, 0, :, 0:512], (S5_SEG, 512))
    ai = jnp.broadcast_to(a_ref[0, 0, :, 512:1024], (S5_SEG, 512))

    def step(s, carry):
        hr, hi = carry
        row = jnp.where(d == 0, s, t_steps - 1 - s)
        off = pl.multiple_of(row * S5_SEG, S5_SEG)
        br = bu_scr[pl.ds(off, S5_SEG), 0:512]
        bi = bu_scr[pl.ds(off, S5_SEG), 512:1024]
        nr = ar * hr - ai * hi + br
        ni = ar * hi + ai * hr + bi
        if store:
            bu_scr[pl.ds(off, S5_SEG), 0:512] = nr
            bu_scr[pl.ds(off, S5_SEG), 512:1024] = ni
        return nr, ni

    hr, hi = lax.fori_loop(0, t_steps, step, (h_scr[:, 0:512], h_scr[:, 512:1024]), unroll=4)
    h_scr[:, 0:512] = hr
    h_scr[:, 512:1024] = hi


def _s5_p1_body(u_ref, wb_ref, a_ref, al_ref, h0_ref, hinit_ref, fin_ref, bu_scr, h_scr, *, t_steps, nt):
    d = pl.program_id(1)
    i = pl.program_id(3)

    @pl.when(i == 0)
    def _():
        h_scr[...] = jnp.zeros_like(h_scr)

    bu_scr[...] = _dot(u_ref[0].astype(BF16), wb_ref[0, 0])
    _s5_scan_tile(d, t_steps, bu_scr, a_ref, h_scr, store=False)

    @pl.when(i == nt - 1)
    def _():
        alr = al_ref[0, 0, :, 0:512]
        ali = al_ref[0, 0, :, 512:1024]
        cr = h0_ref[0, 0, 0, :, 0:512]
        ci = h0_ref[0, 0, 0, :, 512:1024]
        for s in range(S5_SEG):
            k = jnp.where(d == 0, s, S5_SEG - 1 - s)
            hinit_ref[0, 0, 0, pl.ds(k, 1), 0:512] = cr
            hinit_ref[0, 0, 0, pl.ds(k, 1), 512:1024] = ci
            fr = h_scr[pl.ds(k, 1), 0:512]
            fi = h_scr[pl.ds(k, 1), 512:1024]
            cr, ci = alr * cr - ali * ci + fr, alr * ci + ali * cr + fi
        fin_ref[0, 0, 0, :, 0:512] = cr
        fin_ref[0, 0, 0, :, 512:1024] = ci


def _s5_p2_body(u_ref, wb_ref, wc_ref, a_ref, hinit_ref, y_ref, bu_scr, h_scr, *, t_steps):
    d = pl.program_id(1)
    i = pl.program_id(3)

    @pl.when(i == 0)
    def _():
        h_scr[...] = hinit_ref[0, 0, 0]

    bu_scr[...] = _dot(u_ref[0].astype(BF16), wb_ref[0, 0])
    _s5_scan_tile(d, t_steps, bu_scr, a_ref, h_scr, store=True)
    y_ref[0, 0] = _dot(bu_scr[...].astype(BF16), wc_ref[0, 0])


def _s5_fin_body(u_ref, yf_ref, yb_ref, d_ref, w_ref, b_ref, o_ref):
    y = u_ref[0] * d_ref[...] + yf_ref[0, 0] + yb_ref[0, 0]
    yg = jax.nn.gelu(y)
    o_ref[0] = yg * jax.nn.sigmoid(_dot(yg.astype(BF16), w_ref[...]) + b_ref[...])


def s5_mixer(p3, h0, lp, wglu_bf, layer, need_output):
    bsz, seq, _ = p3.shape
    lseg = seq // S5_SEG
    t_steps = min(64, lseg)
    nt = lseg // t_steps
    rows = t_steps * S5_SEG
    wb, wc, a, al = _s5_params(lp, lseg)
    u_perm = p3[:, :, 0:GROUP_W].reshape(bsz, S5_SEG, lseg, GROUP_W).transpose(0, 2, 1, 3).reshape(bsz, seq, GROUP_W)

    def tile(d, i):
        return jnp.where(d == 0, i, nt - 1 - i)

    grid = (bsz, 2, S5_JB, nt)
    u_spec = pl.BlockSpec((1, rows, 128), lambda b, d, j, i: (b, tile(d, i), j))
    wb_spec = pl.BlockSpec((1, 1, 128, 1024), lambda b, d, j, i: (d, j, 0, 0))
    a_spec = pl.BlockSpec((1, 1, 1, 1024), lambda b, d, j, i: (d, j, 0, 0))
    st1_spec = pl.BlockSpec((1, 1, 1, 1, 1024), lambda b, d, j, i: (b, d, j, 0, 0))
    st8_spec = pl.BlockSpec((1, 1, 1, S5_SEG, 1024), lambda b, d, j, i: (b, d, j, 0, 0))
    hinit, fin = pl.pallas_call(
        functools.partial(_s5_p1_body, t_steps=t_steps, nt=nt),
        grid=grid,
        in_specs=[u_spec, wb_spec, a_spec, a_spec, st1_spec],
        out_specs=[st8_spec, st1_spec],
        out_shape=[jax.ShapeDtypeStruct((bsz, 2, S5_JB, S5_SEG, 1024), F32),
                   jax.ShapeDtypeStruct((bsz, 2, S5_JB, 1, 1024), F32)],
        scratch_shapes=[pltpu.VMEM((rows, 1024), F32), pltpu.VMEM((S5_SEG, 1024), F32)],
        compiler_params=_cp("arbitrary", "arbitrary", "arbitrary", "arbitrary"),
        name="s5_pass1",
    )(u_perm, wb, a, al, h0)
    if not need_output:
        return None, fin
    y = pl.pallas_call(
        functools.partial(_s5_p2_body, t_steps=t_steps),
        grid=grid,
        in_specs=[u_spec, wb_spec,
                  pl.BlockSpec((1, 1, 1024, 128), lambda b, d, j, i: (d, j, 0, 0)),
                  a_spec, st8_spec],
        out_specs=pl.BlockSpec((1, 1, rows, 128), lambda b, d, j, i: (d, b, tile(d, i), j)),
        out_shape=jax.ShapeDtypeStruct((2, bsz, seq, GROUP_W), F32),
        scratch_shapes=[pltpu.VMEM((rows, 1024), F32), pltpu.VMEM((S5_SEG, 1024), F32)],
        compiler_params=_cp("arbitrary", "arbitrary", "arbitrary", "arbitrary"),
        name="s5_pass2",
    )(u_perm, wb, wc, a, hinit)
    tr = min(512, seq)
    out = pl.pallas_call(
        _s5_fin_body,
        grid=(bsz, seq // tr),
        in_specs=[pl.BlockSpec((1, tr, GROUP_W), lambda b, i: (b, i, 0)),
                  pl.BlockSpec((1, 1, tr, GROUP_W), lambda b, i: (0, b, i, 0)),
                  pl.BlockSpec((1, 1, tr, GROUP_W), lambda b, i: (1, b, i, 0)),
                  pl.BlockSpec((1, GROUP_W), lambda b, i: (0, 0)),
                  pl.BlockSpec((None, GROUP_W, GROUP_W), lambda b, i: (layer, 0, 0)),
                  pl.BlockSpec((1, GROUP_W), lambda b, i: (0, 0))],
        out_specs=pl.BlockSpec((1, tr, GROUP_W), lambda b, i: (b, i, 0)),
        out_shape=jax.ShapeDtypeStruct((bsz, seq, GROUP_W), F32),
        compiler_params=_cp("arbitrary", "arbitrary"),
        name="s5_finalize",
    )(u_perm, y, y, lp["s5_d"].reshape(1, GROUP_W), wglu_bf, lp["s5_b_glu"].reshape(1, GROUP_W))
    out = out.reshape(bsz, lseg, S5_SEG, GROUP_W).transpose(0, 2, 1, 3).reshape(bsz, seq, GROUP_W)
    return out, fin


def _lru_body(xp_ref, xm_ref, xn_ref, cw_ref, cb_ref, wg_ref, bg_ref, sp_ref, h0_ref, h_ref, fin_ref,
              a_scr, b_scr, hc_scr, *, tile_rows, nt, seq):
    d = pl.program_id(1)
    i = pl.program_id(2)
    ti = jnp.where(d == 0, i, nt - 1 - i)

    @pl.when(i == 0)
    def _():
        hc_scr[...] = h0_ref[0, 0]

    n = tile_rows + 16
    xe = jnp.concatenate([xp_ref[0], xm_ref[0], xn_ref[0]], axis=0)
    rowid = lax.broadcasted_iota(jnp.int32, (n, 1), 0) + (ti * tile_rows - 8)
    xe = jnp.where((rowid >= 0) & (rowid < seq), xe, 0.0)
    cw = cw_ref[...]
    xc = (cb_ref[...]
          + cw[0:1] * pltpu.roll(xe, 2, 0)[8:8 + tile_rows]
          + cw[1:2] * pltpu.roll(xe, 1, 0)[8:8 + tile_rows]
          + cw[2:3] * xe[8:8 + tile_rows]
          + cw[3:4] * pltpu.roll(xe, n - 1, 0)[8:8 + tile_rows])
    for cb in range(4):
        lo, hi = cb * 256, (cb + 1) * 256
        xcb = xc[:, lo:hi]
        pre = _dot(xcb.astype(BF16), wg_ref[0, cb])
        r = jax.nn.sigmoid(pre[:, 0:256] + bg_ref[0, :, lo:hi])
        ig = jax.nn.sigmoid(pre[:, 256:512] + bg_ref[0, :, GROUP_W + lo:GROUP_W + hi])
        log_a = -LRU_C * sp_ref[0, :, lo:hi] * r
        a = jnp.exp(log_a)
        a_scr[:, lo:hi] = a
        b_scr[:, lo:hi] = jnp.sqrt(jnp.tanh(-log_a) * (a * a + 1.0)) * (ig * xcb)

    def step(s, h):
        t = jnp.where(d == 0, s, tile_rows - 1 - s)
        h = a_scr[pl.ds(t, 1), :] * h + b_scr[pl.ds(t, 1), :]
        b_scr[pl.ds(t, 1), :] = h
        return h

    h = lax.fori_loop(0, tile_rows, step, hc_scr[...], unroll=8)
    hc_scr[...] = h
    h_ref[0, 0] = b_scr[...]
    fin_ref[0, 0] = h


def _lru_params(lp):
    def blockdiag(w):
        t = w.reshape(2, 4, 4, 64, 64)
        t = jnp.einsum("dcgij,gh->dcgihj", t, jnp.eye(4, dtype=F32))
        return t.reshape(2, 4, 256, 256)

    wg = jnp.concatenate([blockdiag(lp["lru_w_a"]), blockdiag(lp["lru_w_x"])], axis=-1).astype(BF16)
    bg = jnp.concatenate([lp["lru_b_a"], lp["lru_b_x"]], axis=-1).reshape(2, 1, 2 * GROUP_W)
    sp = jax.nn.softplus(-lp["lru_lam"]).reshape(2, 1, GROUP_W)
    return wg, bg, sp


def lru_mixer(p3, h0, lp):
    bsz, seq, _ = p3.shape
    tr = min(512, seq)
    nt = seq // tr
    wg, bg, sp = _lru_params(lp)
    nb8 = seq // 8

    def tile(d, i):
        return jnp.where(d == 0, i, nt - 1 - i)

    return pl.pallas_call(
        functools.partial(_lru_body, tile_rows=tr, nt=nt, seq=seq),
        grid=(bsz, 2, nt),
        in_specs=[pl.BlockSpec((1, 8, GROUP_W), lambda b, d, i: (b, jnp.maximum(tile(d, i) * (tr // 8) - 1, 0), 1)),
                  pl.BlockSpec((1, tr, GROUP_W), lambda b, d, i: (b, tile(d, i), 1)),
                  pl.BlockSpec((1, 8, GROUP_W), lambda b, d, i: (b, jnp.minimum((tile(d, i) + 1) * (tr // 8), nb8 - 1), 1)),
                  pl.BlockSpec((4, GROUP_W), lambda b, d, i: (0, 0)),
                  pl.BlockSpec((1, GROUP_W), lambda b, d, i: (0, 0)),
                  pl.BlockSpec((1, 4, 256, 512), lambda b, d, i: (d, 0, 0, 0)),
                  pl.BlockSpec((1, 1, 2 * GROUP_W), lambda b, d, i: (d, 0, 0)),
                  pl.BlockSpec((1, 1, GROUP_W), lambda b, d, i: (d, 0, 0)),
                  pl.BlockSpec((1, 1, 1, GROUP_W), lambda b, d, i: (b, d, 0, 0))],
        out_specs=[pl.BlockSpec((1, 1, tr, GROUP_W), lambda b, d, i: (d, b, tile(d, i), 0)),
                   pl.BlockSpec((1, 1, 1, GROUP_W), lambda b, d, i: (b, d, 0, 0))],
        out_shape=[jax.ShapeDtypeStruct((2, bsz, seq, GROUP_W), F32),
                   jax.ShapeDtypeStruct((bsz, 2, 1, GROUP_W), F32)],
        scratch_shapes=[pltpu.VMEM((tr, GROUP_W), F32), pltpu.VMEM((tr, GROUP_W), F32), pltpu.VMEM((1, GROUP_W), F32)],
        compiler_params=_cp("arbitrary", "arbitrary", "arbitrary"),
        name="lru_scan",
    )(p3, p3, p3, lp["lru_conv_w"], lp["lru_conv_b"].reshape(1, GROUP_W), wg, bg, sp, h0)


def _log_sigmoid(z):
    return jnp.minimum(z, 0.0) - jnp.log1p(jnp.exp(-jnp.abs(z)))


def _gla_body(q_ref, k_ref, v_ref, lr_ref, wa_ref, ba_ref, tri_ref, s0_ref, o_ref, sfin_ref, s_scr, *, nch):
    d = pl.program_id(1)
    i = pl.program_id(3)
    c = GLA_CHUNK

    @pl.when(i == 0)
    def _():
        s_scr[...] = s0_ref[0, 0, 0]

    tri = tri_ref[0]

    def chunk(s, carry):
        ci = jnp.where(d == 0, s, nch - 1 - s)
        r0 = pl.multiple_of(ci * c, c)
        q = q_ref[0, pl.ds(r0, c), :] * (GLA_DK ** -0.5)
        k = k_ref[0, pl.ds(r0, c), :]
        v = v_ref[0, pl.ds(r0, c), :].astype(BF16)
        z = jnp.dot(lr_ref[0, pl.ds(r0, c), :], wa_ref[0, 0], preferred_element_type=F32, precision=HI) + ba_ref[0, 0]
        la = _log_sigmoid(z) / GLA_GATE_NORM
        b = jnp.dot(tri, la, preferred_element_type=F32, precision=HI)
        b_mid = jnp.where(d == 0, b[c // 2 - 1:c // 2], b[c // 2:c // 2 + 1])
        qd = (q * jnp.exp(b - b_mid)).astype(BF16)
        kd = (k * jnp.exp(b_mid - b)).astype(BF16)
        sc = lax.dot_general(qd, kd, (((1,), (1,)), ((), ())), preferred_element_type=F32) * tri
        intra = _dot(sc.astype(BF16), v)
        st = s_scr[...]
        inter = _dot((q * jnp.exp(b)).astype(BF16), st.astype(BF16))
        o_ref[0, 0, pl.ds(r0, c), :] = intra + inter
        kt = k.T
        bt = b.T
        bt_last = jnp.where(d == 0, bt[:, c - 1:c], bt[:, 0:1])
        k2t = (kt * jnp.exp(bt_last - bt)).astype(BF16)
        s_scr[...] = jnp.exp(bt_last) * st + _dot(k2t, v)
        return carry

    lax.fori_loop(0, nch, chunk, 0)
    sfin_ref[0, 0, 0] = s_scr[...]


def gla_mixer(p3, s0, lp):
    bsz, seq, _ = p3.shape
    tr = min(512, seq)
    nt = seq // tr
    wa = lp["gla_w_alpha"].reshape(2, 16, GLA_HEADS, GLA_DK).transpose(0, 2, 1, 3)
    wa_pad = jnp.zeros((2, GLA_HEADS, 128, GLA_DK), F32)
    wa_pad = wa_pad.at[0, :, 0:16].set(wa[0]).at[1, :, 16:32].set(wa[1])
    ba = lp["gla_b_alpha"].reshape(2, GLA_HEADS, 1, GLA_DK)
    lower = np.tril(np.ones((GLA_CHUNK, GLA_CHUNK), np.float32))
    tri = jnp.asarray(np.stack([lower, lower.T]))

    def tile(d, i):
        return jnp.where(d == 0, i, nt - 1 - i)

    return pl.pallas_call(
        functools.partial(_gla_body, nch=tr // GLA_CHUNK),
        grid=(bsz, 2, GLA_HEADS, nt),
        in_specs=[pl.BlockSpec((1, tr, 128), lambda b, d, h, i: (b, tile(d, i), 24 + h)),
                  pl.BlockSpec((1, tr, 128), lambda b, d, h, i: (b, tile(d, i), 28 + h)),
                  pl.BlockSpec((1, tr, 256), lambda b, d, h, i: (b, tile(d, i), 16 + h)),
                  pl.BlockSpec((1, tr, 128), lambda b, d, h, i: (b, tile(d, i), 72)),
                  pl.BlockSpec((1, 1, 128, GLA_DK), lambda b, d, h, i: (d, h, 0, 0)),
                  pl.BlockSpec((1, 1, 1, GLA_DK), lambda b, d, h, i: (d, h, 0, 0)),
                  pl.BlockSpec((1, GLA_CHUNK, GLA_CHUNK), lambda b, d, h, i: (d, 0, 0)),
                  pl.BlockSpec((1, 1, 1, GLA_DK, GLA_DV), lambda b, d, h, i: (b, d, h, 0, 0))],
        out_specs=[pl.BlockSpec((1, 1, tr, GLA_DV), lambda b, d, h, i: (d, b, tile(d, i), h)),
                   pl.BlockSpec((1, 1, 1, GLA_DK, GLA_DV), lambda b, d, h, i: (b, d, h, 0, 0))],
        out_shape=[jax.ShapeDtypeStruct((2, bsz, seq, GROUP_W), F32),
                   jax.ShapeDtypeStruct((bsz, 2, GLA_HEADS, GLA_DK, GLA_DV), F32)],
        scratch_shapes=[pltpu.VMEM((GLA_DK, GLA_DV), F32)],
        compiler_params=_cp("arbitrary", "arbitrary", "arbitrary", "arbitrary"),
        name="gla_scan",
    )(p3, p3, p3, p3, wa_pad, ba, tri, s0)


def _hy_conv3_body(x_ref, w_ref, b_ref, o_ref, *, seq):
    x = x_ref[0]
    w = w_ref[...]
    row = lax.broadcasted_iota(jnp.int32, (seq, 1), 0)
    xm = jnp.where(row == 0, 0.0, pltpu.roll(x, 1, 0))
    xp = jnp.where(row == seq - 1, 0.0, pltpu.roll(x, seq - 1, 0))
    o_ref[0] = w[0:1] * xm + w[1:2] * x + w[2:3] * xp + b_ref[...]


def hy_conv3(p3, lp):
    bsz, seq, _ = p3.shape
    w3 = 3 * GROUP_W
    return pl.pallas_call(
        functools.partial(_hy_conv3_body, seq=seq),
        grid=(bsz, w3 // 128),
        in_specs=[pl.BlockSpec((1, seq, 128), lambda b, c: (b, 0, 48 + c)),
                  pl.BlockSpec((3, 128), lambda b, c: (0, c)),
                  pl.BlockSpec((1, 128), lambda b, c: (0, c))],
        out_specs=pl.BlockSpec((1, seq, 128), lambda b, c: (b, 0, c)),
        out_shape=jax.ShapeDtypeStruct((bsz, seq, w3), F32),
        compiler_params=_cp("arbitrary", "arbitrary"),
        name="hy_conv3",
    )(p3, lp["hy_conv_w"], lp["hy_conv_b"].reshape(1, w3))


def _hy_filter_body(bv_ref, w1_ref, b1_ref, f0_ref, w2_ref, b2_ref, f1_ref, w3_ref, dl_ref, o_ref, *, tr, seq):
    i = pl.program_id(0)
    n = 2 * seq
    t = lax.broadcasted_iota(jnp.int32, (tr, 1), 0) + i * tr
    pos = jnp.where(t < seq, t, n - t).astype(F32)
    tt = pos / seq
    w = (2.0 * math.pi) * pos / seq
    lane = lax.broadcasted_iota(jnp.int32, (tr, 128), 1)
    arg = w * bv_ref[...]
    feats = jnp.where(lane == 0, tt,
                      jnp.where(lane <= HY_BANDS, jnp.cos(arg),
                                jnp.where(lane <= 2 * HY_BANDS, -jnp.sin(arg), 0.0)))
    h = jnp.sin(f0_ref[...] * (jnp.dot(feats, w1_ref[...], preferred_element_type=F32, precision=HI) + b1_ref[...]))
    h = jnp.sin(f1_ref[...] * (jnp.dot(h, w2_ref[...], preferred_element_type=F32, precision=HI) + b2_ref[...]))
    out = jnp.dot(h, w3_ref[0], preferred_element_type=F32, precision=HI)
    out = out * jnp.exp(-tt * dl_ref[...])
    o_ref[...] = jnp.where(t == seq, 0.0, out)


def hy_filter(lp, seq):
    n = 2 * seq
    tr = min(512, seq)
    nt = n // tr
    bands = np.linspace(1e-4, HY_BANDS - 1, HY_BANDS, dtype=np.float32)
    bv = np.zeros((1, 128), np.float32)
    bv[0, 1:1 + HY_BANDS] = bands
    bv[0, 1 + HY_BANDS:1 + 2 * HY_BANDS] = bands
    deltas = np.abs(np.linspace(math.log(1e-2) / 0.3, math.log(1e-2) / 1.5, GROUP_W, dtype=np.float32))
    dl = np.concatenate([deltas, deltas])[None, :]

    def pad2(w, r, c):
        return jnp.zeros((r, c), F32).at[:w.shape[0], :w.shape[1]].set(w)

    w1 = pad2(lp["hy_w1"], 128, 128)
    b1 = pad2(lp["hy_b1"][None, :], 1, 128)
    f0 = pad2(lp["hy_freq"][0][None, :], 1, 128)
    w2 = pad2(lp["hy_w2"], 128, 128)
    b2 = pad2(lp["hy_b2"][None, :], 1, 128)
    f1 = pad2(lp["hy_freq"][1][None, :], 1, 128)
    w3 = lp["hy_w3"].reshape(HY_FFN, 2, 2 * GROUP_W).transpose(1, 0, 2)
    w3 = jnp.zeros((2, 128, 2 * GROUP_W), F32).at[:, :HY_FFN].set(w3)
    half = nt // 2
    vec = lambda i: (0, 0)
    return pl.pallas_call(
        functools.partial(_hy_filter_body, tr=tr, seq=seq),
        grid=(nt,),
        in_specs=[pl.BlockSpec((1, 128), vec), pl.BlockSpec((128, 128), vec), pl.BlockSpec((1, 128), vec),
                  pl.BlockSpec((1, 128), vec), pl.BlockSpec((128, 128), vec), pl.BlockSpec((1, 128), vec),
                  pl.BlockSpec((1, 128), vec),
                  pl.BlockSpec((1, 128, 2 * GROUP_W), lambda i: (jnp.where(i < half, 0, 1), 0, 0)),
                  pl.BlockSpec((1, 2 * GROUP_W), vec)],
        out_specs=pl.BlockSpec((tr, 2 * GROUP_W), lambda i: (i, 0)),
        out_shape=jax.ShapeDtypeStruct((n, 2 * GROUP_W), F32),
        compiler_params=_cp("arbitrary"),
        name="hy_filter",
    )(jnp.asarray(bv), w1, b1, f0, w2, b2, f1, w3, jnp.asarray(dl))


HY_N1 = 128
HY_N2 = 128


@functools.lru_cache(maxsize=None)
def _dft_tables(t1_used):
    n = HY_N1 * HY_N2
    k1 = np.arange(HY_N1)[None, :, None]
    t1 = np.arange(t1_used)[None, None, :]
    t2 = np.arange(HY_N2)[:, None, None]
    ph = 2.0 * np.pi * ((k1 * (HY_N2 * t1 + t2)) % n) / n
    fa = np.concatenate([np.cos(ph), -np.sin(ph)], axis=1)
    ga = np.concatenate([np.cos(ph).transpose(0, 2, 1), -np.sin(ph).transpose(0, 2, 1)], axis=2) / n
    kk = np.arange(HY_N2)
    ph2 = 2.0 * np.pi * ((kk[:, None] * kk[None, :]) % HY_N2) / HY_N2
    cm, sm = np.cos(ph2), np.sin(ph2)
    fb = np.block([[cm, sm], [-sm, cm]])
    fbi = np.block([[cm, -sm], [sm, cm]])
    return fa, ga, fb, fbi


def _hy_stage_a_body(z_ref, fh_ref, fl_ref, o_ref):
    for j in range(8):
        x = z_ref[0, :, j, :]
        res = _dot3(fh_ref[j], fl_ref[j], x)
        o_ref[0, 0, :, j, :] = res[0:HY_N1]
        o_ref[0, 1, :, j, :] = res[HY_N1:2 * HY_N1]


def hy_stage_a(z4, t1_used, ch0, nch, cb=512):
    bz = z4.shape[0]
    fa, _, _, _ = _dft_tables(t1_used)
    fh, fl = _np_split(fa)
    cb0 = ch0 // cb
    return pl.pallas_call(
        _hy_stage_a_body,
        grid=(bz, HY_N2 // 8, nch // cb),
        in_specs=[pl.BlockSpec((1, t1_used, 8, cb), lambda b, g, c: (b, 0, g, cb0 + c)),
                  pl.BlockSpec((8, 2 * HY_N1, t1_used), lambda b, g, c: (g, 0, 0)),
                  pl.BlockSpec((8, 2 * HY_N1, t1_used), lambda b, g, c: (g, 0, 0))],
        out_specs=pl.BlockSpec((1, 2, HY_N1, 8, cb), lambda b, g, c: (b, 0, 0, g, c)),
        out_shape=jax.ShapeDtypeStruct((bz, 2, HY_N1, HY_N2, nch), F32),
        compiler_params=_cp("arbitrary", "arbitrary", "arbitrary"),
        name="hy_stage_a",
    )(z4, fh, fl)


def _hy_stage_b_body(a_ref, fh_ref, fl_ref, o_ref, *, cb):
    x = a_ref[0, :, 0].reshape(2 * HY_N2, cb)
    o_ref[0, :, 0] = _dot3(fh_ref[...], fl_ref[...], x).reshape(2, HY_N2, cb)


def hy_stage_b(a5, cb=512):
    bz, _, _, _, nch = a5.shape
    _, _, fb, _ = _dft_tables(HY_N1)
    fh, fl = _np_split(fb)
    mat = pl.BlockSpec((2 * HY_N2, 2 * HY_N2), lambda b, k, c: (0, 0))
    blk = pl.BlockSpec((1, 2, 1, HY_N2, cb), lambda b, k, c: (b, 0, k, 0, c))
    return pl.pallas_call(
        functools.partial(_hy_stage_b_body, cb=cb),
        grid=(bz, HY_N1, nch // cb),
        in_specs=[blk, mat, mat],
        out_specs=blk,
        out_shape=jax.ShapeDtypeStruct(a5.shape, F32),
        compiler_params=_cp("arbitrary", "arbitrary", "arbitrary"),
        name="hy_stage_b",
    )(a5, fh, fl)


def _hy_stage_bb_body(a_ref, h_ref, fh_ref, fl_ref, gh_ref, gl_ref, o_ref, *, cb):
    x = a_ref[0, :, 0].reshape(2 * HY_N2, cb)
    z = _dot3(fh_ref[...], fl_ref[...], x)
    zr, zi = z[0:HY_N2], z[HY_N2:]
    hr, hi = h_ref[0, 0, 0], h_ref[0, 1, 0]
    y = jnp.concatenate([zr * hr - zi * hi, zr * hi + zi * hr], axis=0)
    o_ref[0, :, 0] = _dot3(gh_ref[...], gl_ref[...], y).reshape(2, HY_N2, cb)


def hy_stage_bb(a5, hspec, order, cb=512):
    bz, _, _, _, nch = a5.shape
    _, _, fb, fbi = _dft_tables(HY_N1)
    fh, fl = _np_split(fb)
    gh, gl = _np_split(fbi)
    hb0 = order * (GROUP_W // cb)
    mat = pl.BlockSpec((2 * HY_N2, 2 * HY_N2), lambda c, k, b: (0, 0))
    blk = pl.BlockSpec((1, 2, 1, HY_N2, cb), lambda c, k, b: (b, 0, k, 0, c))
    return pl.pallas_call(
        functools.partial(_hy_stage_bb_body, cb=cb),
        grid=(nch // cb, HY_N1, bz),
        in_specs=[blk,
                  pl.BlockSpec((1, 2, 1, HY_N2, cb), lambda c, k, b: (0, 0, k, 0, hb0 + c)),
                  mat, mat, mat, mat],
        out_specs=blk,
        out_shape=jax.ShapeDtypeStruct(a5.shape, F32),
        compiler_params=_cp("arbitrary", "arbitrary", "arbitrary"),
        name="hy_stage_bb",
    )(a5, hspec, fh, fl, gh, gl)


def _hy_stage_ai_body(b_ref, gh_ref, gl_ref, u_ref, g_ref, bias_ref, o_ref):
    for j in range(8):
        x = jnp.concatenate([b_ref[0, 0, :, j, :], b_ref[0, 1, :, j, :]], axis=0)
        y = _dot3(gh_ref[j], gl_ref[j], x)
        o_ref[0, :, j, :] = g_ref[0, :, j, :] * (y + bias_ref[...] * u_ref[0, :, j, :])


def hy_stage_ai(b5, u4, uc0, g4, gc0, bias, t1_used, cb=512):
    bz, _, _, _, nch = b5.shape
    _, ga, _, _ = _dft_tables(t1_used)
    gh, gl = _np_split(ga)
    ub0, gb0 = uc0 // cb, gc0 // cb
    return pl.pallas_call(
        _hy_stage_ai_body,
        grid=(bz, HY_N2 // 8, nch // cb),
        in_specs=[pl.BlockSpec((1, 2, HY_N1, 8, cb), lambda b, g, c: (b, 0, 0, g, c)),
                  pl.BlockSpec((8, t1_used, 2 * HY_N1), lambda b, g, c: (g, 0, 0)),
                  pl.BlockSpec((8, t1_used, 2 * HY_N1), lambda b, g, c: (g, 0, 0)),
                  pl.BlockSpec((1, t1_used, 8, cb), lambda b, g, c: (b, 0, g, ub0 + c)),
                  pl.BlockSpec((1, t1_used, 8, cb), lambda b, g, c: (b, 0, g, gb0 + c)),
                  pl.BlockSpec((1, cb), lambda b, g, c: (0, c))],
        out_specs=pl.BlockSpec((1, t1_used, 8, cb), lambda b, g, c: (b, 0, g, c)),
        out_shape=jax.ShapeDtypeStruct((bz, t1_used, HY_N2, nch), F32),
        compiler_params=_cp("arbitrary", "arbitrary", "arbitrary"),
        name="hy_stage_ai",
    )(b5, gh, gl, u4, g4, bias)


@functools.lru_cache(maxsize=None)
def _dense_dft_tables(seq):
    n = 2 * seq
    k = np.arange(n)[:, None]
    t = np.arange(n)[None, :]
    ph = 2.0 * np.pi * ((k * t) % n) / n
    fwd = np.concatenate([np.cos(ph), -np.sin(ph)], axis=0)
    inv = np.concatenate([np.cos(ph).T, -np.sin(ph).T], axis=1)[:seq] / n
    return fwd, inv


def _hy_dense_spec_body(c_ref, fh_ref, fl_ref, o_ref):
    o_ref[...] = _dot3(fh_ref[...], fl_ref[...], c_ref[...])


def _hy_dense_body(z_ref, h_ref, fh_ref, fl_ref, gh_ref, gl_ref, g_ref, bias_ref, o_ref, *, n):
    u = z_ref[0]
    z = _dot3(fh_ref[...], fl_ref[...], u)
    zr, zi = z[0:n], z[n:]
    hr, hi = h_ref[0:n], h_ref[n:2 * n]
    y = jnp.concatenate([zr * hr - zi * hi, zr * hi + zi * hr], axis=0)
    conv = _dot3(gh_ref[...], gl_ref[...], y)
    o_ref[0] = g_ref[0] * (conv + bias_ref[...] * u)


def hy_dense(hzc, circ, lp, cb=512):
    bsz, seq, _ = hzc.shape
    n = 2 * seq
    fwd, inv = _dense_dft_tables(seq)
    fh, fl = _np_split(fwd)
    gh, gl = _np_split(inv)
    hspec = pl.pallas_call(
        _hy_dense_spec_body,
        grid=(2 * GROUP_W // cb,),
        in_specs=[pl.BlockSpec((n, cb), lambda c: (0, c)),
                  pl.BlockSpec((2 * n, n), lambda c: (0, 0)),
                  pl.BlockSpec((2 * n, n), lambda c: (0, 0))],
        out_specs=pl.BlockSpec((2 * n, cb), lambda c: (0, c)),
        out_shape=jax.ShapeDtypeStruct((2 * n, 2 * GROUP_W), F32),
        compiler_params=_cp("arbitrary"),
        name="hy_dense_spec",
    )(circ, fh, fl)
    fh_in, fl_in = fh[:, :seq], fl[:, :seq]
    u, uc0 = hzc, 0
    ncb = GROUP_W // cb
    for o in range(2):
        gc0 = (1 + o) * ncb
        u = pl.pallas_call(
            functools.partial(_hy_dense_body, n=n),
            grid=(bsz, ncb),
            in_specs=[pl.BlockSpec((1, seq, cb), lambda b, c, uc0=uc0: (b, 0, uc0 + c)),
                      pl.BlockSpec((2 * n, cb), lambda b, c, o=o: (0, o * ncb + c)),
                      pl.BlockSpec((2 * n, seq), lambda b, c: (0, 0)),
                      pl.BlockSpec((2 * n, seq), lambda b, c: (0, 0)),
                      pl.BlockSpec((seq, 2 * n), lambda b, c: (0, 0)),
                      pl.BlockSpec((seq, 2 * n), lambda b, c: (0, 0)),
                      pl.BlockSpec((1, seq, cb), lambda b, c, gc0=gc0: (b, 0, gc0 + c)),
                      pl.BlockSpec((1, cb), lambda b, c: (0, c))],
            out_specs=pl.BlockSpec((1, seq, cb), lambda b, c: (b, 0, c)),
            out_shape=jax.ShapeDtypeStruct((bsz, seq, GROUP_W), F32),
            compiler_params=_cp("arbitrary", "arbitrary"),
            name="hy_dense_conv",
        )(u, hspec, fh_in, fl_in, gh, gl, hzc, lp["hy_bias"][o].reshape(1, GROUP_W))
        uc0 = 0
    return u


def hyena_mixer(p3, lp):
    bsz, seq, _ = p3.shape
    hzc = hy_conv3(p3, lp)
    circ = hy_filter(lp, seq)
    if 2 * seq != HY_N1 * HY_N2:
        return hy_dense(hzc, circ, lp)
    t1_used = seq // HY_N2
    hspec = hy_stage_b(hy_stage_a(circ.reshape(1, HY_N1, HY_N2, 2 * GROUP_W), HY_N1, 0, 2 * GROUP_W))
    hz4 = hzc.reshape(bsz, t1_used, HY_N2, 3 * GROUP_W)
    u4, uc0 = hz4, 0
    for o in range(2):
        a5 = hy_stage_a(u4, t1_used, uc0, GROUP_W)
        b5 = hy_stage_bb(a5, hspec, o)
        u4 = hy_stage_ai(b5, u4, uc0, hz4, (1 + o) * GROUP_W, lp["hy_bias"][o].reshape(1, GROUP_W), t1_used)
        uc0 = 0
    return u4.reshape(bsz, seq, GROUP_W)


def _rms(y, g):
    return y * lax.rsqrt(jnp.mean(y * y, axis=-1, keepdims=True) + EPS) * g


def _mix_body(s5_ref, hf_ref, hb_ref, lg_ref, of_ref, ob_ref, gg_ref, hy_ref, gn_ref, mg_ref, o_ref):
    w = GROUP_W
    o_ref[:, 0:w] = _rms(s5_ref[...], mg_ref[:, 0:w]).astype(o_ref.dtype)
    y_lru = (hf_ref[0] + hb_ref[0]) * jax.nn.gelu(lg_ref[...])
    o_ref[:, w:2 * w] = _rms(y_lru, mg_ref[:, w:2 * w]).astype(o_ref.dtype)
    o = of_ref[0] + ob_ref[0]
    gg = gg_ref[...]
    heads = []
    for h in range(GLA_HEADS):
        sl = slice(h * GLA_DV, (h + 1) * GLA_DV)
        heads.append(_rms(o[:, sl], gn_ref[...]) * (gg[:, sl] * jax.nn.sigmoid(gg[:, sl])))
    y_gla = jnp.concatenate(heads, axis=-1)
    o_ref[:, 2 * w:3 * w] = _rms(y_gla, mg_ref[:, 2 * w:3 * w]).astype(o_ref.dtype)
    o_ref[:, 3 * w:4 * w] = _rms(hy_ref[...], mg_ref[:, 3 * w:4 * w]).astype(o_ref.dtype)


def mix_assemble(p2, y_s5, h_lru, o_gla, y_hy, lp, tm=256):
    m = p2.shape[0]
    w = GROUP_W
    row = lambda i: (i, 0)
    return pl.pallas_call(
        _mix_body,
        grid=(m // tm,),
        in_specs=[pl.BlockSpec((tm, w), row),
                  pl.BlockSpec((1, tm, w), lambda i: (0, i, 0)),
                  pl.BlockSpec((1, tm, w), lambda i: (1, i, 0)),
                  pl.BlockSpec((tm, w), lambda i: (i, 2)),
                  pl.BlockSpec((1, tm, w), lambda i: (0, i, 0)),
                  pl.BlockSpec((1, tm, w), lambda i: (1, i, 0)),
                  pl.BlockSpec((tm, w), lambda i: (i, 5)),
                  pl.BlockSpec((tm, w), row),
                  pl.BlockSpec((1, GLA_DV), lambda i: (0, 0)),
                  pl.BlockSpec((1, 4 * w), lambda i: (0, 0))],
        out_specs=pl.BlockSpec((tm, 4 * w), row),
        out_shape=jax.ShapeDtypeStruct((m, 4 * w), BF16),
        compiler_params=_cp("arbitrary"),
        name="mix_assemble",
    )(y_s5.reshape(m, w), h_lru.reshape(2, m, w), h_lru.reshape(2, m, w), p2,
      o_gla.reshape(2, m, w), o_gla.reshape(2, m, w), p2, y_hy.reshape(m, w),
      lp["gla_norm_g"].reshape(1, GLA_DV), lp["mix_norm_g"].reshape(1, 4 * w))


def _ffn1_body(hp_ref, hm_ref, hn_ref, wg_ref, wu_ref, cw_ref, cb_ref, o_ref, *, tm, gw, tps):
    i = pl.program_id(0)
    wg = wg_ref[...]
    hm = hm_ref[...]
    gm = _dot(hm, wg)
    gp = _dot(hp_ref[...], wg)
    gn = _dot(hn_ref[...], wg)
    gp = jnp.where(i % tps == 0, 0.0, gp)
    gn = jnp.where(i % tps == tps - 1, 0.0, gn)
    g = jnp.concatenate([gp, gm, gn], axis=0)
    n = tm + 2 * gw
    col = lax.broadcasted_iota(jnp.int32, (n, 1), 0) % gw
    gl = jnp.where(col == 0, 0.0, pltpu.roll(g, 1, 0))
    gr = jnp.where(col == gw - 1, 0.0, pltpu.roll(g, n - 1, 0))
    cw = cw_ref[...]
    acc = cb_ref[...] + jnp.zeros((tm, g.shape[1]), F32)
    for dr in range(3):
        base = dr * gw
        acc = acc + cw[3 * dr:3 * dr + 1] * gl[base:base + tm]
        acc = acc + cw[3 * dr + 1:3 * dr + 2] * g[base:base + tm]
        acc = acc + cw[3 * dr + 2:3 * dr + 3] * gr[base:base + tm]
    up = _dot(hm, wu_ref[...])
    o_ref[...] = (acc * jax.nn.sigmoid(acc) * up).astype(o_ref.dtype)


def ffn1(h2, wg, wu, cw, cb, layer, *, seq, gw, tm, tn=256):
    m, d = h2.shape
    ff = wg.shape[-1]
    tps = seq // tm
    hpb = tm // gw
    nhb = m // gw
    return pl.pallas_call(
        functools.partial(_ffn1_body, tm=tm, gw=gw, tps=tps),
        grid=(m // tm, ff // tn),
        in_specs=[pl.BlockSpec((gw, d), lambda i, j: (jnp.maximum(i * hpb - 1, 0), 0)),
                  pl.BlockSpec((tm, d), lambda i, j: (i, 0)),
                  pl.BlockSpec((gw, d), lambda i, j: (jnp.minimum((i + 1) * hpb, nhb - 1), 0)),
                  pl.BlockSpec((None, d, tn), lambda i, j: (layer, 0, j)),
                  pl.BlockSpec((None, d, tn), lambda i, j: (layer, 0, j)),
                  pl.BlockSpec((None, 9, tn), lambda i, j: (layer, 0, j)),
                  pl.BlockSpec((None, 1, tn), lambda i, j: (layer, 0, j))],
        out_specs=pl.BlockSpec((tm, tn), lambda i, j: (i, j)),
        out_shape=jax.ShapeDtypeStruct((m, ff), BF16),
        compiler_params=_cp("arbitrary", "arbitrary"),
        name="ffn_gate_up",
    )(h2, h2, h2, wg, wu, cw, cb)


def _token_mixers(p2, bsz, seq, states, lp, wglu_bf, layer, need_output):
    p3 = p2.reshape(bsz, seq, N_COL_PAD)
    y_s5, st_s5 = s5_mixer(p3, states[0], lp, wglu_bf, layer, need_output)
    h_lru, st_lru = lru_mixer(p3, states[1], lp)
    o_gla, st_gla = gla_mixer(p3, states[2], lp)
    new_states = (st_s5, st_lru, st_gla)
    if not need_output:
        return None, new_states
    y_hy = hyena_mixer(p3, lp)
    return mix_assemble(p2, y_s5, h_lru, o_gla, y_hy, lp), new_states


def kernel(x, c, ctx, c_ctx, w_ada, b_ada, norm_mix_g, norm_mlp_g, w_in, s5_lam_re, s5_lam_im, s5_log_step, s5_b_re, s5_b_im, s5_c_re, s5_c_im, s5_d, s5_w_glu, s5_b_glu, lru_conv_w, lru_conv_b, lru_w_a, lru_b_a, lru_w_x, lru_b_x, lru_lam, gla_w_alpha, gla_b_alpha, gla_norm_g, hy_conv_w, hy_conv_b, hy_w1, hy_b1, hy_w2, hy_b2, hy_w3, hy_freq, hy_bias, mix_norm_g, w_out, mlp_w_gate, mlp_w_up, mlp_conv_w, mlp_conv_b, mlp_w_down, final_norm_g):
    bsz, seq, d = x.shape
    clen = ctx.shape[1]
    depth = w_ada.shape[0]
    grid_w = 64
    params = dict(
        s5_lam_re=s5_lam_re, s5_lam_im=s5_lam_im, s5_log_step=s5_log_step, s5_b_re=s5_b_re, s5_b_im=s5_b_im,
        s5_c_re=s5_c_re, s5_c_im=s5_c_im, s5_d=s5_d, s5_b_glu=s5_b_glu,
        lru_conv_w=lru_conv_w, lru_conv_b=lru_conv_b, lru_w_a=lru_w_a, lru_b_a=lru_b_a, lru_w_x=lru_w_x,
        lru_b_x=lru_b_x, lru_lam=lru_lam, gla_w_alpha=gla_w_alpha, gla_b_alpha=gla_b_alpha, gla_norm_g=gla_norm_g,
        hy_conv_w=hy_conv_w, hy_conv_b=hy_conv_b, hy_w1=hy_w1, hy_b1=hy_b1, hy_w2=hy_w2, hy_b2=hy_b2, hy_w3=hy_w3,
        hy_freq=hy_freq, hy_bias=hy_bias, mix_norm_g=mix_norm_g)

    w_in_bf = jnp.concatenate(
        [w_in[..., 0:6144], w_in[..., 6176:9248], w_in[..., 6144:6176],
         jnp.zeros((depth, d, N_COL_PAD - 9248), w_in.dtype)], axis=-1).astype(BF16)
    w_out_bf = w_out.astype(BF16)
    wg_bf = mlp_w_gate.astype(BF16)
    wu_bf = mlp_w_up.astype(BF16)
    wd_bf = mlp_w_down.astype(BF16)
    wglu_bf = s5_w_glu.astype(BF16)
    conv_w9 = mlp_conv_w.reshape(depth, 9, D_FF)
    conv_b = mlp_conv_b.reshape(depth, 1, D_FF)

    cvec = jnp.zeros((8, d), F32).at[0:bsz].set(c).at[bsz].set(c_ctx)
    mod = ada_mod(cvec, w_ada, b_ada)

    x2 = x.reshape(bsz * seq, d)
    c2 = ctx.reshape(bsz * clen, d)
    zero_states = (jnp.zeros((bsz, 2, S5_JB, 1, 1024), F32),
                   jnp.zeros((bsz, 2, 1, GROUP_W), F32),
                   jnp.zeros((bsz, 2, GLA_HEADS, GLA_DK, GLA_DV), F32))

    for l in range(depth):
        last = l == depth - 1
        lp = {k: v[l] for k, v in params.items()}
        mx = mod[l, 0:bsz].reshape(bsz, 1, 6, d)
        mc = mod[l, bsz:bsz + 1].reshape(1, 1, 6, d)
        sh1, sc1, g1, sh2, sc2, g2 = (mx[:, :, i] for i in range(6))
        csh1, csc1, cg1, csh2, csc2, cg2 = (mc[:, :, i] for i in range(6))

        hc = normmod(c2, norm_mix_g[l], csh1, csc1, bsz * clen, BF16)
        pc = matmul(hc, w_in_bf, l, tm=bsz * clen, tn=512, name="in_proj_ctx")
        yc, ctx_states = _token_mixers(pc, bsz, clen, zero_states, lp, wglu_bf, l, need_output=not last)

        hx = normmod(x2, norm_mix_g[l], sh1, sc1, seq, BF16)
        px = matmul(hx, w_in_bf, l, tm=1024, tn=512, name="in_proj")
        yx, _ = _token_mixers(px, bsz, seq, ctx_states, lp, wglu_bf, l, need_output=True)
        x2 = matmul(yx, w_out_bf, l, tm=1024, tn=512, res=x2, gate=g1, rows_per_gate=seq, name="out_proj")
        h2 = normmod(x2, norm_mlp_g[l], sh2, sc2, seq, BF16)
        act = ffn1(h2, wg_bf, wu_bf, conv_w9, conv_b, l, seq=seq, gw=grid_w, tm=512)
        x2 = matmul(act, wd_bf, l, tm=512, tn=256, res=x2, gate=g2, rows_per_gate=seq, name="down_proj")

        if not last:
            c2 = matmul(yc, w_out_bf, l, tm=bsz * clen, tn=512, res=c2, gate=cg1, rows_per_gate=bsz * clen,
                        name="out_proj_ctx")
            hc2 = normmod(c2, norm_mlp_g[l], csh2, csc2, bsz * clen, BF16)
            actc = ffn1(hc2, wg_bf, wu_bf, conv_w9, conv_b, l, seq=clen, gw=clen, tm=clen)
            c2 = matmul(actc, wd_bf, l, tm=bsz * clen, tn=512, res=c2, gate=cg2, rows_per_gate=bsz * clen,
                        name="down_proj_ctx")

    zeros = jnp.zeros((1, 1, d), F32)
    out = normmod(x2, final_norm_g, zeros, zeros, bsz * seq, F32)
    return out.reshape(bsz, seq, d)
```

```python
import functools
import math

import numpy as np
import jax
import jax.numpy as jnp
from jax import lax
from jax.experimental import pallas as pl
from jax.experimental.pallas import tpu as pltpu

F32 = jnp.float32
BF16 = jnp.bfloat16
HI = lax.Precision.HIGHEST

EPS = 1e-6
GROUP_W = 1024
N_COL_PAD = 9728
S5_GROUPS = 64
S5_CH = 16
S5_STATE = 64
S5_SEG = 8
S5_JB = 8
LRU_HEADS = 16
LRU_C = 8.0
GLA_HEADS = 4
GLA_DK = 128
GLA_DV = 256
GLA_CHUNK = 64
GLA_GATE_NORM = 16.0
HY_BANDS = 16
HY_FFN = 64
D_FF = 11008
V7X_VMEM_LIMIT = 56 * 1024 * 1024


def _cp(*sem):
    return pltpu.CompilerParams(dimension_semantics=sem, vmem_limit_bytes=V7X_VMEM_LIMIT)


def _dot(a, b):
    return jnp.dot(a, b, preferred_element_type=F32)


def _split(x):
    hi = x.astype(BF16)
    lo = (x - hi.astype(F32)).astype(BF16)
    return hi, lo


def _dot3(fh, fl, x):
    xh, xl = _split(x)
    return _dot(fh, xh) + _dot(fh, xl) + _dot(fl, xh)


def _np_split(a):
    bf = jnp.dtype(BF16)
    hi = np.asarray(a, np.float64).astype(bf)
    lo = (np.asarray(a, np.float64) - hi.astype(np.float64)).astype(bf)
    return jnp.asarray(hi), jnp.asarray(lo)


def _ada_body(c_ref, w_ref, b_ref, o_ref):
    c = c_ref[...]
    a = (c * jax.nn.sigmoid(c)).astype(BF16)
    o_ref[...] = _dot(a, w_ref[...].astype(BF16)) + b_ref[...]


def ada_mod(cvec, w_ada, b_ada, tn=512):
    depth, d, n = w_ada.shape
    return pl.pallas_call(
        _ada_body,
        grid=(depth, n // tn),
        in_specs=[pl.BlockSpec((8, d), lambda l, j: (0, 0)),
                  pl.BlockSpec((None, d, tn), lambda l, j: (l, 0, j)),
                  pl.BlockSpec((None, 1, tn), lambda l, j: (l, 0, j))],
        out_specs=pl.BlockSpec((None, 8, tn), lambda l, j: (l, 0, j)),
        out_shape=jax.ShapeDtypeStruct((depth, 8, n), F32),
        compiler_params=_cp("arbitrary", "arbitrary"),
        name="ada_mod",
    )(cvec, w_ada, b_ada.reshape(depth, 1, n))


def _normmod_body(x_ref, g_ref, sh_ref, sc_ref, o_ref):
    x = x_ref[...]
    y = x * lax.rsqrt(jnp.mean(x * x, axis=-1, keepdims=True) + EPS) * g_ref[...]
    o_ref[...] = (y * (1.0 + sc_ref[0]) + sh_ref[0]).astype(o_ref.dtype)


def normmod(x2d, g, sh, sc, rows_per_mod, out_dtype, tm=256):
    m, d = x2d.shape
    tpm = rows_per_mod // tm
    return pl.pallas_call(
        _normmod_body,
        grid=(m // tm,),
        in_specs=[pl.BlockSpec((tm, d), lambda i: (i, 0)),
                  pl.BlockSpec((1, d), lambda i: (0, 0)),
                  pl.BlockSpec((1, 1, d), lambda i: (i // tpm, 0, 0)),
                  pl.BlockSpec((1, 1, d), lambda i: (i // tpm, 0, 0))],
        out_specs=pl.BlockSpec((tm, d), lambda i: (i, 0)),
        out_shape=jax.ShapeDtypeStruct((m, d), out_dtype),
        compiler_params=_cp("arbitrary"),
        name="normmod",
    )(x2d, g.reshape(1, d), sh, sc)


def _mm_body(*refs, nk, has_res):
    if has_res:
        a_ref, w_ref, res_ref, gate_ref, o_ref = refs[:5]
        scr = refs[5:]
    else:
        a_ref, w_ref, o_ref = refs[:3]
        scr = refs[3:]

    def epilogue(acc):
        if has_res:
            o_ref[...] = res_ref[...] + gate_ref[0] * acc
        else:
            o_ref[...] = acc.astype(o_ref.dtype)

    if nk == 1:
        epilogue(_dot(a_ref[...], w_ref[...]))
    else:
        acc_ref = scr[0]
        k = pl.program_id(2)

        @pl.when(k == 0)
        def _():
            acc_ref[...] = jnp.zeros_like(acc_ref)

        acc_ref[...] += _dot(a_ref[...], w_ref[...])

        @pl.when(k == nk - 1)
        def _():
            epilogue(acc_ref[...])


def matmul(a, w, layer, *, tm, tn, tk=None, res=None, gate=None, rows_per_gate=None, name="matmul"):
    m, kdim = a.shape
    n = w.shape[-1]
    tk = kdim if tk is None else tk
    nk = kdim // tk
    has_res = res is not None
    in_specs = [pl.BlockSpec((tm, tk), lambda i, j, k: (i, k)),
                pl.BlockSpec((None, tk, tn), lambda i, j, k: (layer, k, j))]
    args = [a, w]
    if has_res:
        tpg = rows_per_gate // tm
        in_specs += [pl.BlockSpec((tm, tn), lambda i, j, k: (i, j)),
                     pl.BlockSpec((1, 1, tn), lambda i, j, k: (i // tpg, 0, j))]
        args += [res, gate]
    return pl.pallas_call(
        functools.partial(_mm_body, nk=nk, has_res=has_res),
        grid=(m // tm, n // tn, nk),
        in_specs=in_specs,
        out_specs=pl.BlockSpec((tm, tn), lambda i, j, k: (i, j)),
        out_shape=jax.ShapeDtypeStruct((m, n), F32),
        scratch_shapes=[pltpu.VMEM((tm, tn), F32)] if nk > 1 else [],
        compiler_params=_cp("arbitrary", "arbitrary", "arbitrary"),
        name=name,
    )(*args)


def _s5_params(lp, lseg):
    lam_re, lam_im = lp["s5_lam_re"], lp["s5_lam_im"]
    step = jnp.exp(lp["s5_log_step"])[:, :, None]
    mag = jnp.exp(lam_re * step)
    ab_re = mag * jnp.cos(lam_im * step)
    ab_im = mag * jnp.sin(lam_im * step)
    den = lam_re * lam_re + lam_im * lam_im
    co_re = ((ab_re - 1.0) * lam_re + ab_im * lam_im) / den
    co_im = (ab_im * lam_re - (ab_re - 1.0) * lam_im) / den
    b_re, b_im = lp["s5_b_re"], lp["s5_b_im"]
    bb_re = co_re[..., None] * b_re - co_im[..., None] * b_im
    bb_im = co_re[..., None] * b_im + co_im[..., None] * b_re
    eye = jnp.eye(8, dtype=F32)

    def in_blocks(bb):
        t = bb.reshape(2, S5_JB, 8, S5_STATE, S5_CH)
        t = jnp.einsum("djgpc,gh->djgchp", t, eye)
        return t.reshape(2, S5_JB, 8 * S5_CH, 8 * S5_STATE)

    def out_blocks(cc):
        t = cc.reshape(2, S5_JB, 8, S5_CH, S5_STATE)
        t = jnp.einsum("djgcp,gh->djgphc", t, eye)
        return t.reshape(2, S5_JB, 8 * S5_STATE, 8 * S5_CH)

    wb = jnp.concatenate([in_blocks(bb_re), in_blocks(bb_im)], axis=-1).astype(BF16)
    wc = jnp.concatenate([out_blocks(lp["s5_c_re"]), -out_blocks(lp["s5_c_im"])], axis=-2).astype(BF16)

    def lanes(t):
        return t.reshape(2, S5_JB, 1, 8 * S5_STATE)

    a = jnp.concatenate([lanes(ab_re), lanes(ab_im)], axis=-1)
    pr, pi = ab_re, ab_im
    for _ in range(int(round(math.log2(lseg)))):
        pr, pi = pr * pr - pi * pi, 2.0 * pr * pi
    al = jnp.concatenate([lanes(pr), lanes(pi)], axis=-1)
    return wb, wc, a, al


S5_JP = 2


def _s5_chains(bsz):
    return [(b, jj) for b in range(bsz) for jj in range(S5_JP)]


def _s5_project_in(u_ref, wb_ref, a_ref, bu_scr, ab_scr, bsz):
    for ch, (b, jj) in enumerate(_s5_chains(bsz)):
        bu_scr[ch] = _dot(u_ref[b, :, jj * 128:(jj + 1) * 128].astype(BF16), wb_ref[0, jj])
    for jj in range(S5_JP):
        ab_scr[jj] = jnp.broadcast_to(a_ref[0, jj], (S5_SEG, 1024))


def _s5_scan_tile(d, t_steps, bu_scr, ab_scr, h_scr, store, bsz):
    chains = _s5_chains(bsz)

    def step(s, carry):
        row = jnp.where(d == 0, s, t_steps - 1 - s)
        off = pl.multiple_of(row * S5_SEG, S5_SEG)
        out = []
        for ch, (_, jj) in enumerate(chains):
            hr, hi = carry[2 * ch], carry[2 * ch + 1]
            ar = ab_scr[jj, :, 0:512]
            ai = ab_scr[jj, :, 512:1024]
            nr = ar * hr - ai * hi + bu_scr[ch, pl.ds(off, S5_SEG), 0:512]
            ni = ar * hi + ai * hr + bu_scr[ch, pl.ds(off, S5_SEG), 512:1024]
            if store:
                bu_scr[ch, pl.ds(off, S5_SEG), 0:512] = nr
                bu_scr[ch, pl.ds(off, S5_SEG), 512:1024] = ni
            out += [nr, ni]
        return tuple(out)

    init = tuple(h_scr[ch, :, lo:lo + 512] for ch in range(len(chains)) for lo in (0, 512))
    fin = lax.fori_loop(0, t_steps, step, init, unroll=2)
    for ch in range(len(chains)):
        h_scr[ch, :, 0:512] = fin[2 * ch]
        h_scr[ch, :, 512:1024] = fin[2 * ch + 1]


def _s5_p1_body(u_ref, wb_ref, a_ref, al_ref, h0_ref, hinit_ref, fin_ref, bu_scr, ab_scr, h_scr, *, t_steps, nt, bsz):
    d = pl.program_id(0)
    i = pl.program_id(2)

    @pl.when(i == 0)
    def _():
        h_scr[...] = jnp.zeros_like(h_scr)

    _s5_project_in(u_ref, wb_ref, a_ref, bu_scr, ab_scr, bsz)
    _s5_scan_tile(d, t_steps, bu_scr, ab_scr, h_scr, False, bsz)

    @pl.when(i == nt - 1)
    def _():
        for ch, (b, jj) in enumerate(_s5_chains(bsz)):
            alr = al_ref[0, jj, :, 0:512]
            ali = al_ref[0, jj, :, 512:1024]
            cr = h0_ref[b, 0, jj, :, 0:512]
            ci = h0_ref[b, 0, jj, :, 512:1024]
            for s in range(S5_SEG):
                k = jnp.where(d == 0, s, S5_SEG - 1 - s)
                hinit_ref[b, 0, jj, pl.ds(k, 1), 0:512] = cr
                hinit_ref[b, 0, jj, pl.ds(k, 1), 512:1024] = ci
                fr = h_scr[ch, pl.ds(k, 1), 0:512]
                fi = h_scr[ch, pl.ds(k, 1), 512:1024]
                cr, ci = alr * cr - ali * ci + fr, alr * ci + ali * cr + fi
            fin_ref[b, 0, jj, :, 0:512] = cr
            fin_ref[b, 0, jj, :, 512:1024] = ci


def _s5_p2_body(u_ref, wb_ref, wc_ref, a_ref, hinit_ref, y_ref, bu_scr, ab_scr, h_scr, *, t_steps, bsz):
    d = pl.program_id(0)
    i = pl.program_id(2)

    @pl.when(i == 0)
    def _():
        for ch, (b, jj) in enumerate(_s5_chains(bsz)):
            h_scr[ch] = hinit_ref[b, 0, jj]

    _s5_project_in(u_ref, wb_ref, a_ref, bu_scr, ab_scr, bsz)
    _s5_scan_tile(d, t_steps, bu_scr, ab_scr, h_scr, True, bsz)
    for ch, (b, jj) in enumerate(_s5_chains(bsz)):
        y_ref[0, b, :, jj * 128:(jj + 1) * 128] = _dot(bu_scr[ch].astype(BF16), wc_ref[0, jj])


def _s5_fin_body(u_ref, yf_ref, yb_ref, d_ref, w_ref, b_ref, o_ref):
    y = u_ref[0] * d_ref[...] + yf_ref[0, 0] + yb_ref[0, 0]
    yg = jax.nn.gelu(y)
    o_ref[0] = yg * jax.nn.sigmoid(_dot(yg.astype(BF16), w_ref[...]) + b_ref[...])


def s5_mixer(p3, h0, lp, wglu_bf, layer, need_output):
    bsz, seq, _ = p3.shape
    lseg = seq // S5_SEG
    t_steps = min(64, lseg)
    nt = lseg // t_steps
    rows = t_steps * S5_SEG
    wb, wc, a, al = _s5_params(lp, lseg)
    u_perm = p3[:, :, 0:GROUP_W].reshape(bsz, S5_SEG, lseg, GROUP_W).transpose(0, 2, 1, 3).reshape(bsz, seq, GROUP_W)

    def tile(d, i):
        return jnp.where(d == 0, i, nt - 1 - i)

    nchain = bsz * S5_JP
    grid = (2, S5_JB // S5_JP, nt)
    u_spec = pl.BlockSpec((bsz, rows, 128 * S5_JP), lambda d, j, i: (0, tile(d, i), j))
    wb_spec = pl.BlockSpec((1, S5_JP, 128, 1024), lambda d, j, i: (d, j, 0, 0))
    a_spec = pl.BlockSpec((1, S5_JP, 1, 1024), lambda d, j, i: (d, j, 0, 0))
    st1_spec = pl.BlockSpec((bsz, 1, S5_JP, 1, 1024), lambda d, j, i: (0, d, j, 0, 0))
    st8_spec = pl.BlockSpec((bsz, 1, S5_JP, S5_SEG, 1024), lambda d, j, i: (0, d, j, 0, 0))
    scratch = [pltpu.VMEM((nchain, rows, 1024), F32), pltpu.VMEM((S5_JP, S5_SEG, 1024), F32),
               pltpu.VMEM((nchain, S5_SEG, 1024), F32)]
    hinit, fin = pl.pallas_call(
        functools.partial(_s5_p1_body, t_steps=t_steps, nt=nt, bsz=bsz),
        grid=grid,
        in_specs=[u_spec, wb_spec, a_spec, a_spec, st1_spec],
        out_specs=[st8_spec, st1_spec],
        out_shape=[jax.ShapeDtypeStruct((bsz, 2, S5_JB, S5_SEG, 1024), F32),
                   jax.ShapeDtypeStruct((bsz, 2, S5_JB, 1, 1024), F32)],
        scratch_shapes=scratch,
        compiler_params=_cp("arbitrary", "arbitrary", "arbitrary"),
        name="s5_pass1",
    )(u_perm, wb, a, al, h0)
    if not need_output:
        return None, fin
    y = pl.pallas_call(
        functools.partial(_s5_p2_body, t_steps=t_steps, bsz=bsz),
        grid=grid,
        in_specs=[u_spec, wb_spec,
                  pl.BlockSpec((1, S5_JP, 1024, 128), lambda d, j, i: (d, j, 0, 0)),
                  a_spec, st8_spec],
        out_specs=pl.BlockSpec((1, bsz, rows, 128 * S5_JP), lambda d, j, i: (d, 0, tile(d, i), j)),
        out_shape=jax.ShapeDtypeStruct((2, bsz, seq, GROUP_W), F32),
        scratch_shapes=scratch,
        compiler_params=_cp("arbitrary", "arbitrary", "arbitrary"),
        name="s5_pass2",
    )(u_perm, wb, wc, a, hinit)
    tr = min(512, seq)
    out = pl.pallas_call(
        _s5_fin_body,
        grid=(bsz, seq // tr),
        in_specs=[pl.BlockSpec((1, tr, GROUP_W), lambda b, i: (b, i, 0)),
                  pl.BlockSpec((1, 1, tr, GROUP_W), lambda b, i: (0, b, i, 0)),
                  pl.BlockSpec((1, 1, tr, GROUP_W), lambda b, i: (1, b, i, 0)),
                  pl.BlockSpec((1, GROUP_W), lambda b, i: (0, 0)),
                  pl.BlockSpec((None, GROUP_W, GROUP_W), lambda b, i: (layer, 0, 0)),
                  pl.BlockSpec((1, GROUP_W), lambda b, i: (0, 0))],
        out_specs=pl.BlockSpec((1, tr, GROUP_W), lambda b, i: (b, i, 0)),
        out_shape=jax.ShapeDtypeStruct((bsz, seq, GROUP_W), F32),
        compiler_params=_cp("arbitrary", "arbitrary"),
        name="s5_finalize",
    )(u_perm, y, y, lp["s5_d"].reshape(1, GROUP_W), wglu_bf, lp["s5_b_glu"].reshape(1, GROUP_W))
    out = out.reshape(bsz, lseg, S5_SEG, GROUP_W).transpose(0, 2, 1, 3).reshape(bsz, seq, GROUP_W)
    return out, fin


def _lru_body(xp_ref, xm_ref, xn_ref, cw_ref, cb_ref, wg_ref, bg_ref, sp_ref, h0_ref, h_ref, fin_ref,
              a_scr, b_scr, hc_scr, *, tile_rows, nt, seq):
    d = pl.program_id(1)
    i = pl.program_id(2)
    ti = jnp.where(d == 0, i, nt - 1 - i)

    @pl.when(i == 0)
    def _():
        hc_scr[...] = h0_ref[0, 0]

    n = tile_rows + 16
    xe = jnp.concatenate([xp_ref[0], xm_ref[0], xn_ref[0]], axis=0)
    rowid = lax.broadcasted_iota(jnp.int32, (n, 1), 0) + (ti * tile_rows - 8)
    xe = jnp.where((rowid >= 0) & (rowid < seq), xe, 0.0)
    cw = cw_ref[...]
    xc = (cb_ref[...]
          + cw[0:1] * pltpu.roll(xe, 2, 0)[8:8 + tile_rows]
          + cw[1:2] * pltpu.roll(xe, 1, 0)[8:8 + tile_rows]
          + cw[2:3] * xe[8:8 + tile_rows]
          + cw[3:4] * pltpu.roll(xe, n - 1, 0)[8:8 + tile_rows])
    for cb in range(4):
        lo, hi = cb * 256, (cb + 1) * 256
        xcb = xc[:, lo:hi]
        pre = _dot(xcb.astype(BF16), wg_ref[0, cb])
        r = jax.nn.sigmoid(pre[:, 0:256] + bg_ref[0, :, lo:hi])
        ig = jax.nn.sigmoid(pre[:, 256:512] + bg_ref[0, :, GROUP_W + lo:GROUP_W + hi])
        log_a = -LRU_C * sp_ref[0, :, lo:hi] * r
        a = jnp.exp(log_a)
        a_scr[:, lo:hi] = a
        b_scr[:, lo:hi] = jnp.sqrt(jnp.tanh(-log_a) * (a * a + 1.0)) * (ig * xcb)

    def step(s, h):
        t = jnp.where(d == 0, s, tile_rows - 1 - s)
        h = a_scr[pl.ds(t, 1), :] * h + b_scr[pl.ds(t, 1), :]
        b_scr[pl.ds(t, 1), :] = h
        return h

    h = lax.fori_loop(0, tile_rows, step, hc_scr[...], unroll=8)
    hc_scr[...] = h
    h_ref[0, 0] = b_scr[...]
    fin_ref[0, 0] = h


def _lru_params(lp):
    def blockdiag(w):
        t = w.reshape(2, 4, 4, 64, 64)
        t = jnp.einsum("dcgij,gh->dcgihj", t, jnp.eye(4, dtype=F32))
        return t.reshape(2, 4, 256, 256)

    wg = jnp.concatenate([blockdiag(lp["lru_w_a"]), blockdiag(lp["lru_w_x"])], axis=-1).astype(BF16)
    bg = jnp.concatenate([lp["lru_b_a"], lp["lru_b_x"]], axis=-1).reshape(2, 1, 2 * GROUP_W)
    sp = jax.nn.softplus(-lp["lru_lam"]).reshape(2, 1, GROUP_W)
    return wg, bg, sp


def lru_mixer(p3, h0, lp):
    bsz, seq, _ = p3.shape
    tr = min(512, seq)
    nt = seq // tr
    wg, bg, sp = _lru_params(lp)
    nb8 = seq // 8

    def tile(d, i):
        return jnp.where(d == 0, i, nt - 1 - i)

    return pl.pallas_call(
        functools.partial(_lru_body, tile_rows=tr, nt=nt, seq=seq),
        grid=(bsz, 2, nt),
        in_specs=[pl.BlockSpec((1, 8, GROUP_W), lambda b, d, i: (b, jnp.maximum(tile(d, i) * (tr // 8) - 1, 0), 1)),
                  pl.BlockSpec((1, tr, GROUP_W), lambda b, d, i: (b, tile(d, i), 1)),
                  pl.BlockSpec((1, 8, GROUP_W), lambda b, d, i: (b, jnp.minimum((tile(d, i) + 1) * (tr // 8), nb8 - 1), 1)),
                  pl.BlockSpec((4, GROUP_W), lambda b, d, i: (0, 0)),
                  pl.BlockSpec((1, GROUP_W), lambda b, d, i: (0, 0)),
                  pl.BlockSpec((1, 4, 256, 512), lambda b, d, i: (d, 0, 0, 0)),
                  pl.BlockSpec((1, 1, 2 * GROUP_W), lambda b, d, i: (d, 0, 0)),
                  pl.BlockSpec((1, 1, GROUP_W), lambda b, d, i: (d, 0, 0)),
                  pl.BlockSpec((1, 1, 1, GROUP_W), lambda b, d, i: (b, d, 0, 0))],
        out_specs=[pl.BlockSpec((1, 1, tr, GROUP_W), lambda b, d, i: (d, b, tile(d, i), 0)),
                   pl.BlockSpec((1, 1, 1, GROUP_W), lambda b, d, i: (b, d, 0, 0))],
        out_shape=[jax.ShapeDtypeStruct((2, bsz, seq, GROUP_W), F32),
                   jax.ShapeDtypeStruct((bsz, 2, 1, GROUP_W), F32)],
        scratch_shapes=[pltpu.VMEM((tr, GROUP_W), F32), pltpu.VMEM((tr, GROUP_W), F32), pltpu.VMEM((1, GROUP_W), F32)],
        compiler_params=_cp("arbitrary", "arbitrary", "arbitrary"),
        name="lru_scan",
    )(p3, p3, p3, lp["lru_conv_w"], lp["lru_conv_b"].reshape(1, GROUP_W), wg, bg, sp, h0)


def _log_sigmoid(z):
    return jnp.minimum(z, 0.0) - jnp.log1p(jnp.exp(-jnp.abs(z)))


def _gla_body(q_ref, k_ref, v_ref, lr_ref, wa_ref, ba_ref, tri_ref, s0_ref, o_ref, sfin_ref, s_scr, *, nch):
    d = pl.program_id(1)
    i = pl.program_id(2)
    c = GLA_CHUNK

    @pl.when(i == 0)
    def _():
        s_scr[...] = s0_ref[0, 0]

    tri = tri_ref[0]

    def chunk(s, carry):
        ci = jnp.where(d == 0, s, nch - 1 - s)
        r0 = pl.multiple_of(ci * c, c)
        z = jnp.dot(lr_ref[0, pl.ds(r0, c), :], wa_ref[0], preferred_element_type=F32, precision=HI) + ba_ref[0]
        la = _log_sigmoid(z) / GLA_GATE_NORM
        b_all = jnp.dot(tri, la, preferred_element_type=F32, precision=HI)
        q_all = q_ref[0, pl.ds(r0, c), :] * (GLA_DK ** -0.5)
        k_all = k_ref[0, pl.ds(r0, c), :]
        v_all = v_ref[0, pl.ds(r0, c), :].astype(BF16)
        for h in range(GLA_HEADS):
            ks = slice(h * GLA_DK, (h + 1) * GLA_DK)
            vs = slice(h * GLA_DV, (h + 1) * GLA_DV)
            q, k, v, b = q_all[:, ks], k_all[:, ks], v_all[:, vs], b_all[:, ks]
            b_mid = jnp.where(d == 0, b[c // 2 - 1:c // 2], b[c // 2:c // 2 + 1])
            qd = (q * jnp.exp(b - b_mid)).astype(BF16)
            kd = (k * jnp.exp(b_mid - b)).astype(BF16)
            sc = lax.dot_general(qd, kd, (((1,), (1,)), ((), ())), preferred_element_type=F32) * tri
            intra = _dot(sc.astype(BF16), v)
            st = s_scr[h]
            inter = _dot((q * jnp.exp(b)).astype(BF16), st.astype(BF16))
            o_ref[0, 0, pl.ds(r0, c), vs] = intra + inter
            kt = k.T
            bt = b.T
            bt_last = jnp.where(d == 0, bt[:, c - 1:c], bt[:, 0:1])
            k2t = (kt * jnp.exp(bt_last - bt)).astype(BF16)
            s_scr[h] = jnp.exp(bt_last) * st + _dot(k2t, v)
        return carry

    lax.fori_loop(0, nch, chunk, 0)
    sfin_ref[0, 0] = s_scr[...]


def gla_mixer(p3, s0, lp):
    bsz, seq, _ = p3.shape
    tr = min(512, seq)
    nt = seq // tr
    qkw = GLA_HEADS * GLA_DK
    wa_pad = jnp.zeros((2, 128, qkw), F32)
    wa_pad = wa_pad.at[0, 0:16].set(lp["gla_w_alpha"][0]).at[1, 16:32].set(lp["gla_w_alpha"][1])
    ba = lp["gla_b_alpha"].reshape(2, 1, qkw)
    lower = np.tril(np.ones((GLA_CHUNK, GLA_CHUNK), np.float32))
    tri = jnp.asarray(np.stack([lower, lower.T]))

    def tile(d, i):
        return jnp.where(d == 0, i, nt - 1 - i)

    return pl.pallas_call(
        functools.partial(_gla_body, nch=tr // GLA_CHUNK),
        grid=(bsz, 2, nt),
        in_specs=[pl.BlockSpec((1, tr, qkw), lambda b, d, i: (b, tile(d, i), 3072 // qkw)),
                  pl.BlockSpec((1, tr, qkw), lambda b, d, i: (b, tile(d, i), 3584 // qkw)),
                  pl.BlockSpec((1, tr, GROUP_W), lambda b, d, i: (b, tile(d, i), 4)),
                  pl.BlockSpec((1, tr, 128), lambda b, d, i: (b, tile(d, i), 72)),
                  pl.BlockSpec((1, 128, qkw), lambda b, d, i: (d, 0, 0)),
                  pl.BlockSpec((1, 1, qkw), lambda b, d, i: (d, 0, 0)),
                  pl.BlockSpec((1, GLA_CHUNK, GLA_CHUNK), lambda b, d, i: (d, 0, 0)),
                  pl.BlockSpec((1, 1, GLA_HEADS, GLA_DK, GLA_DV), lambda b, d, i: (b, d, 0, 0, 0))],
        out_specs=[pl.BlockSpec((1, 1, tr, GROUP_W), lambda b, d, i: (d, b, tile(d, i), 0)),
                   pl.BlockSpec((1, 1, GLA_HEADS, GLA_DK, GLA_DV), lambda b, d, i: (b, d, 0, 0, 0))],
        out_shape=[jax.ShapeDtypeStruct((2, bsz, seq, GROUP_W), F32),
                   jax.ShapeDtypeStruct((bsz, 2, GLA_HEADS, GLA_DK, GLA_DV), F32)],
        scratch_shapes=[pltpu.VMEM((GLA_HEADS, GLA_DK, GLA_DV), F32)],
        compiler_params=_cp("arbitrary", "arbitrary", "arbitrary"),
        name="gla_scan",
    )(p3, p3, p3, p3, wa_pad, ba, tri, s0)


def _hy_conv3_body(x_ref, w_ref, b_ref, o_ref, *, seq):
    x = x_ref[0]
    w = w_ref[...]
    row = lax.broadcasted_iota(jnp.int32, (seq, 1), 0)
    xm = jnp.where(row == 0, 0.0, pltpu.roll(x, 1, 0))
    xp = jnp.where(row == seq - 1, 0.0, pltpu.roll(x, seq - 1, 0))
    o_ref[0] = w[0:1] * xm + w[1:2] * x + w[2:3] * xp + b_ref[...]


def hy_conv3(p3, lp):
    bsz, seq, _ = p3.shape
    w3 = 3 * GROUP_W
    return pl.pallas_call(
        functools.partial(_hy_conv3_body, seq=seq),
        grid=(bsz, w3 // 128),
        in_specs=[pl.BlockSpec((1, seq, 128), lambda b, c: (b, 0, 48 + c)),
                  pl.BlockSpec((3, 128), lambda b, c: (0, c)),
                  pl.BlockSpec((1, 128), lambda b, c: (0, c))],
        out_specs=pl.BlockSpec((1, seq, 128), lambda b, c: (b, 0, c)),
        out_shape=jax.ShapeDtypeStruct((bsz, seq, w3), F32),
        compiler_params=_cp("arbitrary", "arbitrary"),
        name="hy_conv3",
    )(p3, lp["hy_conv_w"], lp["hy_conv_b"].reshape(1, w3))


def _hy_filter_body(bv_ref, w1_ref, b1_ref, f0_ref, w2_ref, b2_ref, f1_ref, w3_ref, dl_ref, o_ref, *, tr, seq):
    i = pl.program_id(0)
    n = 2 * seq
    t = lax.broadcasted_iota(jnp.int32, (tr, 1), 0) + i * tr
    pos = jnp.where(t < seq, t, n - t).astype(F32)
    tt = pos / seq
    w = (2.0 * math.pi) * pos / seq
    lane = lax.broadcasted_iota(jnp.int32, (tr, 128), 1)
    arg = w * bv_ref[...]
    feats = jnp.where(lane == 0, tt,
                      jnp.where(lane <= HY_BANDS, jnp.cos(arg),
                                jnp.where(lane <= 2 * HY_BANDS, -jnp.sin(arg), 0.0)))
    h = jnp.sin(f0_ref[...] * (jnp.dot(feats, w1_ref[...], preferred_element_type=F32, precision=HI) + b1_ref[...]))
    h = jnp.sin(f1_ref[...] * (jnp.dot(h, w2_ref[...], preferred_element_type=F32, precision=HI) + b2_ref[...]))
    out = jnp.dot(h, w3_ref[0], preferred_element_type=F32, precision=HI)
    out = out * jnp.exp(-tt * dl_ref[...])
    o_ref[...] = jnp.where(t == seq, 0.0, out)


def hy_filter(lp, seq):
    n = 2 * seq
    tr = min(512, seq)
    nt = n // tr
    bands = np.linspace(1e-4, HY_BANDS - 1, HY_BANDS, dtype=np.float32)
    bv = np.zeros((1, 128), np.float32)
    bv[0, 1:1 + HY_BANDS] = bands
    bv[0, 1 + HY_BANDS:1 + 2 * HY_BANDS] = bands
    deltas = np.abs(np.linspace(math.log(1e-2) / 0.3, math.log(1e-2) / 1.5, GROUP_W, dtype=np.float32))
    dl = np.concatenate([deltas, deltas])[None, :]

    def pad2(w, r, c):
        return jnp.zeros((r, c), F32).at[:w.shape[0], :w.shape[1]].set(w)

    w1 = pad2(lp["hy_w1"], 128, 128)
    b1 = pad2(lp["hy_b1"][None, :], 1, 128)
    f0 = pad2(lp["hy_freq"][0][None, :], 1, 128)
    w2 = pad2(lp["hy_w2"], 128, 128)
    b2 = pad2(lp["hy_b2"][None, :], 1, 128)
    f1 = pad2(lp["hy_freq"][1][None, :], 1, 128)
    w3 = lp["hy_w3"].reshape(HY_FFN, 2, 2 * GROUP_W).transpose(1, 0, 2)
    w3 = jnp.zeros((2, 128, 2 * GROUP_W), F32).at[:, :HY_FFN].set(w3)
    half = nt // 2
    vec = lambda i: (0, 0)
    return pl.pallas_call(
        functools.partial(_hy_filter_body, tr=tr, seq=seq),
        grid=(nt,),
        in_specs=[pl.BlockSpec((1, 128), vec), pl.BlockSpec((128, 128), vec), pl.BlockSpec((1, 128), vec),
                  pl.BlockSpec((1, 128), vec), pl.BlockSpec((128, 128), vec), pl.BlockSpec((1, 128), vec),
                  pl.BlockSpec((1, 128), vec),
                  pl.BlockSpec((1, 128, 2 * GROUP_W), lambda i: (jnp.where(i < half, 0, 1), 0, 0)),
                  pl.BlockSpec((1, 2 * GROUP_W), vec)],
        out_specs=pl.BlockSpec((tr, 2 * GROUP_W), lambda i: (i, 0)),
        out_shape=jax.ShapeDtypeStruct((n, 2 * GROUP_W), F32),
        compiler_params=_cp("arbitrary"),
        name="hy_filter",
    )(jnp.asarray(bv), w1, b1, f0, w2, b2, f1, w3, jnp.asarray(dl))


HY_N1 = 128
HY_N2 = 128


@functools.lru_cache(maxsize=None)
def _dft_tables(t1_used):
    n = HY_N1 * HY_N2
    k1 = np.arange(HY_N1)[None, :, None]
    t1 = np.arange(t1_used)[None, None, :]
    t2 = np.arange(HY_N2)[:, None, None]
    ph = 2.0 * np.pi * ((k1 * (HY_N2 * t1 + t2)) % n) / n
    fa = np.concatenate([np.cos(ph), -np.sin(ph)], axis=1)
    ga = np.concatenate([np.cos(ph).transpose(0, 2, 1), -np.sin(ph).transpose(0, 2, 1)], axis=2) / n
    kk = np.arange(HY_N2)
    ph2 = 2.0 * np.pi * ((kk[:, None] * kk[None, :]) % HY_N2) / HY_N2
    cm, sm = np.cos(ph2), np.sin(ph2)
    fb = np.block([[cm, sm], [-sm, cm]])
    fbi = np.block([[cm, -sm], [sm, cm]])
    return fa, ga, fb, fbi


def _hy_stage_a_body(z_ref, fh_ref, fl_ref, o_ref):
    for j in range(8):
        x = z_ref[0, :, j, :]
        res = _dot3(fh_ref[j], fl_ref[j], x)
        o_ref[0, 0, :, j, :] = res[0:HY_N1]
        o_ref[0, 1, :, j, :] = res[HY_N1:2 * HY_N1]


def hy_stage_a(z4, t1_used, ch0, nch, cb=512):
    bz = z4.shape[0]
    fa, _, _, _ = _dft_tables(t1_used)
    fh, fl = _np_split(fa)
    cb0 = ch0 // cb
    return pl.pallas_call(
        _hy_stage_a_body,
        grid=(bz, HY_N2 // 8, nch // cb),
        in_specs=[pl.BlockSpec((1, t1_used, 8, cb), lambda b, g, c: (b, 0, g, cb0 + c)),
                  pl.BlockSpec((8, 2 * HY_N1, t1_used), lambda b, g, c: (g, 0, 0)),
                  pl.BlockSpec((8, 2 * HY_N1, t1_used), lambda b, g, c: (g, 0, 0))],
        out_specs=pl.BlockSpec((1, 2, HY_N1, 8, cb), lambda b, g, c: (b, 0, 0, g, c)),
        out_shape=jax.ShapeDtypeStruct((bz, 2, HY_N1, HY_N2, nch), F32),
        compiler_params=_cp("arbitrary", "arbitrary", "arbitrary"),
        name="hy_stage_a",
    )(z4, fh, fl)


def _hy_stage_b_body(a_ref, fh_ref, fl_ref, o_ref, *, cb):
    x = a_ref[0, :, 0].reshape(2 * HY_N2, cb)
    o_ref[0, :, 0] = _dot3(fh_ref[...], fl_ref[...], x).reshape(2, HY_N2, cb)


def hy_stage_b(a5, cb=512):
    bz, _, _, _, nch = a5.shape
    _, _, fb, _ = _dft_tables(HY_N1)
    fh, fl = _np_split(fb)
    mat = pl.BlockSpec((2 * HY_N2, 2 * HY_N2), lambda b, k, c: (0, 0))
    blk = pl.BlockSpec((1, 2, 1, HY_N2, cb), lambda b, k, c: (b, 0, k, 0, c))
    return pl.pallas_call(
        functools.partial(_hy_stage_b_body, cb=cb),
        grid=(bz, HY_N1, nch // cb),
        in_specs=[blk, mat, mat],
        out_specs=blk,
        out_shape=jax.ShapeDtypeStruct(a5.shape, F32),
        compiler_params=_cp("arbitrary", "arbitrary", "arbitrary"),
        name="hy_stage_b",
    )(a5, fh, fl)


def _hy_stage_bb_body(a_ref, h_ref, fh_ref, fl_ref, gh_ref, gl_ref, o_ref, *, cb):
    x = a_ref[0, :, 0].reshape(2 * HY_N2, cb)
    z = _dot3(fh_ref[...], fl_ref[...], x)
    zr, zi = z[0:HY_N2], z[HY_N2:]
    hr, hi = h_ref[0, 0, 0], h_ref[0, 1, 0]
    y = jnp.concatenate([zr * hr - zi * hi, zr * hi + zi * hr], axis=0)
    o_ref[0, :, 0] = _dot3(gh_ref[...], gl_ref[...], y).reshape(2, HY_N2, cb)


def hy_stage_bb(a5, hspec, order, cb=512):
    bz, _, _, _, nch = a5.shape
    _, _, fb, fbi = _dft_tables(HY_N1)
    fh, fl = _np_split(fb)
    gh, gl = _np_split(fbi)
    hb0 = order * (GROUP_W // cb)
    mat = pl.BlockSpec((2 * HY_N2, 2 * HY_N2), lambda c, k, b: (0, 0))
    blk = pl.BlockSpec((1, 2, 1, HY_N2, cb), lambda c, k, b: (b, 0, k, 0, c))
    return pl.pallas_call(
        functools.partial(_hy_stage_bb_body, cb=cb),
        grid=(nch // cb, HY_N1, bz),
        in_specs=[blk,
                  pl.BlockSpec((1, 2, 1, HY_N2, cb), lambda c, k, b: (0, 0, k, 0, hb0 + c)),
                  mat, mat, mat, mat],
        out_specs=blk,
        out_shape=jax.ShapeDtypeStruct(a5.shape, F32),
        compiler_params=_cp("arbitrary", "arbitrary", "arbitrary"),
        name="hy_stage_bb",
    )(a5, hspec, fh, fl, gh, gl)


def _hy_stage_ai_body(b_ref, gh_ref, gl_ref, u_ref, g_ref, bias_ref, o_ref):
    for j in range(8):
        x = jnp.concatenate([b_ref[0, 0, :, j, :], b_ref[0, 1, :, j, :]], axis=0)
        y = _dot3(gh_ref[j], gl_ref[j], x)
        o_ref[0, :, j, :] = g_ref[0, :, j, :] * (y + bias_ref[...] * u_ref[0, :, j, :])


def hy_stage_ai(b5, u4, uc0, g4, gc0, bias, t1_used, cb=512):
    bz, _, _, _, nch = b5.shape
    _, ga, _, _ = _dft_tables(t1_used)
    gh, gl = _np_split(ga)
    ub0, gb0 = uc0 // cb, gc0 // cb
    return pl.pallas_call(
        _hy_stage_ai_body,
        grid=(bz, HY_N2 // 8, nch // cb),
        in_specs=[pl.BlockSpec((1, 2, HY_N1, 8, cb), lambda b, g, c: (b, 0, 0, g, c)),
                  pl.BlockSpec((8, t1_used, 2 * HY_N1), lambda b, g, c: (g, 0, 0)),
                  pl.BlockSpec((8, t1_used, 2 * HY_N1), lambda b, g, c: (g, 0, 0)),
                  pl.BlockSpec((1, t1_used, 8, cb), lambda b, g, c: (b, 0, g, ub0 + c)),
                  pl.BlockSpec((1, t1_used, 8, cb), lambda b, g, c: (b, 0, g, gb0 + c)),
                  pl.BlockSpec((1, cb), lambda b, g, c: (0, c))],
        out_specs=pl.BlockSpec((1, t1_used, 8, cb), lambda b, g, c: (b, 0, g, c)),
        out_shape=jax.ShapeDtypeStruct((bz, t1_used, HY_N2, nch), F32),
        compiler_params=_cp("arbitrary", "arbitrary", "arbitrary"),
        name="hy_stage_ai",
    )(b5, gh, gl, u4, g4, bias)


@functools.lru_cache(maxsize=None)
def _dft_tables_t():
    n = HY_N1 * HY_N2
    a = np.arange(128)
    ph = 2.0 * np.pi * ((a[:, None] * a[None, :]) % 128) / 128
    cm, sm = np.cos(ph), np.sin(ph)
    f1 = np.concatenate([cm, -sm], axis=1)
    pht = 2.0 * np.pi * (a[:, None] * a[None, :]) / n
    twr, twi = np.cos(pht), -np.sin(pht)
    f2 = np.block([[cm, -sm], [sm, cm]])
    g2 = np.block([[cm, sm], [-sm, cm]])
    g1 = np.concatenate([cm, -sm], axis=0) / n
    return f1, twr, twi, f2, g2, g1


def _hy_tables_args():
    f1, twr, twi, f2, g2, g1 = _dft_tables_t()
    out = []
    for m in (f1, f2, g2, g1):
        out += list(_np_split(m))
    out += [jnp.asarray(twr, F32), jnp.asarray(twi, F32)]
    return out


def _full_spec(shape):
    nd = len(shape)
    return pl.BlockSpec(shape, lambda *_: (0,) * nd)


def _hy_fwd_t(z3, f1h, f1l, f2h, f2l, twr, twi):
    cb = z3.shape[0]
    x = jnp.swapaxes(z3, 1, 2).reshape(cb * 128, 128)
    a3 = _dot3r(x, f1h, f1l).reshape(cb, 128, 256)
    ar, ai = a3[:, :, 0:128], a3[:, :, 128:256]
    br = ar * twr - ai * twi
    bi = ar * twi + ai * twr
    x2 = jnp.concatenate([jnp.swapaxes(br, 1, 2), jnp.swapaxes(bi, 1, 2)], axis=2)
    return _dot3r(x2.reshape(cb * 128, 256), f2h, f2l)


def _hy_inv_t(y, cb, g2h, g2l, g1h, g1l, twr, twi):
    b3 = _dot3r(y, g2h, g2l).reshape(cb, 128, 256)
    br = jnp.swapaxes(b3[:, :, 0:128], 1, 2)
    bi = jnp.swapaxes(b3[:, :, 128:256], 1, 2)
    cr = br * twr + bi * twi
    ci = bi * twr - br * twi
    x4 = jnp.concatenate([cr, ci], axis=2).reshape(cb * 128, 256)
    yv = _dot3r(x4, g1h, g1l).reshape(cb, 128, 128)
    return jnp.swapaxes(yv, 1, 2)


def _dot3r(x, fh, fl):
    xh, xl = _split(x)
    return _dot(xh, fh) + _dot(xl, fh) + _dot(xh, fl)


HY_SUB = 8


def _hy_spec_t_body(c_ref, f1h, f1l, f2h, f2l, g2h, g2l, g1h, g1l, twr, twi, o_ref):
    for s in range(c_ref.shape[0] // HY_SUB):
        cs = slice(s * HY_SUB, (s + 1) * HY_SUB)
        z = _hy_fwd_t(c_ref[cs], f1h[...], f1l[...], f2h[...], f2l[...], twr[...], twi[...])
        o_ref[cs] = z.reshape(HY_SUB, 128, 256)


def _hy_long_t_body(u_ref, g_ref, bias_ref, h_ref, f1h, f1l, f2h, f2l, g2h, g2l, g1h, g1l, twr, twi, o_ref):
    t1u = u_ref.shape[2]
    cb = HY_SUB
    for s in range(u_ref.shape[1] // cb):
        cs = slice(s * cb, (s + 1) * cb)
        u = u_ref[0, cs]
        z3 = jnp.concatenate([u, jnp.zeros((cb, HY_N1 - t1u, HY_N2), F32)], axis=1)
        z = _hy_fwd_t(z3, f1h[...], f1l[...], f2h[...], f2l[...], twr[...], twi[...])
        hs = h_ref[cs].reshape(cb * 128, 256)
        zr, zi, hr, hi = z[:, 0:128], z[:, 128:256], hs[:, 0:128], hs[:, 128:256]
        y = jnp.concatenate([zr * hr - zi * hi, zr * hi + zi * hr], axis=1)
        conv = _hy_inv_t(y, cb, g2h[...], g2l[...], g1h[...], g1l[...], twr[...], twi[...])[:, 0:t1u, :]
        o_ref[0, cs] = g_ref[0, cs] * (conv + bias_ref[cs] * u)


def hy_conv3_t(p3, lp):
    bsz, seq, _ = p3.shape
    w3 = 3 * GROUP_W

    def body(x_ref, w_ref, b_ref, o_ref):
        x = x_ref[0]
        w = w_ref[...]
        row = lax.broadcasted_iota(jnp.int32, (seq, 1), 0)
        xm = jnp.where(row == 0, 0.0, pltpu.roll(x, 1, 0))
        xp = jnp.where(row == seq - 1, 0.0, pltpu.roll(x, seq - 1, 0))
        o_ref[0] = (w[0:1] * xm + w[1:2] * x + w[2:3] * xp + b_ref[...]).T

    return pl.pallas_call(
        body,
        grid=(bsz, w3 // 128),
        in_specs=[pl.BlockSpec((1, seq, 128), lambda b, c: (b, 0, 48 + c)),
                  pl.BlockSpec((3, 128), lambda b, c: (0, c)),
                  pl.BlockSpec((1, 128), lambda b, c: (0, c))],
        out_specs=pl.BlockSpec((1, 128, seq), lambda b, c: (b, c, 0)),
        out_shape=jax.ShapeDtypeStruct((bsz, w3, seq), F32),
        compiler_params=_cp("arbitrary", "arbitrary"),
        name="hy_conv3_t",
    )(p3, lp["hy_conv_w"], lp["hy_conv_b"].reshape(1, w3))


def _hy_filter_t_body(bc_ref, w1_ref, b1_ref, f0_ref, w2_ref, b2_ref, f1_ref, w3_ref, dl_ref, o_ref, *, tr, seq):
    i = pl.program_id(0)
    n = 2 * seq
    t = lax.broadcasted_iota(jnp.int32, (1, tr), 1) + i * tr
    pos = jnp.where(t < seq, t, n - t).astype(F32)
    tt = pos / seq
    w = (2.0 * math.pi) * pos / seq
    row = lax.broadcasted_iota(jnp.int32, (128, tr), 0)
    arg = bc_ref[...] * w
    feats = jnp.where(row == 0, tt,
                      jnp.where(row <= HY_BANDS, jnp.cos(arg),
                                jnp.where(row <= 2 * HY_BANDS, -jnp.sin(arg), 0.0)))
    h = jnp.sin(f0_ref[...] * (jnp.dot(w1_ref[...], feats, preferred_element_type=F32, precision=HI) + b1_ref[...]))
    h = jnp.sin(f1_ref[...] * (jnp.dot(w2_ref[...], h, preferred_element_type=F32, precision=HI) + b2_ref[...]))
    out = jnp.dot(w3_ref[0], h, preferred_element_type=F32, precision=HI)
    out = out * jnp.exp(-dl_ref[...] * tt)
    o_ref[...] = jnp.where(t == seq, 0.0, out)


def hy_filter_t(lp, seq):
    n = 2 * seq
    tr = 512
    nt = n // tr
    bands = np.linspace(1e-4, HY_BANDS - 1, HY_BANDS, dtype=np.float32)
    bc = np.zeros((128, 1), np.float32)
    bc[1:1 + HY_BANDS, 0] = bands
    bc[1 + HY_BANDS:1 + 2 * HY_BANDS, 0] = bands
    deltas = np.abs(np.linspace(math.log(1e-2) / 0.3, math.log(1e-2) / 1.5, GROUP_W, dtype=np.float32))
    dl = np.concatenate([deltas, deltas])[:, None]

    def pad2(w, r, c):
        return jnp.zeros((r, c), F32).at[:w.shape[0], :w.shape[1]].set(w)

    w1 = pad2(lp["hy_w1"].T, 128, 128)
    b1 = pad2(lp["hy_b1"][:, None], 128, 1)
    f0 = pad2(lp["hy_freq"][0][:, None], 128, 1)
    w2 = pad2(lp["hy_w2"].T, 128, 128)
    b2 = pad2(lp["hy_b2"][:, None], 128, 1)
    f1 = pad2(lp["hy_freq"][1][:, None], 128, 1)
    w3 = lp["hy_w3"].reshape(HY_FFN, 2, 2 * GROUP_W).transpose(1, 2, 0)
    w3 = jnp.zeros((2, 2 * GROUP_W, 128), F32).at[:, :, :HY_FFN].set(w3)
    half = nt // 2
    return pl.pallas_call(
        functools.partial(_hy_filter_t_body, tr=tr, seq=seq),
        grid=(nt,),
        in_specs=[_full_spec((128, 1)), _full_spec((128, 128)), _full_spec((128, 1)), _full_spec((128, 1)),
                  _full_spec((128, 128)), _full_spec((128, 1)), _full_spec((128, 1)),
                  pl.BlockSpec((1, 2 * GROUP_W, 128), lambda i: (jnp.where(i < half, 0, 1), 0, 0)),
                  _full_spec((2 * GROUP_W, 1))],
        out_specs=pl.BlockSpec((2 * GROUP_W, tr), lambda i: (0, i)),
        out_shape=jax.ShapeDtypeStruct((2 * GROUP_W, n), F32),
        compiler_params=_cp("arbitrary"),
        name="hy_filter_t",
    )(jnp.asarray(bc), w1, b1, f0, w2, b2, f1, w3, jnp.asarray(dl))


HY_CB = 16


def hyena_long(p3, lp):
    bsz, seq, _ = p3.shape
    t1u = seq // HY_N2
    tabs = _hy_tables_args()
    tab_specs = [_full_spec(t.shape) for t in tabs]
    hzt = hy_conv3_t(p3, lp).reshape(bsz, 3 * GROUP_W, t1u, HY_N2)
    circ = hy_filter_t(lp, seq).reshape(2 * GROUP_W, HY_N1, HY_N2)
    cb = HY_CB
    hspec = pl.pallas_call(
        _hy_spec_t_body,
        grid=(2 * GROUP_W // cb,),
        in_specs=[pl.BlockSpec((cb, HY_N1, HY_N2), lambda c: (c, 0, 0))] + tab_specs,
        out_specs=pl.BlockSpec((cb, HY_N1, 2 * HY_N2), lambda c: (c, 0, 0)),
        out_shape=jax.ShapeDtypeStruct((2 * GROUP_W, HY_N1, 2 * HY_N2), F32),
        compiler_params=_cp("arbitrary"),
        name="hy_spec_t",
    )(circ, *tabs)
    ncb = GROUP_W // cb
    u, ub0 = hzt, 0
    for o in range(2):
        gb0 = (1 + o) * ncb
        u = pl.pallas_call(
            _hy_long_t_body,
            grid=(ncb, bsz),
            in_specs=[pl.BlockSpec((1, cb, t1u, HY_N2), lambda c, b, ub0=ub0: (b, ub0 + c, 0, 0)),
                      pl.BlockSpec((1, cb, t1u, HY_N2), lambda c, b, gb0=gb0: (b, gb0 + c, 0, 0)),
                      pl.BlockSpec((cb, 1, 1), lambda c, b: (c, 0, 0)),
                      pl.BlockSpec((cb, HY_N1, 2 * HY_N2), lambda c, b, o=o: (o * ncb + c, 0, 0))] + tab_specs,
            out_specs=pl.BlockSpec((1, cb, t1u, HY_N2), lambda c, b: (b, c, 0, 0)),
            out_shape=jax.ShapeDtypeStruct((bsz, GROUP_W, t1u, HY_N2), F32),
            compiler_params=_cp("arbitrary", "arbitrary"),
            name="hy_long_t",
        )(u, hzt, lp["hy_bias"][o].reshape(GROUP_W, 1, 1), hspec, *tabs)
        ub0 = 0
    return u.reshape(bsz, GROUP_W, seq)


@functools.lru_cache(maxsize=None)
def _dense_dft_tables(seq):
    n = 2 * seq
    k = np.arange(n)[:, None]
    t = np.arange(n)[None, :]
    ph = 2.0 * np.pi * ((k * t) % n) / n
    fwd = np.concatenate([np.cos(ph), -np.sin(ph)], axis=0)
    inv = np.concatenate([np.cos(ph).T, -np.sin(ph).T], axis=1)[:seq] / n
    return fwd, inv


def _hy_dense_spec_body(c_ref, fh_ref, fl_ref, o_ref):
    o_ref[...] = _dot3(fh_ref[...], fl_ref[...], c_ref[...])


def _hy_dense_body(z_ref, h_ref, fh_ref, fl_ref, gh_ref, gl_ref, g_ref, bias_ref, o_ref, *, n):
    u = z_ref[0]
    z = _dot3(fh_ref[...], fl_ref[...], u)
    zr, zi = z[0:n], z[n:]
    hr, hi = h_ref[0:n], h_ref[n:2 * n]
    y = jnp.concatenate([zr * hr - zi * hi, zr * hi + zi * hr], axis=0)
    conv = _dot3(gh_ref[...], gl_ref[...], y)
    o_ref[0] = g_ref[0] * (conv + bias_ref[...] * u)


def hy_dense(hzc, circ, lp, cb=512):
    bsz, seq, _ = hzc.shape
    n = 2 * seq
    fwd, inv = _dense_dft_tables(seq)
    fh, fl = _np_split(fwd)
    gh, gl = _np_split(inv)
    hspec = pl.pallas_call(
        _hy_dense_spec_body,
        grid=(2 * GROUP_W // cb,),
        in_specs=[pl.BlockSpec((n, cb), lambda c: (0, c)),
                  pl.BlockSpec((2 * n, n), lambda c: (0, 0)),
                  pl.BlockSpec((2 * n, n), lambda c: (0, 0))],
        out_specs=pl.BlockSpec((2 * n, cb), lambda c: (0, c)),
        out_shape=jax.ShapeDtypeStruct((2 * n, 2 * GROUP_W), F32),
        compiler_params=_cp("arbitrary"),
        name="hy_dense_spec",
    )(circ, fh, fl)
    fh_in, fl_in = fh[:, :seq], fl[:, :seq]
    u, uc0 = hzc, 0
    ncb = GROUP_W // cb
    for o in range(2):
        gc0 = (1 + o) * ncb
        u = pl.pallas_call(
            functools.partial(_hy_dense_body, n=n),
            grid=(bsz, ncb),
            in_specs=[pl.BlockSpec((1, seq, cb), lambda b, c, uc0=uc0: (b, 0, uc0 + c)),
                      pl.BlockSpec((2 * n, cb), lambda b, c, o=o: (0, o * ncb + c)),
                      pl.BlockSpec((2 * n, seq), lambda b, c: (0, 0)),
                      pl.BlockSpec((2 * n, seq), lambda b, c: (0, 0)),
                      pl.BlockSpec((seq, 2 * n), lambda b, c: (0, 0)),
                      pl.BlockSpec((seq, 2 * n), lambda b, c: (0, 0)),
                      pl.BlockSpec((1, seq, cb), lambda b, c, gc0=gc0: (b, 0, gc0 + c)),
                      pl.BlockSpec((1, cb), lambda b, c: (0, c))],
            out_specs=pl.BlockSpec((1, seq, cb), lambda b, c: (b, 0, c)),
            out_shape=jax.ShapeDtypeStruct((bsz, seq, GROUP_W), F32),
            compiler_params=_cp("arbitrary", "arbitrary"),
            name="hy_dense_conv",
        )(u, hspec, fh_in, fl_in, gh, gl, hzc, lp["hy_bias"][o].reshape(1, GROUP_W))
        uc0 = 0
    return u


def hyena_mixer(p3, lp):
    bsz, seq, _ = p3.shape
    if 2 * seq == HY_N1 * HY_N2:
        return hyena_long(p3, lp), True
    hzc = hy_conv3(p3, lp)
    circ = hy_filter(lp, seq)
    return hy_dense(hzc, circ, lp), False


def _rms(y, g):
    return y * lax.rsqrt(jnp.mean(y * y, axis=-1, keepdims=True) + EPS) * g


def _mix_body(s5_ref, hf_ref, hb_ref, lg_ref, of_ref, ob_ref, gg_ref, hy_ref, gn_ref, mg_ref, o_ref, *, hy_t):
    w = GROUP_W
    y_hy = hy_ref[0].T if hy_t else hy_ref[...]
    o_ref[:, 0:w] = _rms(s5_ref[...], mg_ref[:, 0:w]).astype(o_ref.dtype)
    y_lru = (hf_ref[0] + hb_ref[0]) * jax.nn.gelu(lg_ref[...])
    o_ref[:, w:2 * w] = _rms(y_lru, mg_ref[:, w:2 * w]).astype(o_ref.dtype)
    o = of_ref[0] + ob_ref[0]
    gg = gg_ref[...]
    heads = []
    for h in range(GLA_HEADS):
        sl = slice(h * GLA_DV, (h + 1) * GLA_DV)
        heads.append(_rms(o[:, sl], gn_ref[...]) * (gg[:, sl] * jax.nn.sigmoid(gg[:, sl])))
    y_gla = jnp.concatenate(heads, axis=-1)
    o_ref[:, 2 * w:3 * w] = _rms(y_gla, mg_ref[:, 2 * w:3 * w]).astype(o_ref.dtype)
    o_ref[:, 3 * w:4 * w] = _rms(y_hy, mg_ref[:, 3 * w:4 * w]).astype(o_ref.dtype)


def mix_assemble(p2, y_s5, h_lru, o_gla, y_hy, hy_t, seq, lp, tm=256):
    m = p2.shape[0]
    w = GROUP_W
    row = lambda i: (i, 0)
    tpb = seq // tm
    if hy_t:
        hy_spec = pl.BlockSpec((1, w, tm), lambda i: (i // tpb, 0, i % tpb))
    else:
        hy_spec = pl.BlockSpec((tm, w), row)
        y_hy = y_hy.reshape(m, w)
    return pl.pallas_call(
        functools.partial(_mix_body, hy_t=hy_t),
        grid=(m // tm,),
        in_specs=[pl.BlockSpec((tm, w), row),
                  pl.BlockSpec((1, tm, w), lambda i: (0, i, 0)),
                  pl.BlockSpec((1, tm, w), lambda i: (1, i, 0)),
                  pl.BlockSpec((tm, w), lambda i: (i, 2)),
                  pl.BlockSpec((1, tm, w), lambda i: (0, i, 0)),
                  pl.BlockSpec((1, tm, w), lambda i: (1, i, 0)),
                  pl.BlockSpec((tm, w), lambda i: (i, 5)),
                  hy_spec,
                  pl.BlockSpec((1, GLA_DV), lambda i: (0, 0)),
                  pl.BlockSpec((1, 4 * w), lambda i: (0, 0))],
        out_specs=pl.BlockSpec((tm, 4 * w), row),
        out_shape=jax.ShapeDtypeStruct((m, 4 * w), BF16),
        compiler_params=_cp("arbitrary"),
        name="mix_assemble",
    )(y_s5.reshape(m, w), h_lru.reshape(2, m, w), h_lru.reshape(2, m, w), p2,
      o_gla.reshape(2, m, w), o_gla.reshape(2, m, w), p2, y_hy,
      lp["gla_norm_g"].reshape(1, GLA_DV), lp["mix_norm_g"].reshape(1, 4 * w))


FFN_LAG = 2
FFN_SUB_ROWS = 256


def _ffn1_body(h_ref, wg_ref, wu_ref, cw_ref, cb_ref, o_ref, g_scr, u_scr, *, tm, gw, tps, nt):
    s = pl.program_id(1)

    @pl.when(s == 0)
    def _():
        g_scr[...] = jnp.zeros_like(g_scr)
        u_scr[...] = jnp.zeros_like(u_scr)

    t = s - FFN_LAG
    cur = s % 3
    mid = (s + 1) % 3
    nxt = (s + 2) % 3

    rb = min(FFN_SUB_ROWS, tm)
    nsub = tm // rb
    gpt = tm // gw
    tn = o_ref.shape[1]
    col = lax.broadcasted_iota(jnp.int32, (gw, 1), 0)
    first_col = col == 0
    last_col = col == gw - 1

    def grid_row(rho, ls):
        if rho < 0:
            return jnp.where(t % tps == 0, 0.0, g_scr[cur, tm - gw:tm, ls])
        if rho >= gpt:
            return jnp.where(t % tps == tps - 1, 0.0, g_scr[nxt, 0:gw, ls])
        return g_scr[mid, rho * gw:(rho + 1) * gw, ls]

    def conv_finish(r):
        for rho in range(r * rb // gw, (r + 1) * rb // gw):
            for lh in range(tn // 128):
                ls = slice(lh * 128, (lh + 1) * 128)
                acc = jnp.broadcast_to(cb_ref[:, ls], (gw, 128))
                for dr in range(3):
                    src = grid_row(rho + dr - 1, ls)
                    left = jnp.where(first_col, 0.0, pltpu.roll(src, 1, 0))
                    right = jnp.where(last_col, 0.0, pltpu.roll(src, gw - 1, 0))
                    acc = (acc + cw_ref[3 * dr:3 * dr + 1, ls] * left + cw_ref[3 * dr + 1:3 * dr + 2, ls] * src
                           + cw_ref[3 * dr + 2:3 * dr + 3, ls] * right)
                rows = slice(rho * gw, (rho + 1) * gw)
                o_ref[rows, ls] = (acc * jax.nn.sigmoid(acc) * u_scr[mid, rows, ls]).astype(o_ref.dtype)

    @pl.when(s < nt)
    def _():
        for r in range(nsub):
            conv_finish(r)
            h = h_ref[r * rb:(r + 1) * rb, :]
            g_scr[cur, r * rb:(r + 1) * rb] = _dot(h, wg_ref[...])
            u_scr[cur, r * rb:(r + 1) * rb] = _dot(h, wu_ref[...])

    @pl.when(s >= nt)
    def _():
        for r in range(nsub):
            conv_finish(r)


def ffn1(h2, wg, wu, cw, cb, layer, *, seq, gw, tm, tn=256):
    m, d = h2.shape
    ff = wg.shape[-1]
    tps = seq // tm
    nt = m // tm
    return pl.pallas_call(
        functools.partial(_ffn1_body, tm=tm, gw=gw, tps=tps, nt=nt),
        grid=(ff // tn, nt + FFN_LAG),
        in_specs=[pl.BlockSpec((tm, d), lambda j, s: (jnp.minimum(s, nt - 1), 0)),
                  pl.BlockSpec((None, d, tn), lambda j, s: (layer, 0, j)),
                  pl.BlockSpec((None, d, tn), lambda j, s: (layer, 0, j)),
                  pl.BlockSpec((None, 9, tn), lambda j, s: (layer, 0, j)),
                  pl.BlockSpec((None, 1, tn), lambda j, s: (layer, 0, j))],
        out_specs=pl.BlockSpec((tm, tn), lambda j, s: (jnp.maximum(s - FFN_LAG, 0), j)),
        out_shape=jax.ShapeDtypeStruct((m, ff), BF16),
        scratch_shapes=[pltpu.VMEM((3, tm, tn), F32), pltpu.VMEM((3, tm, tn), F32)],
        compiler_params=_cp("arbitrary", "arbitrary"),
        name="ffn_gate_up",
    )(h2, wg, wu, cw, cb)


def _token_mixers(p2, bsz, seq, states, lp, wglu_bf, layer, need_output):
    p3 = p2.reshape(bsz, seq, N_COL_PAD)
    y_s5, st_s5 = s5_mixer(p3, states[0], lp, wglu_bf, layer, need_output)
    h_lru, st_lru = lru_mixer(p3, states[1], lp)
    o_gla, st_gla = gla_mixer(p3, states[2], lp)
    new_states = (st_s5, st_lru, st_gla)
    if not need_output:
        return None, new_states
    y_hy, hy_t = hyena_mixer(p3, lp)
    return mix_assemble(p2, y_s5, h_lru, o_gla, y_hy, hy_t, seq, lp), new_states


def kernel(x, c, ctx, c_ctx, w_ada, b_ada, norm_mix_g, norm_mlp_g, w_in, s5_lam_re, s5_lam_im, s5_log_step, s5_b_re, s5_b_im, s5_c_re, s5_c_im, s5_d, s5_w_glu, s5_b_glu, lru_conv_w, lru_conv_b, lru_w_a, lru_b_a, lru_w_x, lru_b_x, lru_lam, gla_w_alpha, gla_b_alpha, gla_norm_g, hy_conv_w, hy_conv_b, hy_w1, hy_b1, hy_w2, hy_b2, hy_w3, hy_freq, hy_bias, mix_norm_g, w_out, mlp_w_gate, mlp_w_up, mlp_conv_w, mlp_conv_b, mlp_w_down, final_norm_g):
    bsz, seq, d = x.shape
    clen = ctx.shape[1]
    depth = w_ada.shape[0]
    grid_w = 64
    params = dict(
        s5_lam_re=s5_lam_re, s5_lam_im=s5_lam_im, s5_log_step=s5_log_step, s5_b_re=s5_b_re, s5_b_im=s5_b_im,
        s5_c_re=s5_c_re, s5_c_im=s5_c_im, s5_d=s5_d, s5_b_glu=s5_b_glu,
        lru_conv_w=lru_conv_w, lru_conv_b=lru_conv_b, lru_w_a=lru_w_a, lru_b_a=lru_b_a, lru_w_x=lru_w_x,
        lru_b_x=lru_b_x, lru_lam=lru_lam, gla_w_alpha=gla_w_alpha, gla_b_alpha=gla_b_alpha, gla_norm_g=gla_norm_g,
        hy_conv_w=hy_conv_w, hy_conv_b=hy_conv_b, hy_w1=hy_w1, hy_b1=hy_b1, hy_w2=hy_w2, hy_b2=hy_b2, hy_w3=hy_w3,
        hy_freq=hy_freq, hy_bias=hy_bias, mix_norm_g=mix_norm_g)

    w_in_bf = jnp.concatenate(
        [w_in[..., 0:6144], w_in[..., 6176:9248], w_in[..., 6144:6176],
         jnp.zeros((depth, d, N_COL_PAD - 9248), w_in.dtype)], axis=-1).astype(BF16)
    w_out_bf = w_out.astype(BF16)
    wg_bf = mlp_w_gate.astype(BF16)
    wu_bf = mlp_w_up.astype(BF16)
    wd_bf = mlp_w_down.astype(BF16)
    wglu_bf = s5_w_glu.astype(BF16)
    conv_w9 = mlp_conv_w.reshape(depth, 9, D_FF)
    conv_b = mlp_conv_b.reshape(depth, 1, D_FF)

    cvec = jnp.zeros((8, d), F32).at[0:bsz].set(c).at[bsz].set(c_ctx)
    mod = ada_mod(cvec, w_ada, b_ada)

    x2 = x.reshape(bsz * seq, d)
    c2 = ctx.reshape(bsz * clen, d)
    zero_states = (jnp.zeros((bsz, 2, S5_JB, 1, 1024), F32),
                   jnp.zeros((bsz, 2, 1, GROUP_W), F32),
                   jnp.zeros((bsz, 2, GLA_HEADS, GLA_DK, GLA_DV), F32))

    for l in range(depth):
        last = l == depth - 1
        lp = {k: v[l] for k, v in params.items()}
        mx = mod[l, 0:bsz].reshape(bsz, 1, 6, d)
        mc = mod[l, bsz:bsz + 1].reshape(1, 1, 6, d)
        sh1, sc1, g1, sh2, sc2, g2 = (mx[:, :, i] for i in range(6))
        csh1, csc1, cg1, csh2, csc2, cg2 = (mc[:, :, i] for i in range(6))

        hc = normmod(c2, norm_mix_g[l], csh1, csc1, bsz * clen, BF16)
        pc = matmul(hc, w_in_bf, l, tm=bsz * clen, tn=512, name="in_proj_ctx")
        yc, ctx_states = _token_mixers(pc, bsz, clen, zero_states, lp, wglu_bf, l, need_output=not last)

        hx = normmod(x2, norm_mix_g[l], sh1, sc1, seq, BF16)
        px = matmul(hx, w_in_bf, l, tm=1024, tn=512, name="in_proj")
        yx, _ = _token_mixers(px, bsz, seq, ctx_states, lp, wglu_bf, l, need_output=True)
        x2 = matmul(yx, w_out_bf, l, tm=1024, tn=512, res=x2, gate=g1, rows_per_gate=seq, name="out_proj")
        h2 = normmod(x2, norm_mlp_g[l], sh2, sc2, seq, BF16)
        act = ffn1(h2, wg_bf, wu_bf, conv_w9, conv_b, l, seq=seq, gw=grid_w, tm=1024)
        x2 = matmul(act, wd_bf, l, tm=512, tn=256, res=x2, gate=g2, rows_per_gate=seq, name="down_proj")

        if not last:
            c2 = matmul(yc, w_out_bf, l, tm=bsz * clen, tn=512, res=c2, gate=cg1, rows_per_gate=bsz * clen,
                        name="out_proj_ctx")
            hc2 = normmod(c2, norm_mlp_g[l], csh2, csc2, bsz * clen, BF16)
            actc = ffn1(hc2, wg_bf, wu_bf, conv_w9, conv_b, l, seq=clen, gw=clen, tm=clen)
            c2 = matmul(actc, wd_bf, l, tm=bsz * clen, tn=512, res=c2, gate=cg2, rows_per_gate=bsz * clen,
                        name="down_proj_ctx")

    zeros = jnp.zeros((1, 1, d), F32)
    out = normmod(x2, final_norm_g, zeros, zeros, bsz * seq, F32)
    return out.reshape(bsz, seq, d)
```

```python
import functools
import math

import numpy as np
import jax
import jax.numpy as jnp
from jax import lax
from jax.experimental import pallas as pl
from jax.experimental.pallas import tpu as pltpu

F32 = jnp.float32
BF16 = jnp.bfloat16
HI = lax.Precision.HIGHEST

EPS = 1e-6
GROUP_W = 1024
N_COL_PAD = 9728
S5_GROUPS = 64
S5_CH = 16
S5_STATE = 64
S5_SEG = 8
S5_JB = 8
LRU_HEADS = 16
LRU_C = 8.0
GLA_HEADS = 4
GLA_DK = 128
GLA_DV = 256
GLA_CHUNK = 64
GLA_GATE_NORM = 16.0
HY_BANDS = 16
HY_FFN = 64
D_FF = 11008
V7X_VMEM_LIMIT = 56 * 1024 * 1024


def _cp(*sem):
    return pltpu.CompilerParams(dimension_semantics=sem, vmem_limit_bytes=V7X_VMEM_LIMIT)


def _dot(a, b):
    return jnp.dot(a, b, preferred_element_type=F32)


def _split(x):
    hi = x.astype(BF16)
    lo = (x - hi.astype(F32)).astype(BF16)
    return hi, lo


def _dot3(fh, fl, x):
    xh, xl = _split(x)
    return _dot(fh, xh) + _dot(fh, xl) + _dot(fl, xh)


def _np_split(a):
    bf = jnp.dtype(BF16)
    hi = np.asarray(a, np.float64).astype(bf)
    lo = (np.asarray(a, np.float64) - hi.astype(np.float64)).astype(bf)
    return jnp.asarray(hi), jnp.asarray(lo)


def _ada_body(c_ref, w_ref, b_ref, o_ref):
    c = c_ref[...]
    a = (c * jax.nn.sigmoid(c)).astype(BF16)
    o_ref[...] = _dot(a, w_ref[...].astype(BF16)) + b_ref[...]


def ada_mod(cvec, w_ada, b_ada, tn=512):
    depth, d, n = w_ada.shape
    return pl.pallas_call(
        _ada_body,
        grid=(depth, n // tn),
        in_specs=[pl.BlockSpec((8, d), lambda l, j: (0, 0)),
                  pl.BlockSpec((None, d, tn), lambda l, j: (l, 0, j)),
                  pl.BlockSpec((None, 1, tn), lambda l, j: (l, 0, j))],
        out_specs=pl.BlockSpec((None, 8, tn), lambda l, j: (l, 0, j)),
        out_shape=jax.ShapeDtypeStruct((depth, 8, n), F32),
        compiler_params=_cp("arbitrary", "arbitrary"),
        name="ada_mod",
    )(cvec, w_ada, b_ada.reshape(depth, 1, n))


def _normmod_body(x_ref, g_ref, sh_ref, sc_ref, o_ref):
    x = x_ref[...]
    y = x * lax.rsqrt(jnp.mean(x * x, axis=-1, keepdims=True) + EPS) * g_ref[...]
    o_ref[...] = (y * (1.0 + sc_ref[0]) + sh_ref[0]).astype(o_ref.dtype)


def normmod(x2d, g, sh, sc, rows_per_mod, out_dtype, tm=256):
    m, d = x2d.shape
    tpm = rows_per_mod // tm
    return pl.pallas_call(
        _normmod_body,
        grid=(m // tm,),
        in_specs=[pl.BlockSpec((tm, d), lambda i: (i, 0)),
                  pl.BlockSpec((1, d), lambda i: (0, 0)),
                  pl.BlockSpec((1, 1, d), lambda i: (i // tpm, 0, 0)),
                  pl.BlockSpec((1, 1, d), lambda i: (i // tpm, 0, 0))],
        out_specs=pl.BlockSpec((tm, d), lambda i: (i, 0)),
        out_shape=jax.ShapeDtypeStruct((m, d), out_dtype),
        compiler_params=_cp("arbitrary"),
        name="normmod",
    )(x2d, g.reshape(1, d), sh, sc)


def _mm_body(*refs, nk, has_res):
    if has_res:
        a_ref, w_ref, res_ref, gate_ref, o_ref = refs[:5]
        scr = refs[5:]
    else:
        a_ref, w_ref, o_ref = refs[:3]
        scr = refs[3:]

    def epilogue(acc):
        if has_res:
            o_ref[...] = res_ref[...] + gate_ref[0] * acc
        else:
            o_ref[...] = acc.astype(o_ref.dtype)

    if nk == 1:
        epilogue(_dot(a_ref[...], w_ref[...]))
    else:
        acc_ref = scr[0]
        k = pl.program_id(2)

        @pl.when(k == 0)
        def _():
            acc_ref[...] = jnp.zeros_like(acc_ref)

        acc_ref[...] += _dot(a_ref[...], w_ref[...])

        @pl.when(k == nk - 1)
        def _():
            epilogue(acc_ref[...])


def matmul(a, w, layer, *, tm, tn, tk=None, res=None, gate=None, rows_per_gate=None, name="matmul"):
    m, kdim = a.shape
    n = w.shape[-1]
    tk = kdim if tk is None else tk
    nk = kdim // tk
    has_res = res is not None
    in_specs = [pl.BlockSpec((tm, tk), lambda i, j, k: (i, k)),
                pl.BlockSpec((None, tk, tn), lambda i, j, k: (layer, k, j))]
    args = [a, w]
    if has_res:
        tpg = rows_per_gate // tm
        in_specs += [pl.BlockSpec((tm, tn), lambda i, j, k: (i, j)),
                     pl.BlockSpec((1, 1, tn), lambda i, j, k: (i // tpg, 0, j))]
        args += [res, gate]
    return pl.pallas_call(
        functools.partial(_mm_body, nk=nk, has_res=has_res),
        grid=(m // tm, n // tn, nk),
        in_specs=in_specs,
        out_specs=pl.BlockSpec((tm, tn), lambda i, j, k: (i, j)),
        out_shape=jax.ShapeDtypeStruct((m, n), F32),
        scratch_shapes=[pltpu.VMEM((tm, tn), F32)] if nk > 1 else [],
        compiler_params=_cp("arbitrary", "arbitrary", "arbitrary"),
        name=name,
    )(*args)


def _s5_params(lp, lseg):
    lam_re, lam_im = lp["s5_lam_re"], lp["s5_lam_im"]
    step = jnp.exp(lp["s5_log_step"])[:, :, None]
    mag = jnp.exp(lam_re * step)
    ab_re = mag * jnp.cos(lam_im * step)
    ab_im = mag * jnp.sin(lam_im * step)
    den = lam_re * lam_re + lam_im * lam_im
    co_re = ((ab_re - 1.0) * lam_re + ab_im * lam_im) / den
    co_im = (ab_im * lam_re - (ab_re - 1.0) * lam_im) / den
    b_re, b_im = lp["s5_b_re"], lp["s5_b_im"]
    bb_re = co_re[..., None] * b_re - co_im[..., None] * b_im
    bb_im = co_re[..., None] * b_im + co_im[..., None] * b_re
    eye = jnp.eye(8, dtype=F32)

    def in_blocks(bb):
        t = bb.reshape(2, S5_JB, 8, S5_STATE, S5_CH)
        t = jnp.einsum("djgpc,gh->djgchp", t, eye)
        return t.reshape(2, S5_JB, 8 * S5_CH, 8 * S5_STATE)

    def out_blocks(cc):
        t = cc.reshape(2, S5_JB, 8, S5_CH, S5_STATE)
        t = jnp.einsum("djgcp,gh->djgphc", t, eye)
        return t.reshape(2, S5_JB, 8 * S5_STATE, 8 * S5_CH)

    wb = jnp.concatenate([in_blocks(bb_re), in_blocks(bb_im)], axis=-1).astype(BF16)
    wc = jnp.concatenate([out_blocks(lp["s5_c_re"]), -out_blocks(lp["s5_c_im"])], axis=-2).astype(BF16)

    def lanes(t):
        return t.reshape(2, S5_JB, 1, 8 * S5_STATE)

    a = jnp.concatenate([lanes(ab_re), lanes(ab_im)], axis=-1)
    pr, pi = ab_re, ab_im
    for _ in range(int(round(math.log2(lseg)))):
        pr, pi = pr * pr - pi * pi, 2.0 * pr * pi
    al = jnp.concatenate([lanes(pr), lanes(pi)], axis=-1)
    return wb, wc, a, al


S5_JP = 2


def _s5_chains(bsz):
    return [(b, jj) for b in range(bsz) for jj in range(S5_JP)]


def _s5_project_in(u_ref, wb_ref, a_ref, bu_scr, ab_scr, bsz):
    for ch, (b, jj) in enumerate(_s5_chains(bsz)):
        bu_scr[ch] = _dot(u_ref[b, :, jj * 128:(jj + 1) * 128].astype(BF16), wb_ref[0, jj])
    for jj in range(S5_JP):
        ab_scr[jj] = jnp.broadcast_to(a_ref[0, jj], (S5_SEG, 1024))


def _s5_scan_tile(d, t_steps, bu_scr, ab_scr, h_scr, store, bsz):
    chains = _s5_chains(bsz)

    def step(s, carry):
        row = jnp.where(d == 0, s, t_steps - 1 - s)
        off = pl.multiple_of(row * S5_SEG, S5_SEG)
        out = []
        for ch, (_, jj) in enumerate(chains):
            hr, hi = carry[2 * ch], carry[2 * ch + 1]
            ar = ab_scr[jj, :, 0:512]
            ai = ab_scr[jj, :, 512:1024]
            nr = ar * hr - ai * hi + bu_scr[ch, pl.ds(off, S5_SEG), 0:512]
            ni = ar * hi + ai * hr + bu_scr[ch, pl.ds(off, S5_SEG), 512:1024]
            if store:
                bu_scr[ch, pl.ds(off, S5_SEG), 0:512] = nr
                bu_scr[ch, pl.ds(off, S5_SEG), 512:1024] = ni
            out += [nr, ni]
        return tuple(out)

    init = tuple(h_scr[ch, :, lo:lo + 512] for ch in range(len(chains)) for lo in (0, 512))
    fin = lax.fori_loop(0, t_steps, step, init, unroll=2)
    for ch in range(len(chains)):
        h_scr[ch, :, 0:512] = fin[2 * ch]
        h_scr[ch, :, 512:1024] = fin[2 * ch + 1]


def _s5_p1_body(u_ref, wb_ref, a_ref, al_ref, h0_ref, hinit_ref, fin_ref, bu_scr, ab_scr, h_scr, *, t_steps, nt, bsz):
    d = pl.program_id(0)
    i = pl.program_id(2)

    @pl.when(i == 0)
    def _():
        h_scr[...] = jnp.zeros_like(h_scr)

    _s5_project_in(u_ref, wb_ref, a_ref, bu_scr, ab_scr, bsz)
    _s5_scan_tile(d, t_steps, bu_scr, ab_scr, h_scr, False, bsz)

    @pl.when(i == nt - 1)
    def _():
        for ch, (b, jj) in enumerate(_s5_chains(bsz)):
            alr = al_ref[0, jj, :, 0:512]
            ali = al_ref[0, jj, :, 512:1024]
            cr = h0_ref[b, 0, jj, :, 0:512]
            ci = h0_ref[b, 0, jj, :, 512:1024]
            for s in range(S5_SEG):
                k = jnp.where(d == 0, s, S5_SEG - 1 - s)
                hinit_ref[b, 0, jj, pl.ds(k, 1), 0:512] = cr
                hinit_ref[b, 0, jj, pl.ds(k, 1), 512:1024] = ci
                fr = h_scr[ch, pl.ds(k, 1), 0:512]
                fi = h_scr[ch, pl.ds(k, 1), 512:1024]
                cr, ci = alr * cr - ali * ci + fr, alr * ci + ali * cr + fi
            fin_ref[b, 0, jj, :, 0:512] = cr
            fin_ref[b, 0, jj, :, 512:1024] = ci


def _s5_p2_body(u_ref, wb_ref, wc_ref, a_ref, hinit_ref, y_ref, bu_scr, ab_scr, h_scr, *, t_steps, bsz):
    d = pl.program_id(0)
    i = pl.program_id(2)

    @pl.when(i == 0)
    def _():
        for ch, (b, jj) in enumerate(_s5_chains(bsz)):
            h_scr[ch] = hinit_ref[b, 0, jj]

    _s5_project_in(u_ref, wb_ref, a_ref, bu_scr, ab_scr, bsz)
    _s5_scan_tile(d, t_steps, bu_scr, ab_scr, h_scr, True, bsz)
    for ch, (b, jj) in enumerate(_s5_chains(bsz)):
        y_ref[0, b, :, jj * 128:(jj + 1) * 128] = _dot(bu_scr[ch].astype(BF16), wc_ref[0, jj])


def _s5_fin_body(u_ref, yf_ref, yb_ref, d_ref, w_ref, b_ref, o_ref):
    y = u_ref[0] * d_ref[...] + yf_ref[0, 0] + yb_ref[0, 0]
    yg = jax.nn.gelu(y)
    o_ref[0] = yg * jax.nn.sigmoid(_dot(yg.astype(BF16), w_ref[...]) + b_ref[...])


def s5_mixer(p3, h0, lp, wglu_bf, layer, need_output):
    bsz, seq, _ = p3.shape
    lseg = seq // S5_SEG
    t_steps = min(64, lseg)
    nt = lseg // t_steps
    rows = t_steps * S5_SEG
    wb, wc, a, al = _s5_params(lp, lseg)
    u_perm = p3[:, :, 0:GROUP_W].reshape(bsz, S5_SEG, lseg, GROUP_W).transpose(0, 2, 1, 3).reshape(bsz, seq, GROUP_W)

    def tile(d, i):
        return jnp.where(d == 0, i, nt - 1 - i)

    nchain = bsz * S5_JP
    grid = (2, S5_JB // S5_JP, nt)
    u_spec = pl.BlockSpec((bsz, rows, 128 * S5_JP), lambda d, j, i: (0, tile(d, i), j))
    wb_spec = pl.BlockSpec((1, S5_JP, 128, 1024), lambda d, j, i: (d, j, 0, 0))
    a_spec = pl.BlockSpec((1, S5_JP, 1, 1024), lambda d, j, i: (d, j, 0, 0))
    st1_spec = pl.BlockSpec((bsz, 1, S5_JP, 1, 1024), lambda d, j, i: (0, d, j, 0, 0))
    st8_spec = pl.BlockSpec((bsz, 1, S5_JP, S5_SEG, 1024), lambda d, j, i: (0, d, j, 0, 0))
    scratch = [pltpu.VMEM((nchain, rows, 1024), F32), pltpu.VMEM((S5_JP, S5_SEG, 1024), F32),
               pltpu.VMEM((nchain, S5_SEG, 1024), F32)]
    hinit, fin = pl.pallas_call(
        functools.partial(_s5_p1_body, t_steps=t_steps, nt=nt, bsz=bsz),
        grid=grid,
        in_specs=[u_spec, wb_spec, a_spec, a_spec, st1_spec],
        out_specs=[st8_spec, st1_spec],
        out_shape=[jax.ShapeDtypeStruct((bsz, 2, S5_JB, S5_SEG, 1024), F32),
                   jax.ShapeDtypeStruct((bsz, 2, S5_JB, 1, 1024), F32)],
        scratch_shapes=scratch,
        compiler_params=_cp("arbitrary", "arbitrary", "arbitrary"),
        name="s5_pass1",
    )(u_perm, wb, a, al, h0)
    if not need_output:
        return None, fin
    y = pl.pallas_call(
        functools.partial(_s5_p2_body, t_steps=t_steps, bsz=bsz),
        grid=grid,
        in_specs=[u_spec, wb_spec,
                  pl.BlockSpec((1, S5_JP, 1024, 128), lambda d, j, i: (d, j, 0, 0)),
                  a_spec, st8_spec],
        out_specs=pl.BlockSpec((1, bsz, rows, 128 * S5_JP), lambda d, j, i: (d, 0, tile(d, i), j)),
        out_shape=jax.ShapeDtypeStruct((2, bsz, seq, GROUP_W), F32),
        scratch_shapes=scratch,
        compiler_params=_cp("arbitrary", "arbitrary", "arbitrary"),
        name="s5_pass2",
    )(u_perm, wb, wc, a, hinit)
    tr = min(512, seq)
    out = pl.pallas_call(
        _s5_fin_body,
        grid=(bsz, seq // tr),
        in_specs=[pl.BlockSpec((1, tr, GROUP_W), lambda b, i: (b, i, 0)),
                  pl.BlockSpec((1, 1, tr, GROUP_W), lambda b, i: (0, b, i, 0)),
                  pl.BlockSpec((1, 1, tr, GROUP_W), lambda b, i: (1, b, i, 0)),
                  pl.BlockSpec((1, GROUP_W), lambda b, i: (0, 0)),
                  pl.BlockSpec((None, GROUP_W, GROUP_W), lambda b, i: (layer, 0, 0)),
                  pl.BlockSpec((1, GROUP_W), lambda b, i: (0, 0))],
        out_specs=pl.BlockSpec((1, tr, GROUP_W), lambda b, i: (b, i, 0)),
        out_shape=jax.ShapeDtypeStruct((bsz, seq, GROUP_W), F32),
        compiler_params=_cp("arbitrary", "arbitrary"),
        name="s5_finalize",
    )(u_perm, y, y, lp["s5_d"].reshape(1, GROUP_W), wglu_bf, lp["s5_b_glu"].reshape(1, GROUP_W))
    out = out.reshape(bsz, lseg, S5_SEG, GROUP_W).transpose(0, 2, 1, 3).reshape(bsz, seq, GROUP_W)
    return out, fin


def _lru_body(xp_ref, xm_ref, xn_ref, cw_ref, cb_ref, wg_ref, bg_ref, sp_ref, h0_ref, h_ref, fin_ref,
              a_scr, b_scr, hc_scr, *, tile_rows, nt, seq):
    d = pl.program_id(0)
    i = pl.program_id(1)
    ti = jnp.where(d == 0, i, nt - 1 - i)
    bsz = xm_ref.shape[0]

    @pl.when(i == 0)
    def _():
        hc_scr[...] = h0_ref[:, 0]

    n = tile_rows + 16
    rowid = lax.broadcasted_iota(jnp.int32, (n, 1), 0) + (ti * tile_rows - 8)
    valid = (rowid >= 0) & (rowid < seq)
    cw = cw_ref[...]
    for b in range(bsz):
        xe = jnp.where(valid, jnp.concatenate([xp_ref[b], xm_ref[b], xn_ref[b]], axis=0), 0.0)
        xc = (cb_ref[...]
              + cw[0:1] * pltpu.roll(xe, 2, 0)[8:8 + tile_rows]
              + cw[1:2] * pltpu.roll(xe, 1, 0)[8:8 + tile_rows]
              + cw[2:3] * xe[8:8 + tile_rows]
              + cw[3:4] * pltpu.roll(xe, n - 1, 0)[8:8 + tile_rows])
        for cb in range(4):
            lo, hi = cb * 256, (cb + 1) * 256
            xcb = xc[:, lo:hi]
            pre = _dot(xcb.astype(BF16), wg_ref[0, cb])
            r = jax.nn.sigmoid(pre[:, 0:256] + bg_ref[0, :, lo:hi])
            ig = jax.nn.sigmoid(pre[:, 256:512] + bg_ref[0, :, GROUP_W + lo:GROUP_W + hi])
            log_a = -LRU_C * sp_ref[0, :, lo:hi] * r
            a = jnp.exp(log_a)
            a_scr[b, :, lo:hi] = a
            b_scr[b, :, lo:hi] = jnp.sqrt(jnp.tanh(-log_a) * (a * a + 1.0)) * (ig * xcb)

    def step(s, hs):
        t = jnp.where(d == 0, s, tile_rows - 1 - s)
        out = []
        for b in range(bsz):
            h = a_scr[b, pl.ds(t, 1), :] * hs[b] + b_scr[b, pl.ds(t, 1), :]
            b_scr[b, pl.ds(t, 1), :] = h
            out.append(h)
        return tuple(out)

    hs = lax.fori_loop(0, tile_rows, step, tuple(hc_scr[b] for b in range(bsz)), unroll=8)
    for b in range(bsz):
        hc_scr[b] = hs[b]
        fin_ref[b, 0] = hs[b]
    h_ref[0] = b_scr[...]


def _lru_params(lp):
    def blockdiag(w):
        t = w.reshape(2, 4, 4, 64, 64)
        t = jnp.einsum("dcgij,gh->dcgihj", t, jnp.eye(4, dtype=F32))
        return t.reshape(2, 4, 256, 256)

    wg = jnp.concatenate([blockdiag(lp["lru_w_a"]), blockdiag(lp["lru_w_x"])], axis=-1).astype(BF16)
    bg = jnp.concatenate([lp["lru_b_a"], lp["lru_b_x"]], axis=-1).reshape(2, 1, 2 * GROUP_W)
    sp = jax.nn.softplus(-lp["lru_lam"]).reshape(2, 1, GROUP_W)
    return wg, bg, sp


def lru_mixer(p3, h0, lp):
    bsz, seq, _ = p3.shape
    tr = min(512, seq)
    nt = seq // tr
    wg, bg, sp = _lru_params(lp)
    nb8 = seq // 8

    def tile(d, i):
        return jnp.where(d == 0, i, nt - 1 - i)

    return pl.pallas_call(
        functools.partial(_lru_body, tile_rows=tr, nt=nt, seq=seq),
        grid=(2, nt),
        in_specs=[pl.BlockSpec((bsz, 8, GROUP_W), lambda d, i: (0, jnp.maximum(tile(d, i) * (tr // 8) - 1, 0), 1)),
                  pl.BlockSpec((bsz, tr, GROUP_W), lambda d, i: (0, tile(d, i), 1)),
                  pl.BlockSpec((bsz, 8, GROUP_W), lambda d, i: (0, jnp.minimum((tile(d, i) + 1) * (tr // 8), nb8 - 1), 1)),
                  pl.BlockSpec((4, GROUP_W), lambda d, i: (0, 0)),
                  pl.BlockSpec((1, GROUP_W), lambda d, i: (0, 0)),
                  pl.BlockSpec((1, 4, 256, 512), lambda d, i: (d, 0, 0, 0)),
                  pl.BlockSpec((1, 1, 2 * GROUP_W), lambda d, i: (d, 0, 0)),
                  pl.BlockSpec((1, 1, GROUP_W), lambda d, i: (d, 0, 0)),
                  pl.BlockSpec((bsz, 1, 1, GROUP_W), lambda d, i: (0, d, 0, 0))],
        out_specs=[pl.BlockSpec((1, bsz, tr, GROUP_W), lambda d, i: (d, 0, tile(d, i), 0)),
                   pl.BlockSpec((bsz, 1, 1, GROUP_W), lambda d, i: (0, d, 0, 0))],
        out_shape=[jax.ShapeDtypeStruct((2, bsz, seq, GROUP_W), F32),
                   jax.ShapeDtypeStruct((bsz, 2, 1, GROUP_W), F32)],
        scratch_shapes=[pltpu.VMEM((bsz, tr, GROUP_W), F32), pltpu.VMEM((bsz, tr, GROUP_W), F32),
                        pltpu.VMEM((bsz, 1, GROUP_W), F32)],
        compiler_params=_cp("arbitrary", "arbitrary"),
        name="lru_scan",
    )(p3, p3, p3, lp["lru_conv_w"], lp["lru_conv_b"].reshape(1, GROUP_W), wg, bg, sp, h0)


def _log_sigmoid(z):
    return jnp.minimum(z, 0.0) - jnp.log1p(jnp.exp(-jnp.abs(z)))


def _gla_body(q_ref, k_ref, v_ref, lr_ref, wa_ref, ba_ref, tri_ref, s0_ref, o_ref, sfin_ref, s_scr, *, nch):
    d = pl.program_id(1)
    i = pl.program_id(2)
    c = GLA_CHUNK

    @pl.when(i == 0)
    def _():
        s_scr[...] = s0_ref[0, 0]

    tri = tri_ref[0]

    def chunk(s, carry):
        ci = jnp.where(d == 0, s, nch - 1 - s)
        r0 = pl.multiple_of(ci * c, c)
        z = _dot(lr_ref[0, pl.ds(r0, c), :].astype(BF16), wa_ref[0]) + ba_ref[0]
        la = _log_sigmoid(z) / GLA_GATE_NORM
        tri_bf = tri.astype(BF16)
        la_hi = la.astype(BF16)
        la_r = la - la_hi.astype(F32)
        la_mid = la_r.astype(BF16)
        la_lo = (la_r - la_mid.astype(F32)).astype(BF16)
        b_all = _dot(tri_bf, la_hi) + _dot(tri_bf, la_mid) + _dot(tri_bf, la_lo)
        q_all = q_ref[0, pl.ds(r0, c), :] * (GLA_DK ** -0.5)
        k_all = k_ref[0, pl.ds(r0, c), :]
        v_all = v_ref[0, pl.ds(r0, c), :].astype(BF16)
        for h in range(GLA_HEADS):
            ks = slice(h * GLA_DK, (h + 1) * GLA_DK)
            vs = slice(h * GLA_DV, (h + 1) * GLA_DV)
            q, k, v, b = q_all[:, ks], k_all[:, ks], v_all[:, vs], b_all[:, ks]
            b_mid = jnp.where(d == 0, b[c // 2 - 1:c // 2], b[c // 2:c // 2 + 1])
            qd = (q * jnp.exp(b - b_mid)).astype(BF16)
            kd = (k * jnp.exp(b_mid - b)).astype(BF16)
            sc = lax.dot_general(qd, kd, (((1,), (1,)), ((), ())), preferred_element_type=F32) * tri
            intra = _dot(sc.astype(BF16), v)
            st = s_scr[h]
            inter = _dot((q * jnp.exp(b)).astype(BF16), st.astype(BF16))
            o_ref[0, 0, pl.ds(r0, c), vs] = intra + inter
            kt = k.T
            bt = b.T
            bt_last = jnp.where(d == 0, bt[:, c - 1:c], bt[:, 0:1])
            k2t = (kt * jnp.exp(bt_last - bt)).astype(BF16)
            s_scr[h] = jnp.exp(bt_last) * st + _dot(k2t, v)
        return carry

    lax.fori_loop(0, nch, chunk, 0)
    sfin_ref[0, 0] = s_scr[...]


def gla_mixer(p3, s0, lp):
    bsz, seq, _ = p3.shape
    tr = min(512, seq)
    nt = seq // tr
    qkw = GLA_HEADS * GLA_DK
    wa_pad = jnp.zeros((2, 128, qkw), F32)
    wa_pad = wa_pad.at[0, 0:16].set(lp["gla_w_alpha"][0]).at[1, 16:32].set(lp["gla_w_alpha"][1]).astype(BF16)
    ba = lp["gla_b_alpha"].reshape(2, 1, qkw)
    lower = np.tril(np.ones((GLA_CHUNK, GLA_CHUNK), np.float32))
    tri = jnp.asarray(np.stack([lower, lower.T]))

    def tile(d, i):
        return jnp.where(d == 0, i, nt - 1 - i)

    return pl.pallas_call(
        functools.partial(_gla_body, nch=tr // GLA_CHUNK),
        grid=(bsz, 2, nt),
        in_specs=[pl.BlockSpec((1, tr, qkw), lambda b, d, i: (b, tile(d, i), 3072 // qkw)),
                  pl.BlockSpec((1, tr, qkw), lambda b, d, i: (b, tile(d, i), 3584 // qkw)),
                  pl.BlockSpec((1, tr, GROUP_W), lambda b, d, i: (b, tile(d, i), 4)),
                  pl.BlockSpec((1, tr, 128), lambda b, d, i: (b, tile(d, i), 72)),
                  pl.BlockSpec((1, 128, qkw), lambda b, d, i: (d, 0, 0)),
                  pl.BlockSpec((1, 1, qkw), lambda b, d, i: (d, 0, 0)),
                  pl.BlockSpec((1, GLA_CHUNK, GLA_CHUNK), lambda b, d, i: (d, 0, 0)),
                  pl.BlockSpec((1, 1, GLA_HEADS, GLA_DK, GLA_DV), lambda b, d, i: (b, d, 0, 0, 0))],
        out_specs=[pl.BlockSpec((1, 1, tr, GROUP_W), lambda b, d, i: (d, b, tile(d, i), 0)),
                   pl.BlockSpec((1, 1, GLA_HEADS, GLA_DK, GLA_DV), lambda b, d, i: (b, d, 0, 0, 0))],
        out_shape=[jax.ShapeDtypeStruct((2, bsz, seq, GROUP_W), F32),
                   jax.ShapeDtypeStruct((bsz, 2, GLA_HEADS, GLA_DK, GLA_DV), F32)],
        scratch_shapes=[pltpu.VMEM((GLA_HEADS, GLA_DK, GLA_DV), F32)],
        compiler_params=_cp("arbitrary", "arbitrary", "arbitrary"),
        name="gla_scan",
    )(p3, p3, p3, p3, wa_pad, ba, tri, s0)


def _hy_conv3_body(x_ref, w_ref, b_ref, o_ref, *, seq):
    x = x_ref[0]
    w = w_ref[...]
    row = lax.broadcasted_iota(jnp.int32, (seq, 1), 0)
    xm = jnp.where(row == 0, 0.0, pltpu.roll(x, 1, 0))
    xp = jnp.where(row == seq - 1, 0.0, pltpu.roll(x, seq - 1, 0))
    o_ref[0] = w[0:1] * xm + w[1:2] * x + w[2:3] * xp + b_ref[...]


def hy_conv3(p3, lp):
    bsz, seq, _ = p3.shape
    w3 = 3 * GROUP_W
    return pl.pallas_call(
        functools.partial(_hy_conv3_body, seq=seq),
        grid=(bsz, w3 // 128),
        in_specs=[pl.BlockSpec((1, seq, 128), lambda b, c: (b, 0, 48 + c)),
                  pl.BlockSpec((3, 128), lambda b, c: (0, c)),
                  pl.BlockSpec((1, 128), lambda b, c: (0, c))],
        out_specs=pl.BlockSpec((1, seq, 128), lambda b, c: (b, 0, c)),
        out_shape=jax.ShapeDtypeStruct((bsz, seq, w3), F32),
        compiler_params=_cp("arbitrary", "arbitrary"),
        name="hy_conv3",
    )(p3, lp["hy_conv_w"], lp["hy_conv_b"].reshape(1, w3))


def _hy_filter_body(bv_ref, w1_ref, b1_ref, f0_ref, w2_ref, b2_ref, f1_ref, w3_ref, dl_ref, o_ref, *, tr, seq):
    i = pl.program_id(0)
    n = 2 * seq
    t = lax.broadcasted_iota(jnp.int32, (tr, 1), 0) + i * tr
    pos = jnp.where(t < seq, t, n - t).astype(F32)
    tt = pos / seq
    w = (2.0 * math.pi) * pos / seq
    lane = lax.broadcasted_iota(jnp.int32, (tr, 128), 1)
    arg = w * bv_ref[...]
    feats = jnp.where(lane == 0, tt,
                      jnp.where(lane <= HY_BANDS, jnp.cos(arg),
                                jnp.where(lane <= 2 * HY_BANDS, -jnp.sin(arg), 0.0)))
    h = jnp.sin(f0_ref[...] * (jnp.dot(feats, w1_ref[...], preferred_element_type=F32, precision=HI) + b1_ref[...]))
    h = jnp.sin(f1_ref[...] * (jnp.dot(h, w2_ref[...], preferred_element_type=F32, precision=HI) + b2_ref[...]))
    out = jnp.dot(h, w3_ref[0], preferred_element_type=F32, precision=HI)
    out = out * jnp.exp(-tt * dl_ref[...])
    o_ref[...] = jnp.where(t == seq, 0.0, out)


def hy_filter(lp, seq):
    n = 2 * seq
    tr = min(512, seq)
    nt = n // tr
    bands = np.linspace(1e-4, HY_BANDS - 1, HY_BANDS, dtype=np.float32)
    bv = np.zeros((1, 128), np.float32)
    bv[0, 1:1 + HY_BANDS] = bands
    bv[0, 1 + HY_BANDS:1 + 2 * HY_BANDS] = bands
    deltas = np.abs(np.linspace(math.log(1e-2) / 0.3, math.log(1e-2) / 1.5, GROUP_W, dtype=np.float32))
    dl = np.concatenate([deltas, deltas])[None, :]

    def pad2(w, r, c):
        return jnp.zeros((r, c), F32).at[:w.shape[0], :w.shape[1]].set(w)

    w1 = pad2(lp["hy_w1"], 128, 128)
    b1 = pad2(lp["hy_b1"][None, :], 1, 128)
    f0 = pad2(lp["hy_freq"][0][None, :], 1, 128)
    w2 = pad2(lp["hy_w2"], 128, 128)
    b2 = pad2(lp["hy_b2"][None, :], 1, 128)
    f1 = pad2(lp["hy_freq"][1][None, :], 1, 128)
    w3 = lp["hy_w3"].reshape(HY_FFN, 2, 2 * GROUP_W).transpose(1, 0, 2)
    w3 = jnp.zeros((2, 128, 2 * GROUP_W), F32).at[:, :HY_FFN].set(w3)
    half = nt // 2
    vec = lambda i: (0, 0)
    return pl.pallas_call(
        functools.partial(_hy_filter_body, tr=tr, seq=seq),
        grid=(nt,),
        in_specs=[pl.BlockSpec((1, 128), vec), pl.BlockSpec((128, 128), vec), pl.BlockSpec((1, 128), vec),
                  pl.BlockSpec((1, 128), vec), pl.BlockSpec((128, 128), vec), pl.BlockSpec((1, 128), vec),
                  pl.BlockSpec((1, 128), vec),
                  pl.BlockSpec((1, 128, 2 * GROUP_W), lambda i: (jnp.where(i < half, 0, 1), 0, 0)),
                  pl.BlockSpec((1, 2 * GROUP_W), vec)],
        out_specs=pl.BlockSpec((tr, 2 * GROUP_W), lambda i: (i, 0)),
        out_shape=jax.ShapeDtypeStruct((n, 2 * GROUP_W), F32),
        compiler_params=_cp("arbitrary"),
        name="hy_filter",
    )(jnp.asarray(bv), w1, b1, f0, w2, b2, f1, w3, jnp.asarray(dl))


HY_N1 = 128
HY_N2 = 128


@functools.lru_cache(maxsize=None)
def _dft_tables_t():
    n = HY_N1 * HY_N2
    a = np.arange(128)
    ph = 2.0 * np.pi * ((a[:, None] * a[None, :]) % 128) / 128
    cm, sm = np.cos(ph), np.sin(ph)
    f1 = np.concatenate([cm, -sm], axis=1)
    pht = 2.0 * np.pi * (a[:, None] * a[None, :]) / n
    twr, twi = np.cos(pht), -np.sin(pht)
    f2 = np.block([[cm, -sm], [sm, cm]])
    g2 = np.block([[cm, sm], [-sm, cm]])
    g1 = np.concatenate([cm, -sm], axis=0) / n
    return f1, twr, twi, f2, g2, g1


def _hy_tables_args():
    f1, twr, twi, f2, g2, g1 = _dft_tables_t()
    out = [_np_split(m)[0] for m in (f1, f2, g2, g1)]
    out += [jnp.asarray(twr, F32), jnp.asarray(twi, F32)]
    return out


def _full_spec(shape):
    nd = len(shape)
    return pl.BlockSpec(shape, lambda *_: (0,) * nd)


def _dotr(x, f):
    return _dot(x.astype(BF16), f)


def _hy_fwd_t(z3, f1, f2, twr, twi):
    cb = z3.shape[0]
    x = jnp.swapaxes(z3, 1, 2).reshape(cb * 128, 128)
    a3 = _dotr(x, f1).reshape(cb, 128, 256)
    ar, ai = a3[:, :, 0:128], a3[:, :, 128:256]
    br = ar * twr - ai * twi
    bi = ar * twi + ai * twr
    x2 = jnp.concatenate([jnp.swapaxes(br, 1, 2), jnp.swapaxes(bi, 1, 2)], axis=2)
    return _dotr(x2.reshape(cb * 128, 256), f2)


def _hy_inv_t(y, cb, g2, g1, twr, twi):
    b3 = _dotr(y, g2).reshape(cb, 128, 256)
    br = jnp.swapaxes(b3[:, :, 0:128], 1, 2)
    bi = jnp.swapaxes(b3[:, :, 128:256], 1, 2)
    cr = br * twr + bi * twi
    ci = bi * twr - br * twi
    x4 = jnp.concatenate([cr, ci], axis=2).reshape(cb * 128, 256)
    yv = _dotr(x4, g1).reshape(cb, 128, 128)
    return jnp.swapaxes(yv, 1, 2)


HY_SUB = 8


def _hy_spec_t_body(c_ref, f1, f2, g2, g1, twr, twi, o_ref):
    for s in range(c_ref.shape[0] // HY_SUB):
        cs = slice(s * HY_SUB, (s + 1) * HY_SUB)
        z = _hy_fwd_t(c_ref[cs], f1[...], f2[...], twr[...], twi[...])
        o_ref[cs] = z.reshape(HY_SUB, 128, 256)


def _hy_long_t_body(u_ref, g_ref, bias_ref, h_ref, f1, f2, g2, g1, twr, twi, o_ref):
    t1u = u_ref.shape[2]
    cb = HY_SUB
    for s in range(u_ref.shape[1] // cb):
        cs = slice(s * cb, (s + 1) * cb)
        u = u_ref[0, cs]
        z3 = jnp.concatenate([u, jnp.zeros((cb, HY_N1 - t1u, HY_N2), F32)], axis=1)
        z = _hy_fwd_t(z3, f1[...], f2[...], twr[...], twi[...])
        hs = h_ref[cs].reshape(cb * 128, 256)
        zr, zi, hr, hi = z[:, 0:128], z[:, 128:256], hs[:, 0:128], hs[:, 128:256]
        y = jnp.concatenate([zr * hr - zi * hi, zr * hi + zi * hr], axis=1)
        conv = _hy_inv_t(y, cb, g2[...], g1[...], twr[...], twi[...])[:, 0:t1u, :]
        o_ref[0, cs] = g_ref[0, cs] * (conv + bias_ref[cs] * u)


def hy_conv3_t(p3, lp):
    bsz, seq, _ = p3.shape
    w3 = 3 * GROUP_W

    def body(x_ref, w_ref, b_ref, o_ref):
        x = x_ref[0]
        w = w_ref[...]
        row = lax.broadcasted_iota(jnp.int32, (seq, 1), 0)
        xm = jnp.where(row == 0, 0.0, pltpu.roll(x, 1, 0))
        xp = jnp.where(row == seq - 1, 0.0, pltpu.roll(x, seq - 1, 0))
        o_ref[0] = (w[0:1] * xm + w[1:2] * x + w[2:3] * xp + b_ref[...]).T

    return pl.pallas_call(
        body,
        grid=(bsz, w3 // 128),
        in_specs=[pl.BlockSpec((1, seq, 128), lambda b, c: (b, 0, 48 + c)),
                  pl.BlockSpec((3, 128), lambda b, c: (0, c)),
                  pl.BlockSpec((1, 128), lambda b, c: (0, c))],
        out_specs=pl.BlockSpec((1, 128, seq), lambda b, c: (b, c, 0)),
        out_shape=jax.ShapeDtypeStruct((bsz, w3, seq), F32),
        compiler_params=_cp("arbitrary", "arbitrary"),
        name="hy_conv3_t",
    )(p3, lp["hy_conv_w"], lp["hy_conv_b"].reshape(1, w3))


def _hy_filter_t_body(bc_ref, w1_ref, b1_ref, f0_ref, w2_ref, b2_ref, f1_ref, w3_ref, dl_ref, o_ref, *, tr, seq):
    i = pl.program_id(0)
    n = 2 * seq
    t = lax.broadcasted_iota(jnp.int32, (1, tr), 1) + i * tr
    pos = jnp.where(t < seq, t, n - t).astype(F32)
    tt = pos / seq
    w = (2.0 * math.pi) * pos / seq
    row = lax.broadcasted_iota(jnp.int32, (128, tr), 0)
    arg = bc_ref[...] * w
    feats = jnp.where(row == 0, tt,
                      jnp.where(row <= HY_BANDS, jnp.cos(arg),
                                jnp.where(row <= 2 * HY_BANDS, -jnp.sin(arg), 0.0)))
    h = jnp.sin(f0_ref[...] * (jnp.dot(w1_ref[...], feats, preferred_element_type=F32, precision=HI) + b1_ref[...]))
    h = jnp.sin(f1_ref[...] * (jnp.dot(w2_ref[...], h, preferred_element_type=F32, precision=HI) + b2_ref[...]))
    out = jnp.dot(w3_ref[0], h, preferred_element_type=F32, precision=HI)
    out = out * jnp.exp(-dl_ref[...] * tt)
    o_ref[...] = jnp.where(t == seq, 0.0, out)


def hy_filter_t(lp, seq):
    n = 2 * seq
    tr = 512
    nt = n // tr
    bands = np.linspace(1e-4, HY_BANDS - 1, HY_BANDS, dtype=np.float32)
    bc = np.zeros((128, 1), np.float32)
    bc[1:1 + HY_BANDS, 0] = bands
    bc[1 + HY_BANDS:1 + 2 * HY_BANDS, 0] = bands
    deltas = np.abs(np.linspace(math.log(1e-2) / 0.3, math.log(1e-2) / 1.5, GROUP_W, dtype=np.float32))
    dl = np.concatenate([deltas, deltas])[:, None]

    def pad2(w, r, c):
        return jnp.zeros((r, c), F32).at[:w.shape[0], :w.shape[1]].set(w)

    w1 = pad2(lp["hy_w1"].T, 128, 128)
    b1 = pad2(lp["hy_b1"][:, None], 128, 1)
    f0 = pad2(lp["hy_freq"][0][:, None], 128, 1)
    w2 = pad2(lp["hy_w2"].T, 128, 128)
    b2 = pad2(lp["hy_b2"][:, None], 128, 1)
    f1 = pad2(lp["hy_freq"][1][:, None], 128, 1)
    w3 = lp["hy_w3"].reshape(HY_FFN, 2, 2 * GROUP_W).transpose(1, 2, 0)
    w3 = jnp.zeros((2, 2 * GROUP_W, 128), F32).at[:, :, :HY_FFN].set(w3)
    half = nt // 2
    return pl.pallas_call(
        functools.partial(_hy_filter_t_body, tr=tr, seq=seq),
        grid=(nt,),
        in_specs=[_full_spec((128, 1)), _full_spec((128, 128)), _full_spec((128, 1)), _full_spec((128, 1)),
                  _full_spec((128, 128)), _full_spec((128, 1)), _full_spec((128, 1)),
                  pl.BlockSpec((1, 2 * GROUP_W, 128), lambda i: (jnp.where(i < half, 0, 1), 0, 0)),
                  _full_spec((2 * GROUP_W, 1))],
        out_specs=pl.BlockSpec((2 * GROUP_W, tr), lambda i: (0, i)),
        out_shape=jax.ShapeDtypeStruct((2 * GROUP_W, n), F32),
        compiler_params=_cp("arbitrary"),
        name="hy_filter_t",
    )(jnp.asarray(bc), w1, b1, f0, w2, b2, f1, w3, jnp.asarray(dl))


HY_CB = 32


def hyena_long(p3, lp):
    bsz, seq, _ = p3.shape
    t1u = seq // HY_N2
    tabs = _hy_tables_args()
    tab_specs = [_full_spec(t.shape) for t in tabs]
    hzt = hy_conv3_t(p3, lp).reshape(bsz, 3 * GROUP_W, t1u, HY_N2)
    circ = hy_filter_t(lp, seq).reshape(2 * GROUP_W, HY_N1, HY_N2)
    cb = HY_CB
    hspec = pl.pallas_call(
        _hy_spec_t_body,
        grid=(2 * GROUP_W // cb,),
        in_specs=[pl.BlockSpec((cb, HY_N1, HY_N2), lambda c: (c, 0, 0))] + tab_specs,
        out_specs=pl.BlockSpec((cb, HY_N1, 2 * HY_N2), lambda c: (c, 0, 0)),
        out_shape=jax.ShapeDtypeStruct((2 * GROUP_W, HY_N1, 2 * HY_N2), F32),
        compiler_params=_cp("arbitrary"),
        name="hy_spec_t",
    )(circ, *tabs)
    ncb = GROUP_W // cb
    u, ub0 = hzt, 0
    for o in range(2):
        gb0 = (1 + o) * ncb
        u = pl.pallas_call(
            _hy_long_t_body,
            grid=(ncb, bsz),
            in_specs=[pl.BlockSpec((1, cb, t1u, HY_N2), lambda c, b, ub0=ub0: (b, ub0 + c, 0, 0)),
                      pl.BlockSpec((1, cb, t1u, HY_N2), lambda c, b, gb0=gb0: (b, gb0 + c, 0, 0)),
                      pl.BlockSpec((cb, 1, 1), lambda c, b: (c, 0, 0)),
                      pl.BlockSpec((cb, HY_N1, 2 * HY_N2), lambda c, b, o=o: (o * ncb + c, 0, 0))] + tab_specs,
            out_specs=pl.BlockSpec((1, cb, t1u, HY_N2), lambda c, b: (b, c, 0, 0)),
            out_shape=jax.ShapeDtypeStruct((bsz, GROUP_W, t1u, HY_N2), F32),
            compiler_params=_cp("arbitrary", "arbitrary"),
            name="hy_long_t",
        )(u, hzt, lp["hy_bias"][o].reshape(GROUP_W, 1, 1), hspec, *tabs)
        ub0 = 0
    return u.reshape(bsz, GROUP_W, seq)


@functools.lru_cache(maxsize=None)
def _dense_dft_tables(seq):
    n = 2 * seq
    k = np.arange(n)[:, None]
    t = np.arange(n)[None, :]
    ph = 2.0 * np.pi * ((k * t) % n) / n
    fwd = np.concatenate([np.cos(ph), -np.sin(ph)], axis=0)
    inv = np.concatenate([np.cos(ph).T, -np.sin(ph).T], axis=1)[:seq] / n
    return fwd, inv


def _hy_dense_spec_body(c_ref, fh_ref, fl_ref, o_ref):
    o_ref[...] = _dot3(fh_ref[...], fl_ref[...], c_ref[...])


def _hy_dense_body(z_ref, h_ref, fh_ref, fl_ref, gh_ref, gl_ref, g_ref, bias_ref, o_ref, *, n):
    u = z_ref[0]
    z = _dot3(fh_ref[...], fl_ref[...], u)
    zr, zi = z[0:n], z[n:]
    hr, hi = h_ref[0:n], h_ref[n:2 * n]
    y = jnp.concatenate([zr * hr - zi * hi, zr * hi + zi * hr], axis=0)
    conv = _dot3(gh_ref[...], gl_ref[...], y)
    o_ref[0] = g_ref[0] * (conv + bias_ref[...] * u)


def hy_dense(hzc, circ, lp, cb=512):
    bsz, seq, _ = hzc.shape
    n = 2 * seq
    fwd, inv = _dense_dft_tables(seq)
    fh, fl = _np_split(fwd)
    gh, gl = _np_split(inv)
    hspec = pl.pallas_call(
        _hy_dense_spec_body,
        grid=(2 * GROUP_W // cb,),
        in_specs=[pl.BlockSpec((n, cb), lambda c: (0, c)),
                  pl.BlockSpec((2 * n, n), lambda c: (0, 0)),
                  pl.BlockSpec((2 * n, n), lambda c: (0, 0))],
        out_specs=pl.BlockSpec((2 * n, cb), lambda c: (0, c)),
        out_shape=jax.ShapeDtypeStruct((2 * n, 2 * GROUP_W), F32),
        compiler_params=_cp("arbitrary"),
        name="hy_dense_spec",
    )(circ, fh, fl)
    fh_in, fl_in = fh[:, :seq], fl[:, :seq]
    u, uc0 = hzc, 0
    ncb = GROUP_W // cb
    for o in range(2):
        gc0 = (1 + o) * ncb
        u = pl.pallas_call(
            functools.partial(_hy_dense_body, n=n),
            grid=(bsz, ncb),
            in_specs=[pl.BlockSpec((1, seq, cb), lambda b, c, uc0=uc0: (b, 0, uc0 + c)),
                      pl.BlockSpec((2 * n, cb), lambda b, c, o=o: (0, o * ncb + c)),
                      pl.BlockSpec((2 * n, seq), lambda b, c: (0, 0)),
                      pl.BlockSpec((2 * n, seq), lambda b, c: (0, 0)),
                      pl.BlockSpec((seq, 2 * n), lambda b, c: (0, 0)),
                      pl.BlockSpec((seq, 2 * n), lambda b, c: (0, 0)),
                      pl.BlockSpec((1, seq, cb), lambda b, c, gc0=gc0: (b, 0, gc0 + c)),
                      pl.BlockSpec((1, cb), lambda b, c: (0, c))],
            out_specs=pl.BlockSpec((1, seq, cb), lambda b, c: (b, 0, c)),
            out_shape=jax.ShapeDtypeStruct((bsz, seq, GROUP_W), F32),
            compiler_params=_cp("arbitrary", "arbitrary"),
            name="hy_dense_conv",
        )(u, hspec, fh_in, fl_in, gh, gl, hzc, lp["hy_bias"][o].reshape(1, GROUP_W))
        uc0 = 0
    return u


def hyena_mixer(p3, lp):
    bsz, seq, _ = p3.shape
    if 2 * seq == HY_N1 * HY_N2:
        return hyena_long(p3, lp), True
    hzc = hy_conv3(p3, lp)
    circ = hy_filter(lp, seq)
    return hy_dense(hzc, circ, lp), False


def _rms(y, g):
    return y * lax.rsqrt(jnp.mean(y * y, axis=-1, keepdims=True) + EPS) * g


def _mix_body(s5_ref, hf_ref, hb_ref, lg_ref, of_ref, ob_ref, gg_ref, hy_ref, gn_ref, mg_ref, o_ref, *, hy_t):
    w = GROUP_W
    y_hy = hy_ref[0].T if hy_t else hy_ref[...]
    o_ref[:, 0:w] = _rms(s5_ref[...], mg_ref[:, 0:w]).astype(o_ref.dtype)
    y_lru = (hf_ref[0] + hb_ref[0]) * jax.nn.gelu(lg_ref[...])
    o_ref[:, w:2 * w] = _rms(y_lru, mg_ref[:, w:2 * w]).astype(o_ref.dtype)
    o = of_ref[0] + ob_ref[0]
    gg = gg_ref[...]
    heads = []
    for h in range(GLA_HEADS):
        sl = slice(h * GLA_DV, (h + 1) * GLA_DV)
        heads.append(_rms(o[:, sl], gn_ref[...]) * (gg[:, sl] * jax.nn.sigmoid(gg[:, sl])))
    y_gla = jnp.concatenate(heads, axis=-1)
    o_ref[:, 2 * w:3 * w] = _rms(y_gla, mg_ref[:, 2 * w:3 * w]).astype(o_ref.dtype)
    o_ref[:, 3 * w:4 * w] = _rms(y_hy, mg_ref[:, 3 * w:4 * w]).astype(o_ref.dtype)


def mix_assemble(p2, y_s5, h_lru, o_gla, y_hy, hy_t, seq, lp, tm=256):
    m = p2.shape[0]
    w = GROUP_W
    row = lambda i: (i, 0)
    tpb = seq // tm
    if hy_t:
        hy_spec = pl.BlockSpec((1, w, tm), lambda i: (i // tpb, 0, i % tpb))
    else:
        hy_spec = pl.BlockSpec((tm, w), row)
        y_hy = y_hy.reshape(m, w)
    return pl.pallas_call(
        functools.partial(_mix_body, hy_t=hy_t),
        grid=(m // tm,),
        in_specs=[pl.BlockSpec((tm, w), row),
                  pl.BlockSpec((1, tm, w), lambda i: (0, i, 0)),
                  pl.BlockSpec((1, tm, w), lambda i: (1, i, 0)),
                  pl.BlockSpec((tm, w), lambda i: (i, 2)),
                  pl.BlockSpec((1, tm, w), lambda i: (0, i, 0)),
                  pl.BlockSpec((1, tm, w), lambda i: (1, i, 0)),
                  pl.BlockSpec((tm, w), lambda i: (i, 5)),
                  hy_spec,
                  pl.BlockSpec((1, GLA_DV), lambda i: (0, 0)),
                  pl.BlockSpec((1, 4 * w), lambda i: (0, 0))],
        out_specs=pl.BlockSpec((tm, 4 * w), row),
        out_shape=jax.ShapeDtypeStruct((m, 4 * w), BF16),
        compiler_params=_cp("arbitrary"),
        name="mix_assemble",
    )(y_s5.reshape(m, w), h_lru.reshape(2, m, w), h_lru.reshape(2, m, w), p2,
      o_gla.reshape(2, m, w), o_gla.reshape(2, m, w), p2, y_hy,
      lp["gla_norm_g"].reshape(1, GLA_DV), lp["mix_norm_g"].reshape(1, 4 * w))


FFN_LAG = 2
FFN_SUB_ROWS = 256


def _ffn1_body(h_ref, wg_ref, wu_ref, cw_ref, cb_ref, o_ref, g_scr, u_scr, wgb_scr, wub_scr, *, tm, gw, tps, nt):
    s = pl.program_id(1)

    @pl.when(s == 0)
    def _():
        g_scr[...] = jnp.zeros_like(g_scr)
        u_scr[...] = jnp.zeros_like(u_scr)
        wgb_scr[...] = wg_ref[...].astype(BF16)
        wub_scr[...] = wu_ref[...].astype(BF16)

    t = s - FFN_LAG
    cur = s % 3
    mid = (s + 1) % 3
    nxt = (s + 2) % 3
    g_cur, g_mid, g_nxt = g_scr.at[cur], g_scr.at[mid], g_scr.at[nxt]
    u_cur, u_mid = u_scr.at[cur], u_scr.at[mid]

    rb = min(FFN_SUB_ROWS, tm)
    nsub = tm // rb
    gpt = tm // gw
    tn = o_ref.shape[1]
    col = lax.broadcasted_iota(jnp.int32, (gw, 1), 0)
    first_col = col == 0
    last_col = col == gw - 1

    def grid_row(rho, ls):
        if rho < 0:
            return jnp.where(t % tps == 0, 0.0, g_cur[tm - gw:tm, ls])
        if rho >= gpt:
            return jnp.where(t % tps == tps - 1, 0.0, g_nxt[0:gw, ls])
        return g_mid[rho * gw:(rho + 1) * gw, ls]

    never = s < 0
    chain = [None]

    def conv_finish(r):
        for rho in range(r * rb // gw, (r + 1) * rb // gw):
            for lh in range(tn // 128):
                ls = slice(lh * 128, (lh + 1) * 128)
                acc = jnp.broadcast_to(cb_ref[:, ls], (gw, 128))
                if chain[0] is not None:
                    acc = jnp.where(never, chain[0], acc)
                for dr in range(3):
                    src = grid_row(rho + dr - 1, ls)
                    left = jnp.where(first_col, 0.0, pltpu.roll(src, 1, 0))
                    right = jnp.where(last_col, 0.0, pltpu.roll(src, gw - 1, 0))
                    acc = (acc + cw_ref[3 * dr:3 * dr + 1, ls] * left + cw_ref[3 * dr + 1:3 * dr + 2, ls] * src
                           + cw_ref[3 * dr + 2:3 * dr + 3, ls] * right)
                rows = slice(rho * gw, (rho + 1) * gw)
                o_ref[rows, ls] = (acc * jax.nn.sigmoid(acc) * u_mid[rows, ls]).astype(o_ref.dtype)
                chain[0] = acc

    @pl.when(s < nt)
    def _():
        chain[0] = None
        for r in range(nsub):
            conv_finish(r)
            h = h_ref[r * rb:(r + 1) * rb, :]
            g_cur[r * rb:(r + 1) * rb] = _dot(h, wgb_scr[...])
            u_cur[r * rb:(r + 1) * rb] = _dot(h, wub_scr[...])

    @pl.when(s >= nt)
    def _():
        chain[0] = None
        for r in range(nsub):
            conv_finish(r)


def ffn1(h2, wg, wu, cw, cb, layer, *, seq, gw, tm, tn=256):
    m, d = h2.shape
    ff = wg.shape[-1]
    tps = seq // tm
    nt = m // tm
    return pl.pallas_call(
        functools.partial(_ffn1_body, tm=tm, gw=gw, tps=tps, nt=nt),
        grid=(ff // tn, nt + FFN_LAG),
        in_specs=[pl.BlockSpec((tm, d), lambda j, s: (jnp.minimum(s, nt - 1), 0)),
                  pl.BlockSpec((None, d, tn), lambda j, s: (layer, 0, j)),
                  pl.BlockSpec((None, d, tn), lambda j, s: (layer, 0, j)),
                  pl.BlockSpec((None, 9, tn), lambda j, s: (layer, 0, j)),
                  pl.BlockSpec((None, 1, tn), lambda j, s: (layer, 0, j))],
        out_specs=pl.BlockSpec((tm, tn), lambda j, s: (jnp.maximum(s - FFN_LAG, 0), j)),
        out_shape=jax.ShapeDtypeStruct((m, ff), BF16),
        scratch_shapes=[pltpu.VMEM((3, tm, tn), F32), pltpu.VMEM((3, tm, tn), F32),
                        pltpu.VMEM((d, tn), BF16), pltpu.VMEM((d, tn), BF16)],
        compiler_params=_cp("arbitrary", "arbitrary"),
        name="ffn_gate_up",
    )(h2, wg, wu, cw, cb)


def _token_mixers(p2, bsz, seq, states, lp, wglu_bf, layer, need_output):
    p3 = p2.reshape(bsz, seq, N_COL_PAD)
    y_s5, st_s5 = s5_mixer(p3, states[0], lp, wglu_bf, layer, need_output)
    h_lru, st_lru = lru_mixer(p3, states[1], lp)
    o_gla, st_gla = gla_mixer(p3, states[2], lp)
    new_states = (st_s5, st_lru, st_gla)
    if not need_output:
        return None, new_states
    y_hy, hy_t = hyena_mixer(p3, lp)
    return mix_assemble(p2, y_s5, h_lru, o_gla, y_hy, hy_t, seq, lp), new_states


def kernel(x, c, ctx, c_ctx, w_ada, b_ada, norm_mix_g, norm_mlp_g, w_in, s5_lam_re, s5_lam_im, s5_log_step, s5_b_re, s5_b_im, s5_c_re, s5_c_im, s5_d, s5_w_glu, s5_b_glu, lru_conv_w, lru_conv_b, lru_w_a, lru_b_a, lru_w_x, lru_b_x, lru_lam, gla_w_alpha, gla_b_alpha, gla_norm_g, hy_conv_w, hy_conv_b, hy_w1, hy_b1, hy_w2, hy_b2, hy_w3, hy_freq, hy_bias, mix_norm_g, w_out, mlp_w_gate, mlp_w_up, mlp_conv_w, mlp_conv_b, mlp_w_down, final_norm_g):
    bsz, seq, d = x.shape
    clen = ctx.shape[1]
    depth = w_ada.shape[0]
    grid_w = 64
    params = dict(
        s5_lam_re=s5_lam_re, s5_lam_im=s5_lam_im, s5_log_step=s5_log_step, s5_b_re=s5_b_re, s5_b_im=s5_b_im,
        s5_c_re=s5_c_re, s5_c_im=s5_c_im, s5_d=s5_d, s5_b_glu=s5_b_glu,
        lru_conv_w=lru_conv_w, lru_conv_b=lru_conv_b, lru_w_a=lru_w_a, lru_b_a=lru_b_a, lru_w_x=lru_w_x,
        lru_b_x=lru_b_x, lru_lam=lru_lam, gla_w_alpha=gla_w_alpha, gla_b_alpha=gla_b_alpha, gla_norm_g=gla_norm_g,
        hy_conv_w=hy_conv_w, hy_conv_b=hy_conv_b, hy_w1=hy_w1, hy_b1=hy_b1, hy_w2=hy_w2, hy_b2=hy_b2, hy_w3=hy_w3,
        hy_freq=hy_freq, hy_bias=hy_bias, mix_norm_g=mix_norm_g)

    w_in_bf = jnp.concatenate(
        [w_in[..., 0:6144], w_in[..., 6176:9248], w_in[..., 6144:6176],
         jnp.zeros((depth, d, N_COL_PAD - 9248), w_in.dtype)], axis=-1).astype(BF16)
    w_out_bf = w_out.astype(BF16)
    wd_bf = mlp_w_down.astype(BF16)
    wglu_bf = s5_w_glu.astype(BF16)
    conv_w9 = mlp_conv_w.reshape(depth, 9, D_FF)
    conv_b = mlp_conv_b.reshape(depth, 1, D_FF)

    cvec = jnp.zeros((8, d), F32).at[0:bsz].set(c).at[bsz].set(c_ctx)
    mod = ada_mod(cvec, w_ada, b_ada)

    x2 = x.reshape(bsz * seq, d)
    c2 = ctx.reshape(bsz * clen, d)
    zero_states = (jnp.zeros((bsz, 2, S5_JB, 1, 1024), F32),
                   jnp.zeros((bsz, 2, 1, GROUP_W), F32),
                   jnp.zeros((bsz, 2, GLA_HEADS, GLA_DK, GLA_DV), F32))

    for l in range(depth):
        last = l == depth - 1
        lp = {k: v[l] for k, v in params.items()}
        mx = mod[l, 0:bsz].reshape(bsz, 1, 6, d)
        mc = mod[l, bsz:bsz + 1].reshape(1, 1, 6, d)
        sh1, sc1, g1, sh2, sc2, g2 = (mx[:, :, i] for i in range(6))
        csh1, csc1, cg1, csh2, csc2, cg2 = (mc[:, :, i] for i in range(6))

        hc = normmod(c2, norm_mix_g[l], csh1, csc1, bsz * clen, BF16)
        pc = matmul(hc, w_in_bf, l, tm=bsz * clen, tn=512, name="in_proj_ctx")
        yc, ctx_states = _token_mixers(pc, bsz, clen, zero_states, lp, wglu_bf, l, need_output=not last)

        hx = normmod(x2, norm_mix_g[l], sh1, sc1, seq, BF16)
        px = matmul(hx, w_in_bf, l, tm=1024, tn=512, name="in_proj")
        yx, _ = _token_mixers(px, bsz, seq, ctx_states, lp, wglu_bf, l, need_output=True)
        x2 = matmul(yx, w_out_bf, l, tm=1024, tn=512, res=x2, gate=g1, rows_per_gate=seq, name="out_proj")
        h2 = normmod(x2, norm_mlp_g[l], sh2, sc2, seq, BF16)
        act = ffn1(h2, mlp_w_gate, mlp_w_up, conv_w9, conv_b, l, seq=seq, gw=grid_w, tm=1024)
        x2 = matmul(act, wd_bf, l, tm=512, tn=512, res=x2, gate=g2, rows_per_gate=seq, name="down_proj")

        if not last:
            c2 = matmul(yc, w_out_bf, l, tm=bsz * clen, tn=512, res=c2, gate=cg1, rows_per_gate=bsz * clen,
                        name="out_proj_ctx")
            hc2 = normmod(c2, norm_mlp_g[l], csh2, csc2, bsz * clen, BF16)
            actc = ffn1(hc2, mlp_w_gate, mlp_w_up, conv_w9, conv_b, l, seq=clen, gw=clen, tm=clen)
            c2 = matmul(actc, wd_bf, l, tm=bsz * clen, tn=512, res=c2, gate=cg2, rows_per_gate=bsz * clen,
                        name="down_proj_ctx")

    zeros = jnp.zeros((1, 1, d), F32)
    out = normmod(x2, final_norm_g, zeros, zeros, bsz * seq, F32)
    return out.reshape(bsz, seq, d)
```

```python
import functools
import math

import numpy as np
import jax
import jax.numpy as jnp
from jax import lax
from jax.experimental import pallas as pl
from jax.experimental.pallas import tpu as pltpu

F32 = jnp.float32
BF16 = jnp.bfloat16
HI = lax.Precision.HIGHEST

EPS = 1e-6
GROUP_W = 1024
N_COL_PAD = 9728
S5_GROUPS = 64
S5_CH = 16
S5_STATE = 64
S5_SEG = 8
S5_JB = 8
LRU_HEADS = 16
LRU_C = 8.0
GLA_HEADS = 4
GLA_DK = 128
GLA_DV = 256
GLA_CHUNK = 64
GLA_GATE_NORM = 16.0
HY_BANDS = 16
HY_FFN = 64
D_FF = 11008
V7X_VMEM_LIMIT = 56 * 1024 * 1024


def _cp(*sem):
    return pltpu.CompilerParams(dimension_semantics=sem, vmem_limit_bytes=V7X_VMEM_LIMIT)


def _dot(a, b):
    return jnp.dot(a, b, preferred_element_type=F32)


def _split(x):
    hi = x.astype(BF16)
    lo = (x - hi.astype(F32)).astype(BF16)
    return hi, lo


def _dot3(fh, fl, x):
    xh, xl = _split(x)
    return _dot(fh, xh) + _dot(fh, xl) + _dot(fl, xh)


def _np_split(a):
    bf = jnp.dtype(BF16)
    hi = np.asarray(a, np.float64).astype(bf)
    lo = (np.asarray(a, np.float64) - hi.astype(np.float64)).astype(bf)
    return jnp.asarray(hi), jnp.asarray(lo)


def _ada_body(c_ref, w_ref, b_ref, o_ref):
    c = c_ref[...]
    a = (c * jax.nn.sigmoid(c)).astype(BF16)
    o_ref[...] = _dot(a, w_ref[...].astype(BF16)) + b_ref[...]


def ada_mod(cvec, w_ada, b_ada, tn=1024):
    depth, d, n = w_ada.shape
    return pl.pallas_call(
        _ada_body,
        grid=(depth, n // tn),
        in_specs=[pl.BlockSpec((8, d), lambda l, j: (0, 0)),
                  pl.BlockSpec((None, d, tn), lambda l, j: (l, 0, j)),
                  pl.BlockSpec((None, 1, tn), lambda l, j: (l, 0, j))],
        out_specs=pl.BlockSpec((None, 8, tn), lambda l, j: (l, 0, j)),
        out_shape=jax.ShapeDtypeStruct((depth, 8, n), F32),
        compiler_params=_cp("arbitrary", "arbitrary"),
        name="ada_mod",
    )(cvec, w_ada, b_ada.reshape(depth, 1, n))


def _normmod_body(x_ref, g_ref, sh_ref, sc_ref, o_ref):
    x = x_ref[...]
    y = x * lax.rsqrt(jnp.mean(x * x, axis=-1, keepdims=True) + EPS) * g_ref[...]
    o_ref[...] = (y * (1.0 + sc_ref[0]) + sh_ref[0]).astype(o_ref.dtype)


def normmod(x2d, g, sh, sc, rows_per_mod, out_dtype, tm=256):
    m, d = x2d.shape
    tpm = rows_per_mod // tm
    return pl.pallas_call(
        _normmod_body,
        grid=(m // tm,),
        in_specs=[pl.BlockSpec((tm, d), lambda i: (i, 0)),
                  pl.BlockSpec((1, d), lambda i: (0, 0)),
                  pl.BlockSpec((1, 1, d), lambda i: (i // tpm, 0, 0)),
                  pl.BlockSpec((1, 1, d), lambda i: (i // tpm, 0, 0))],
        out_specs=pl.BlockSpec((tm, d), lambda i: (i, 0)),
        out_shape=jax.ShapeDtypeStruct((m, d), out_dtype),
        compiler_params=_cp("arbitrary"),
        name="normmod",
    )(x2d, g.reshape(1, d), sh, sc)


def _mm_body(*refs, nk, has_res):
    if has_res:
        a_ref, w_ref, res_ref, gate_ref, o_ref = refs[:5]
        scr = refs[5:]
    else:
        a_ref, w_ref, o_ref = refs[:3]
        scr = refs[3:]

    def epilogue(acc):
        if has_res:
            o_ref[...] = res_ref[...] + gate_ref[0] * acc
        else:
            o_ref[...] = acc.astype(o_ref.dtype)

    if nk == 1:
        epilogue(_dot(a_ref[...], w_ref[...]))
    else:
        acc_ref = scr[0]
        k = pl.program_id(2)

        @pl.when(k == 0)
        def _():
            acc_ref[...] = jnp.zeros_like(acc_ref)

        acc_ref[...] += _dot(a_ref[...], w_ref[...])

        @pl.when(k == nk - 1)
        def _():
            epilogue(acc_ref[...])


def matmul(a, w, layer, *, tm, tn, tk=None, res=None, gate=None, rows_per_gate=None, name="matmul"):
    m, kdim = a.shape
    n = w.shape[-1]
    tk = kdim if tk is None else tk
    nk = kdim // tk
    has_res = res is not None
    in_specs = [pl.BlockSpec((tm, tk), lambda i, j, k: (i, k)),
                pl.BlockSpec((None, tk, tn), lambda i, j, k: (layer, k, j))]
    args = [a, w]
    if has_res:
        tpg = rows_per_gate // tm
        in_specs += [pl.BlockSpec((tm, tn), lambda i, j, k: (i, j)),
                     pl.BlockSpec((1, 1, tn), lambda i, j, k: (i // tpg, 0, j))]
        args += [res, gate]
    return pl.pallas_call(
        functools.partial(_mm_body, nk=nk, has_res=has_res),
        grid=(m // tm, n // tn, nk),
        in_specs=in_specs,
        out_specs=pl.BlockSpec((tm, tn), lambda i, j, k: (i, j)),
        out_shape=jax.ShapeDtypeStruct((m, n), F32),
        scratch_shapes=[pltpu.VMEM((tm, tn), F32)] if nk > 1 else [],
        compiler_params=_cp("arbitrary", "arbitrary", "arbitrary"),
        name=name,
    )(*args)


def _s5_params(lp, lseg):
    lam_re, lam_im = lp["s5_lam_re"], lp["s5_lam_im"]
    step = jnp.exp(lp["s5_log_step"])[:, :, None]
    mag = jnp.exp(lam_re * step)
    ab_re = mag * jnp.cos(lam_im * step)
    ab_im = mag * jnp.sin(lam_im * step)
    den = lam_re * lam_re + lam_im * lam_im
    co_re = ((ab_re - 1.0) * lam_re + ab_im * lam_im) / den
    co_im = (ab_im * lam_re - (ab_re - 1.0) * lam_im) / den
    b_re, b_im = lp["s5_b_re"], lp["s5_b_im"]
    bb_re = co_re[..., None] * b_re - co_im[..., None] * b_im
    bb_im = co_re[..., None] * b_im + co_im[..., None] * b_re
    eye = jnp.eye(8, dtype=F32)

    def in_blocks(bb):
        t = bb.reshape(2, S5_JB, 8, S5_STATE, S5_CH)
        t = jnp.einsum("djgpc,gh->djgchp", t, eye)
        return t.reshape(2, S5_JB, 8 * S5_CH, 8 * S5_STATE)

    def out_blocks(cc):
        t = cc.reshape(2, S5_JB, 8, S5_CH, S5_STATE)
        t = jnp.einsum("djgcp,gh->djgphc", t, eye)
        return t.reshape(2, S5_JB, 8 * S5_STATE, 8 * S5_CH)

    wb = jnp.concatenate([in_blocks(bb_re), in_blocks(bb_im)], axis=-1).astype(BF16)
    wc = jnp.concatenate([out_blocks(lp["s5_c_re"]), -out_blocks(lp["s5_c_im"])], axis=-2).astype(BF16)

    def lanes(t):
        return t.reshape(2, S5_JB, 1, 8 * S5_STATE)

    a = jnp.concatenate([lanes(ab_re), lanes(ab_im)], axis=-1)
    pr, pi = ab_re, ab_im
    for _ in range(int(round(math.log2(lseg)))):
        pr, pi = pr * pr - pi * pi, 2.0 * pr * pi
    al = jnp.concatenate([lanes(pr), lanes(pi)], axis=-1)
    return wb, wc, a, al


S5_JP = 2


def _s5_chains(bsz):
    return [(b, jj) for b in range(bsz) for jj in range(S5_JP)]


def _s5_project_in(u_ref, wb_ref, a_ref, bu_scr, ab_scr, bsz):
    for ch, (b, jj) in enumerate(_s5_chains(bsz)):
        bu_scr[ch] = _dot(u_ref[b, :, jj * 128:(jj + 1) * 128].astype(BF16), wb_ref[0, jj])
    for jj in range(S5_JP):
        ab_scr[jj] = jnp.broadcast_to(a_ref[0, jj], (S5_SEG, 1024))


def _s5_scan_tile(d, t_steps, bu_scr, ab_scr, h_scr, store, bsz):
    chains = _s5_chains(bsz)

    def step(s, carry):
        row = jnp.where(d == 0, s, t_steps - 1 - s)
        off = pl.multiple_of(row * S5_SEG, S5_SEG)
        out = []
        for ch, (_, jj) in enumerate(chains):
            hr, hi = carry[2 * ch], carry[2 * ch + 1]
            ar = ab_scr[jj, :, 0:512]
            ai = ab_scr[jj, :, 512:1024]
            nr = ar * hr - ai * hi + bu_scr[ch, pl.ds(off, S5_SEG), 0:512]
            ni = ar * hi + ai * hr + bu_scr[ch, pl.ds(off, S5_SEG), 512:1024]
            if store:
                bu_scr[ch, pl.ds(off, S5_SEG), 0:512] = nr
                bu_scr[ch, pl.ds(off, S5_SEG), 512:1024] = ni
            out += [nr, ni]
        return tuple(out)

    init = tuple(h_scr[ch, :, lo:lo + 512] for ch in range(len(chains)) for lo in (0, 512))
    fin = lax.fori_loop(0, t_steps, step, init, unroll=2)
    for ch in range(len(chains)):
        h_scr[ch, :, 0:512] = fin[2 * ch]
        h_scr[ch, :, 512:1024] = fin[2 * ch + 1]


def _s5_p1_body(u_ref, wb_ref, a_ref, al_ref, h0_ref, hinit_ref, fin_ref, bu_scr, ab_scr, h_scr, *, t_steps, nt, bsz):
    d = pl.program_id(0)
    i = pl.program_id(2)

    @pl.when(i == 0)
    def _():
        h_scr[...] = jnp.zeros_like(h_scr)

    _s5_project_in(u_ref, wb_ref, a_ref, bu_scr, ab_scr, bsz)
    _s5_scan_tile(d, t_steps, bu_scr, ab_scr, h_scr, False, bsz)

    @pl.when(i == nt - 1)
    def _():
        for ch, (b, jj) in enumerate(_s5_chains(bsz)):
            alr = al_ref[0, jj, :, 0:512]
            ali = al_ref[0, jj, :, 512:1024]
            cr = h0_ref[b, 0, jj, :, 0:512]
            ci = h0_ref[b, 0, jj, :, 512:1024]
            for s in range(S5_SEG):
                k = jnp.where(d == 0, s, S5_SEG - 1 - s)
                hinit_ref[b, 0, jj, pl.ds(k, 1), 0:512] = cr
                hinit_ref[b, 0, jj, pl.ds(k, 1), 512:1024] = ci
                fr = h_scr[ch, pl.ds(k, 1), 0:512]
                fi = h_scr[ch, pl.ds(k, 1), 512:1024]
                cr, ci = alr * cr - ali * ci + fr, alr * ci + ali * cr + fi
            fin_ref[b, 0, jj, :, 0:512] = cr
            fin_ref[b, 0, jj, :, 512:1024] = ci


def _s5_p2_body(u_ref, wb_ref, wc_ref, a_ref, hinit_ref, y_ref, bu_scr, ab_scr, h_scr, *, t_steps, bsz):
    d = pl.program_id(0)
    i = pl.program_id(2)

    @pl.when(i == 0)
    def _():
        for ch, (b, jj) in enumerate(_s5_chains(bsz)):
            h_scr[ch] = hinit_ref[b, 0, jj]

    _s5_project_in(u_ref, wb_ref, a_ref, bu_scr, ab_scr, bsz)
    _s5_scan_tile(d, t_steps, bu_scr, ab_scr, h_scr, True, bsz)
    for ch, (b, jj) in enumerate(_s5_chains(bsz)):
        y_ref[0, b, :, jj * 128:(jj + 1) * 128] = _dot(bu_scr[ch].astype(BF16), wc_ref[0, jj])


def _s5_fin_body(u_ref, yf_ref, yb_ref, d_ref, w_ref, b_ref, o_ref):
    y = u_ref[0] * d_ref[...] + yf_ref[0, 0] + yb_ref[0, 0]
    yg = jax.nn.gelu(y)
    o_ref[0] = yg * jax.nn.sigmoid(_dot(yg.astype(BF16), w_ref[...]) + b_ref[...])


def s5_mixer(p3, h0, lp, wglu_bf, layer, need_output):
    bsz, seq, _ = p3.shape
    lseg = seq // S5_SEG
    t_steps = min(64, lseg)
    nt = lseg // t_steps
    rows = t_steps * S5_SEG
    wb, wc, a, al = _s5_params(lp, lseg)
    u_perm = p3[:, :, 0:GROUP_W].reshape(bsz, S5_SEG, lseg, GROUP_W).transpose(0, 2, 1, 3).reshape(bsz, seq, GROUP_W)

    def tile(d, i):
        return jnp.where(d == 0, i, nt - 1 - i)

    nchain = bsz * S5_JP
    grid = (2, S5_JB // S5_JP, nt)
    u_spec = pl.BlockSpec((bsz, rows, 128 * S5_JP), lambda d, j, i: (0, tile(d, i), j))
    wb_spec = pl.BlockSpec((1, S5_JP, 128, 1024), lambda d, j, i: (d, j, 0, 0))
    a_spec = pl.BlockSpec((1, S5_JP, 1, 1024), lambda d, j, i: (d, j, 0, 0))
    st1_spec = pl.BlockSpec((bsz, 1, S5_JP, 1, 1024), lambda d, j, i: (0, d, j, 0, 0))
    st8_spec = pl.BlockSpec((bsz, 1, S5_JP, S5_SEG, 1024), lambda d, j, i: (0, d, j, 0, 0))
    scratch = [pltpu.VMEM((nchain, rows, 1024), F32), pltpu.VMEM((S5_JP, S5_SEG, 1024), F32),
               pltpu.VMEM((nchain, S5_SEG, 1024), F32)]
    hinit, fin = pl.pallas_call(
        functools.partial(_s5_p1_body, t_steps=t_steps, nt=nt, bsz=bsz),
        grid=grid,
        in_specs=[u_spec, wb_spec, a_spec, a_spec, st1_spec],
        out_specs=[st8_spec, st1_spec],
        out_shape=[jax.ShapeDtypeStruct((bsz, 2, S5_JB, S5_SEG, 1024), F32),
                   jax.ShapeDtypeStruct((bsz, 2, S5_JB, 1, 1024), F32)],
        scratch_shapes=scratch,
        compiler_params=_cp("arbitrary", "arbitrary", "arbitrary"),
        name="s5_pass1",
    )(u_perm, wb, a, al, h0)
    if not need_output:
        return None, fin
    y = pl.pallas_call(
        functools.partial(_s5_p2_body, t_steps=t_steps, bsz=bsz),
        grid=grid,
        in_specs=[u_spec, wb_spec,
                  pl.BlockSpec((1, S5_JP, 1024, 128), lambda d, j, i: (d, j, 0, 0)),
                  a_spec, st8_spec],
        out_specs=pl.BlockSpec((1, bsz, rows, 128 * S5_JP), lambda d, j, i: (d, 0, tile(d, i), j)),
        out_shape=jax.ShapeDtypeStruct((2, bsz, seq, GROUP_W), F32),
        scratch_shapes=scratch,
        compiler_params=_cp("arbitrary", "arbitrary", "arbitrary"),
        name="s5_pass2",
    )(u_perm, wb, wc, a, hinit)
    tr = min(512, seq)
    out = pl.pallas_call(
        _s5_fin_body,
        grid=(bsz, seq // tr),
        in_specs=[pl.BlockSpec((1, tr, GROUP_W), lambda b, i: (b, i, 0)),
                  pl.BlockSpec((1, 1, tr, GROUP_W), lambda b, i: (0, b, i, 0)),
                  pl.BlockSpec((1, 1, tr, GROUP_W), lambda b, i: (1, b, i, 0)),
                  pl.BlockSpec((1, GROUP_W), lambda b, i: (0, 0)),
                  pl.BlockSpec((None, GROUP_W, GROUP_W), lambda b, i: (layer, 0, 0)),
                  pl.BlockSpec((1, GROUP_W), lambda b, i: (0, 0))],
        out_specs=pl.BlockSpec((1, tr, GROUP_W), lambda b, i: (b, i, 0)),
        out_shape=jax.ShapeDtypeStruct((bsz, seq, GROUP_W), F32),
        compiler_params=_cp("arbitrary", "arbitrary"),
        name="s5_finalize",
    )(u_perm, y, y, lp["s5_d"].reshape(1, GROUP_W), wglu_bf, lp["s5_b_glu"].reshape(1, GROUP_W))
    out = out.reshape(bsz, lseg, S5_SEG, GROUP_W).transpose(0, 2, 1, 3).reshape(bsz, seq, GROUP_W)
    return out, fin


def _lru_body(xp_ref, xm_ref, xn_ref, cw_ref, cb_ref, wg_ref, bg_ref, sp_ref, h0_ref, h_ref, fin_ref,
              a_scr, b_scr, hc_scr, *, tile_rows, nt, seq):
    d = pl.program_id(0)
    i = pl.program_id(1)
    ti = jnp.where(d == 0, i, nt - 1 - i)
    bsz = xm_ref.shape[0]

    @pl.when(i == 0)
    def _():
        hc_scr[...] = h0_ref[:, 0]

    n = tile_rows + 16
    rowid = lax.broadcasted_iota(jnp.int32, (n, 1), 0) + (ti * tile_rows - 8)
    valid = (rowid >= 0) & (rowid < seq)
    cw = cw_ref[...]
    for b in range(bsz):
        xe = jnp.where(valid, jnp.concatenate([xp_ref[b], xm_ref[b], xn_ref[b]], axis=0), 0.0)
        xc = (cb_ref[...]
              + cw[0:1] * pltpu.roll(xe, 2, 0)[8:8 + tile_rows]
              + cw[1:2] * pltpu.roll(xe, 1, 0)[8:8 + tile_rows]
              + cw[2:3] * xe[8:8 + tile_rows]
              + cw[3:4] * pltpu.roll(xe, n - 1, 0)[8:8 + tile_rows])
        for cb in range(4):
            lo, hi = cb * 256, (cb + 1) * 256
            xcb = xc[:, lo:hi]
            pre = _dot(xcb.astype(BF16), wg_ref[0, cb])
            r = jax.nn.sigmoid(pre[:, 0:256] + bg_ref[0, :, lo:hi])
            ig = jax.nn.sigmoid(pre[:, 256:512] + bg_ref[0, :, GROUP_W + lo:GROUP_W + hi])
            log_a = -LRU_C * sp_ref[0, :, lo:hi] * r
            a = jnp.exp(log_a)
            a_scr[b, :, lo:hi] = a
            b_scr[b, :, lo:hi] = jnp.sqrt(jnp.tanh(-log_a) * (a * a + 1.0)) * (ig * xcb)

    def step(s, hs):
        t = jnp.where(d == 0, s, tile_rows - 1 - s)
        out = []
        for b in range(bsz):
            h = a_scr[b, pl.ds(t, 1), :] * hs[b] + b_scr[b, pl.ds(t, 1), :]
            b_scr[b, pl.ds(t, 1), :] = h
            out.append(h)
        return tuple(out)

    hs = lax.fori_loop(0, tile_rows, step, tuple(hc_scr[b] for b in range(bsz)), unroll=8)
    for b in range(bsz):
        hc_scr[b] = hs[b]
        fin_ref[b, 0] = hs[b]
    h_ref[0] = b_scr[...]


def _lru_params(lp):
    def blockdiag(w):
        t = w.reshape(2, 4, 4, 64, 64)
        t = jnp.einsum("dcgij,gh->dcgihj", t, jnp.eye(4, dtype=F32))
        return t.reshape(2, 4, 256, 256)

    wg = jnp.concatenate([blockdiag(lp["lru_w_a"]), blockdiag(lp["lru_w_x"])], axis=-1).astype(BF16)
    bg = jnp.concatenate([lp["lru_b_a"], lp["lru_b_x"]], axis=-1).reshape(2, 1, 2 * GROUP_W)
    sp = jax.nn.softplus(-lp["lru_lam"]).reshape(2, 1, GROUP_W)
    return wg, bg, sp


def lru_mixer(p3, h0, lp):
    bsz, seq, _ = p3.shape
    tr = min(512, seq)
    nt = seq // tr
    wg, bg, sp = _lru_params(lp)
    nb8 = seq // 8

    def tile(d, i):
        return jnp.where(d == 0, i, nt - 1 - i)

    return pl.pallas_call(
        functools.partial(_lru_body, tile_rows=tr, nt=nt, seq=seq),
        grid=(2, nt),
        in_specs=[pl.BlockSpec((bsz, 8, GROUP_W), lambda d, i: (0, jnp.maximum(tile(d, i) * (tr // 8) - 1, 0), 1)),
                  pl.BlockSpec((bsz, tr, GROUP_W), lambda d, i: (0, tile(d, i), 1)),
                  pl.BlockSpec((bsz, 8, GROUP_W), lambda d, i: (0, jnp.minimum((tile(d, i) + 1) * (tr // 8), nb8 - 1), 1)),
                  pl.BlockSpec((4, GROUP_W), lambda d, i: (0, 0)),
                  pl.BlockSpec((1, GROUP_W), lambda d, i: (0, 0)),
                  pl.BlockSpec((1, 4, 256, 512), lambda d, i: (d, 0, 0, 0)),
                  pl.BlockSpec((1, 1, 2 * GROUP_W), lambda d, i: (d, 0, 0)),
                  pl.BlockSpec((1, 1, GROUP_W), lambda d, i: (d, 0, 0)),
                  pl.BlockSpec((bsz, 1, 1, GROUP_W), lambda d, i: (0, d, 0, 0))],
        out_specs=[pl.BlockSpec((1, bsz, tr, GROUP_W), lambda d, i: (d, 0, tile(d, i), 0)),
                   pl.BlockSpec((bsz, 1, 1, GROUP_W), lambda d, i: (0, d, 0, 0))],
        out_shape=[jax.ShapeDtypeStruct((2, bsz, seq, GROUP_W), F32),
                   jax.ShapeDtypeStruct((bsz, 2, 1, GROUP_W), F32)],
        scratch_shapes=[pltpu.VMEM((bsz, tr, GROUP_W), F32), pltpu.VMEM((bsz, tr, GROUP_W), F32),
                        pltpu.VMEM((bsz, 1, GROUP_W), F32)],
        compiler_params=_cp("arbitrary", "arbitrary"),
        name="lru_scan",
    )(p3, p3, p3, lp["lru_conv_w"], lp["lru_conv_b"].reshape(1, GROUP_W), wg, bg, sp, h0)


def _log_sigmoid(z):
    return jnp.minimum(z, 0.0) - jnp.log1p(jnp.exp(-jnp.abs(z)))


def _gla_body(qf_ref, kf_ref, vf_ref, lf_ref, qb_ref, kb_ref, vb_ref, lb_ref, wa_ref, ba_ref, tri_ref, s0_ref,
              of_ref, ob_ref, sfin_ref, s_scr, *, nch):
    i = pl.program_id(1)
    c = GLA_CHUNK

    @pl.when(i == 0)
    def _():
        s_scr[...] = s0_ref[0]

    def one_chunk(d, r0, q_ref, k_ref, v_ref, lr_ref, o_ref):
        tri = tri_ref[d]
        z = _dot(lr_ref[0, pl.ds(r0, c), :].astype(BF16), wa_ref[d]) + ba_ref[d]
        la = _log_sigmoid(z) / GLA_GATE_NORM
        tri_bf = tri.astype(BF16)
        la_hi = la.astype(BF16)
        la_r = la - la_hi.astype(F32)
        la_mid = la_r.astype(BF16)
        la_lo = (la_r - la_mid.astype(F32)).astype(BF16)
        b_all = _dot(tri_bf, la_hi) + _dot(tri_bf, la_mid) + _dot(tri_bf, la_lo)
        q_all = q_ref[0, pl.ds(r0, c), :] * (GLA_DK ** -0.5)
        k_all = k_ref[0, pl.ds(r0, c), :]
        v_all = v_ref[0, pl.ds(r0, c), :].astype(BF16)
        mid = c // 2 - 1 if d == 0 else c // 2
        last = c - 1 if d == 0 else 0
        for h in range(GLA_HEADS):
            ks = slice(h * GLA_DK, (h + 1) * GLA_DK)
            vs = slice(h * GLA_DV, (h + 1) * GLA_DV)
            q, k, v, b = q_all[:, ks], k_all[:, ks], v_all[:, vs], b_all[:, ks]
            b_mid = b[mid:mid + 1]
            qd = (q * jnp.exp(b - b_mid)).astype(BF16)
            kd = (k * jnp.exp(b_mid - b)).astype(BF16)
            sc = lax.dot_general(qd, kd, (((1,), (1,)), ((), ())), preferred_element_type=F32) * tri
            intra = _dot(sc.astype(BF16), v)
            st = s_scr[d, h]
            inter = _dot((q * jnp.exp(b)).astype(BF16), st.astype(BF16))
            o_ref[0, 0, pl.ds(r0, c), vs] = intra + inter
            kt = k.T
            bt = b.T
            bt_last = bt[:, last:last + 1]
            k2t = (kt * jnp.exp(bt_last - bt)).astype(BF16)
            s_scr[d, h] = jnp.exp(bt_last) * st + _dot(k2t, v)

    def chunk(s, carry):
        one_chunk(0, pl.multiple_of(s * c, c), qf_ref, kf_ref, vf_ref, lf_ref, of_ref)
        one_chunk(1, pl.multiple_of((nch - 1 - s) * c, c), qb_ref, kb_ref, vb_ref, lb_ref, ob_ref)
        return carry

    lax.fori_loop(0, nch, chunk, 0)
    sfin_ref[0] = s_scr[...]


def gla_mixer(p3, s0, lp):
    bsz, seq, _ = p3.shape
    tr = min(512, seq)
    nt = seq // tr
    qkw = GLA_HEADS * GLA_DK
    wa_pad = jnp.zeros((2, 128, qkw), F32)
    wa_pad = wa_pad.at[0, 0:16].set(lp["gla_w_alpha"][0]).at[1, 16:32].set(lp["gla_w_alpha"][1]).astype(BF16)
    ba = lp["gla_b_alpha"].reshape(2, 1, qkw)
    lower = np.tril(np.ones((GLA_CHUNK, GLA_CHUNK), np.float32))
    tri = jnp.asarray(np.stack([lower, lower.T]))

    def in_specs(tile):
        return [pl.BlockSpec((1, tr, qkw), lambda b, i: (b, tile(i), 3072 // qkw)),
                pl.BlockSpec((1, tr, qkw), lambda b, i: (b, tile(i), 3584 // qkw)),
                pl.BlockSpec((1, tr, GROUP_W), lambda b, i: (b, tile(i), 4)),
                pl.BlockSpec((1, tr, 128), lambda b, i: (b, tile(i), 72))]

    fwd = lambda i: i
    bwd = lambda i: nt - 1 - i
    o_f, o_b, sfin = pl.pallas_call(
        functools.partial(_gla_body, nch=tr // GLA_CHUNK),
        grid=(bsz, nt),
        in_specs=in_specs(fwd) + in_specs(bwd) + [
            _full_spec((2, 128, qkw)), _full_spec((2, 1, qkw)), _full_spec((2, GLA_CHUNK, GLA_CHUNK)),
            pl.BlockSpec((1, 2, GLA_HEADS, GLA_DK, GLA_DV), lambda b, i: (b, 0, 0, 0, 0))],
        out_specs=[pl.BlockSpec((1, 1, tr, GROUP_W), lambda b, i: (0, b, fwd(i), 0)),
                   pl.BlockSpec((1, 1, tr, GROUP_W), lambda b, i: (0, b, bwd(i), 0)),
                   pl.BlockSpec((1, 2, GLA_HEADS, GLA_DK, GLA_DV), lambda b, i: (b, 0, 0, 0, 0))],
        out_shape=[jax.ShapeDtypeStruct((1, bsz, seq, GROUP_W), F32),
                   jax.ShapeDtypeStruct((1, bsz, seq, GROUP_W), F32),
                   jax.ShapeDtypeStruct((bsz, 2, GLA_HEADS, GLA_DK, GLA_DV), F32)],
        scratch_shapes=[pltpu.VMEM((2, GLA_HEADS, GLA_DK, GLA_DV), F32)],
        compiler_params=_cp("arbitrary", "arbitrary"),
        name="gla_scan",
    )(p3, p3, p3, p3, p3, p3, p3, p3, wa_pad, ba, tri, s0)
    return (o_f, o_b), sfin


def _hy_conv3_body(x_ref, w_ref, b_ref, o_ref, *, seq):
    x = x_ref[0]
    w = w_ref[...]
    row = lax.broadcasted_iota(jnp.int32, (seq, 1), 0)
    xm = jnp.where(row == 0, 0.0, pltpu.roll(x, 1, 0))
    xp = jnp.where(row == seq - 1, 0.0, pltpu.roll(x, seq - 1, 0))
    o_ref[0] = w[0:1] * xm + w[1:2] * x + w[2:3] * xp + b_ref[...]


def hy_conv3(p3, lp):
    bsz, seq, _ = p3.shape
    w3 = 3 * GROUP_W
    return pl.pallas_call(
        functools.partial(_hy_conv3_body, seq=seq),
        grid=(bsz, w3 // 128),
        in_specs=[pl.BlockSpec((1, seq, 128), lambda b, c: (b, 0, 48 + c)),
                  pl.BlockSpec((3, 128), lambda b, c: (0, c)),
                  pl.BlockSpec((1, 128), lambda b, c: (0, c))],
        out_specs=pl.BlockSpec((1, seq, 128), lambda b, c: (b, 0, c)),
        out_shape=jax.ShapeDtypeStruct((bsz, seq, w3), F32),
        compiler_params=_cp("arbitrary", "arbitrary"),
        name="hy_conv3",
    )(p3, lp["hy_conv_w"], lp["hy_conv_b"].reshape(1, w3))


def _hy_filter_body(bv_ref, w1_ref, b1_ref, f0_ref, w2_ref, b2_ref, f1_ref, w3_ref, dl_ref, o_ref, *, tr, seq):
    i = pl.program_id(0)
    n = 2 * seq
    t = lax.broadcasted_iota(jnp.int32, (tr, 1), 0) + i * tr
    pos = jnp.where(t < seq, t, n - t).astype(F32)
    tt = pos / seq
    w = (2.0 * math.pi) * pos / seq
    lane = lax.broadcasted_iota(jnp.int32, (tr, 128), 1)
    arg = w * bv_ref[...]
    feats = jnp.where(lane == 0, tt,
                      jnp.where(lane <= HY_BANDS, jnp.cos(arg),
                                jnp.where(lane <= 2 * HY_BANDS, -jnp.sin(arg), 0.0)))
    h = jnp.sin(f0_ref[...] * (jnp.dot(feats, w1_ref[...], preferred_element_type=F32, precision=HI) + b1_ref[...]))
    h = jnp.sin(f1_ref[...] * (jnp.dot(h, w2_ref[...], preferred_element_type=F32, precision=HI) + b2_ref[...]))
    out = jnp.dot(h, w3_ref[0], preferred_element_type=F32, precision=HI)
    out = out * jnp.exp(-tt * dl_ref[...])
    o_ref[...] = jnp.where(t == seq, 0.0, out)


def hy_filter(lp, seq):
    n = 2 * seq
    tr = min(512, seq)
    nt = n // tr
    bands = np.linspace(1e-4, HY_BANDS - 1, HY_BANDS, dtype=np.float32)
    bv = np.zeros((1, 128), np.float32)
    bv[0, 1:1 + HY_BANDS] = bands
    bv[0, 1 + HY_BANDS:1 + 2 * HY_BANDS] = bands
    deltas = np.abs(np.linspace(math.log(1e-2) / 0.3, math.log(1e-2) / 1.5, GROUP_W, dtype=np.float32))
    dl = np.concatenate([deltas, deltas])[None, :]

    def pad2(w, r, c):
        return jnp.zeros((r, c), F32).at[:w.shape[0], :w.shape[1]].set(w)

    w1 = pad2(lp["hy_w1"], 128, 128)
    b1 = pad2(lp["hy_b1"][None, :], 1, 128)
    f0 = pad2(lp["hy_freq"][0][None, :], 1, 128)
    w2 = pad2(lp["hy_w2"], 128, 128)
    b2 = pad2(lp["hy_b2"][None, :], 1, 128)
    f1 = pad2(lp["hy_freq"][1][None, :], 1, 128)
    w3 = lp["hy_w3"].reshape(HY_FFN, 2, 2 * GROUP_W).transpose(1, 0, 2)
    w3 = jnp.zeros((2, 128, 2 * GROUP_W), F32).at[:, :HY_FFN].set(w3)
    half = nt // 2
    vec = lambda i: (0, 0)
    return pl.pallas_call(
        functools.partial(_hy_filter_body, tr=tr, seq=seq),
        grid=(nt,),
        in_specs=[pl.BlockSpec((1, 128), vec), pl.BlockSpec((128, 128), vec), pl.BlockSpec((1, 128), vec),
                  pl.BlockSpec((1, 128), vec), pl.BlockSpec((128, 128), vec), pl.BlockSpec((1, 128), vec),
                  pl.BlockSpec((1, 128), vec),
                  pl.BlockSpec((1, 128, 2 * GROUP_W), lambda i: (jnp.where(i < half, 0, 1), 0, 0)),
                  pl.BlockSpec((1, 2 * GROUP_W), vec)],
        out_specs=pl.BlockSpec((tr, 2 * GROUP_W), lambda i: (i, 0)),
        out_shape=jax.ShapeDtypeStruct((n, 2 * GROUP_W), F32),
        compiler_params=_cp("arbitrary"),
        name="hy_filter",
    )(jnp.asarray(bv), w1, b1, f0, w2, b2, f1, w3, jnp.asarray(dl))


HY_N1 = 128
HY_N2 = 128


@functools.lru_cache(maxsize=None)
def _dft_tables_t():
    n = HY_N1 * HY_N2
    a = np.arange(128)
    ph = 2.0 * np.pi * ((a[:, None] * a[None, :]) % 128) / 128
    cm, sm = np.cos(ph), np.sin(ph)
    f1 = np.concatenate([cm, -sm], axis=1)
    pht = 2.0 * np.pi * (a[:, None] * a[None, :]) / n
    twr, twi = np.cos(pht), -np.sin(pht)
    f2 = np.block([[cm, -sm], [sm, cm]])
    g2 = np.block([[cm, sm], [-sm, cm]])
    g1 = np.concatenate([cm, -sm], axis=0) / n
    return f1, twr, twi, f2, g2, g1


def _hy_tables_args():
    f1, twr, twi, f2, g2, g1 = _dft_tables_t()
    out = [_np_split(m)[0] for m in (f1, f2, g2, g1)]
    out += [jnp.asarray(twr, F32), jnp.asarray(twi, F32)]
    return out


def _full_spec(shape):
    nd = len(shape)
    return pl.BlockSpec(shape, lambda *_: (0,) * nd)


def _dotr(x, f):
    return _dot(x.astype(BF16), f)


def _hy_fwd_t(z3, f1, f2, twr, twi):
    cb = z3.shape[0]
    x = jnp.swapaxes(z3, 1, 2).reshape(cb * 128, 128)
    a3 = _dotr(x, f1).reshape(cb, 128, 256)
    ar, ai = a3[:, :, 0:128], a3[:, :, 128:256]
    br = ar * twr - ai * twi
    bi = ar * twi + ai * twr
    x2 = jnp.concatenate([jnp.swapaxes(br, 1, 2), jnp.swapaxes(bi, 1, 2)], axis=2)
    return _dotr(x2.reshape(cb * 128, 256), f2)


def _hy_inv_t(y, cb, g2, g1, twr, twi):
    b3 = _dotr(y, g2).reshape(cb, 128, 256)
    br = jnp.swapaxes(b3[:, :, 0:128], 1, 2)
    bi = jnp.swapaxes(b3[:, :, 128:256], 1, 2)
    cr = br * twr + bi * twi
    ci = bi * twr - br * twi
    x4 = jnp.concatenate([cr, ci], axis=2).reshape(cb * 128, 256)
    yv = _dotr(x4, g1).reshape(cb, 128, 128)
    return jnp.swapaxes(yv, 1, 2)


HY_SUB = 8


def _hy_spec_t_body(c_ref, f1, f2, g2, g1, twr, twi, o_ref):
    for s in range(c_ref.shape[0] // HY_SUB):
        cs = slice(s * HY_SUB, (s + 1) * HY_SUB)
        z = _hy_fwd_t(c_ref[cs], f1[...], f2[...], twr[...], twi[...])
        o_ref[cs] = z.reshape(HY_SUB, 128, 256)


def _hy_long_t_body(u_ref, g_ref, bias_ref, h_ref, f1, f2, g2, g1, twr, twi, o_ref):
    t1u = u_ref.shape[2]
    cb = HY_SUB
    for s in range(u_ref.shape[1] // cb):
        cs = slice(s * cb, (s + 1) * cb)
        u = u_ref[0, cs]
        z3 = jnp.concatenate([u, jnp.zeros((cb, HY_N1 - t1u, HY_N2), F32)], axis=1)
        z = _hy_fwd_t(z3, f1[...], f2[...], twr[...], twi[...])
        hs = h_ref[cs].reshape(cb * 128, 256)
        zr, zi, hr, hi = z[:, 0:128], z[:, 128:256], hs[:, 0:128], hs[:, 128:256]
        y = jnp.concatenate([zr * hr - zi * hi, zr * hi + zi * hr], axis=1)
        conv = _hy_inv_t(y, cb, g2[...], g1[...], twr[...], twi[...])[:, 0:t1u, :]
        o_ref[0, cs] = g_ref[0, cs] * (conv + bias_ref[cs] * u)


def hy_conv3_t(p3, lp):
    bsz, seq, _ = p3.shape
    w3 = 3 * GROUP_W

    def body(x_ref, w_ref, b_ref, o_ref):
        x = x_ref[0]
        w = w_ref[...]
        row = lax.broadcasted_iota(jnp.int32, (seq, 1), 0)
        xm = jnp.where(row == 0, 0.0, pltpu.roll(x, 1, 0))
        xp = jnp.where(row == seq - 1, 0.0, pltpu.roll(x, seq - 1, 0))
        o_ref[0] = (w[0:1] * xm + w[1:2] * x + w[2:3] * xp + b_ref[...]).T

    return pl.pallas_call(
        body,
        grid=(bsz, w3 // 128),
        in_specs=[pl.BlockSpec((1, seq, 128), lambda b, c: (b, 0, 48 + c)),
                  pl.BlockSpec((3, 128), lambda b, c: (0, c)),
                  pl.BlockSpec((1, 128), lambda b, c: (0, c))],
        out_specs=pl.BlockSpec((1, 128, seq), lambda b, c: (b, c, 0)),
        out_shape=jax.ShapeDtypeStruct((bsz, w3, seq), F32),
        compiler_params=_cp("arbitrary", "arbitrary"),
        name="hy_conv3_t",
    )(p3, lp["hy_conv_w"], lp["hy_conv_b"].reshape(1, w3))


def _hy_filter_t_body(bc_ref, w1_ref, b1_ref, f0_ref, w2_ref, b2_ref, f1_ref, w3_ref, dl_ref, o_ref, *, tr, seq):
    i = pl.program_id(0)
    n = 2 * seq
    t = lax.broadcasted_iota(jnp.int32, (1, tr), 1) + i * tr
    pos = jnp.where(t < seq, t, n - t).astype(F32)
    tt = pos / seq
    w = (2.0 * math.pi) * pos / seq
    row = lax.broadcasted_iota(jnp.int32, (128, tr), 0)
    arg = bc_ref[...] * w
    feats = jnp.where(row == 0, tt,
                      jnp.where(row <= HY_BANDS, jnp.cos(arg),
                                jnp.where(row <= 2 * HY_BANDS, -jnp.sin(arg), 0.0)))
    h = jnp.sin(f0_ref[...] * (jnp.dot(w1_ref[...], feats, preferred_element_type=F32, precision=HI) + b1_ref[...]))
    h = jnp.sin(f1_ref[...] * (jnp.dot(w2_ref[...], h, preferred_element_type=F32, precision=HI) + b2_ref[...]))
    out = _dot(w3_ref[0], h.astype(BF16))
    out = out * jnp.exp(-dl_ref[...] * tt)
    o_ref[...] = jnp.where(t == seq, 0.0, out)


def hy_filter_t(lp, seq):
    n = 2 * seq
    tr = 512
    nt = n // tr
    bands = np.linspace(1e-4, HY_BANDS - 1, HY_BANDS, dtype=np.float32)
    bc = np.zeros((128, 1), np.float32)
    bc[1:1 + HY_BANDS, 0] = bands
    bc[1 + HY_BANDS:1 + 2 * HY_BANDS, 0] = bands
    deltas = np.abs(np.linspace(math.log(1e-2) / 0.3, math.log(1e-2) / 1.5, GROUP_W, dtype=np.float32))
    dl = np.concatenate([deltas, deltas])[:, None]

    def pad2(w, r, c):
        return jnp.zeros((r, c), F32).at[:w.shape[0], :w.shape[1]].set(w)

    w1 = pad2(lp["hy_w1"].T, 128, 128)
    b1 = pad2(lp["hy_b1"][:, None], 128, 1)
    f0 = pad2(lp["hy_freq"][0][:, None], 128, 1)
    w2 = pad2(lp["hy_w2"].T, 128, 128)
    b2 = pad2(lp["hy_b2"][:, None], 128, 1)
    f1 = pad2(lp["hy_freq"][1][:, None], 128, 1)
    w3 = lp["hy_w3"].reshape(HY_FFN, 2, 2 * GROUP_W).transpose(1, 2, 0)
    w3 = jnp.zeros((2, 2 * GROUP_W, 128), F32).at[:, :, :HY_FFN].set(w3).astype(BF16)
    half = nt // 2
    return pl.pallas_call(
        functools.partial(_hy_filter_t_body, tr=tr, seq=seq),
        grid=(nt,),
        in_specs=[_full_spec((128, 1)), _full_spec((128, 128)), _full_spec((128, 1)), _full_spec((128, 1)),
                  _full_spec((128, 128)), _full_spec((128, 1)), _full_spec((128, 1)),
                  pl.BlockSpec((1, 2 * GROUP_W, 128), lambda i: (jnp.where(i < half, 0, 1), 0, 0)),
                  _full_spec((2 * GROUP_W, 1))],
        out_specs=pl.BlockSpec((2 * GROUP_W, tr), lambda i: (0, i)),
        out_shape=jax.ShapeDtypeStruct((2 * GROUP_W, n), F32),
        compiler_params=_cp("arbitrary"),
        name="hy_filter_t",
    )(jnp.asarray(bc), w1, b1, f0, w2, b2, f1, w3, jnp.asarray(dl))


HY_CB = 32


def hyena_long(p3, lp):
    bsz, seq, _ = p3.shape
    t1u = seq // HY_N2
    tabs = _hy_tables_args()
    tab_specs = [_full_spec(t.shape) for t in tabs]
    hzt = hy_conv3_t(p3, lp).reshape(bsz, 3 * GROUP_W, t1u, HY_N2)
    circ = hy_filter_t(lp, seq).reshape(2 * GROUP_W, HY_N1, HY_N2)
    cb = HY_CB
    hspec = pl.pallas_call(
        _hy_spec_t_body,
        grid=(2 * GROUP_W // cb,),
        in_specs=[pl.BlockSpec((cb, HY_N1, HY_N2), lambda c: (c, 0, 0))] + tab_specs,
        out_specs=pl.BlockSpec((cb, HY_N1, 2 * HY_N2), lambda c: (c, 0, 0)),
        out_shape=jax.ShapeDtypeStruct((2 * GROUP_W, HY_N1, 2 * HY_N2), F32),
        compiler_params=_cp("arbitrary"),
        name="hy_spec_t",
    )(circ, *tabs)
    ncb = GROUP_W // cb
    u, ub0 = hzt, 0
    for o in range(2):
        gb0 = (1 + o) * ncb
        u = pl.pallas_call(
            _hy_long_t_body,
            grid=(ncb, bsz),
            in_specs=[pl.BlockSpec((1, cb, t1u, HY_N2), lambda c, b, ub0=ub0: (b, ub0 + c, 0, 0)),
                      pl.BlockSpec((1, cb, t1u, HY_N2), lambda c, b, gb0=gb0: (b, gb0 + c, 0, 0)),
                      pl.BlockSpec((cb, 1, 1), lambda c, b: (c, 0, 0)),
                      pl.BlockSpec((cb, HY_N1, 2 * HY_N2), lambda c, b, o=o: (o * ncb + c, 0, 0))] + tab_specs,
            out_specs=pl.BlockSpec((1, cb, t1u, HY_N2), lambda c, b: (b, c, 0, 0)),
            out_shape=jax.ShapeDtypeStruct((bsz, GROUP_W, t1u, HY_N2), F32),
            compiler_params=_cp("arbitrary", "arbitrary"),
            name="hy_long_t",
        )(u, hzt, lp["hy_bias"][o].reshape(GROUP_W, 1, 1), hspec, *tabs)
        ub0 = 0
    return u.reshape(bsz, GROUP_W, seq)


@functools.lru_cache(maxsize=None)
def _dense_dft_tables(seq):
    n = 2 * seq
    k = np.arange(n)[:, None]
    t = np.arange(n)[None, :]
    ph = 2.0 * np.pi * ((k * t) % n) / n
    fwd = np.concatenate([np.cos(ph), -np.sin(ph)], axis=0)
    inv = np.concatenate([np.cos(ph).T, -np.sin(ph).T], axis=1)[:seq] / n
    return fwd, inv


def _hy_dense_spec_body(c_ref, fh_ref, fl_ref, o_ref):
    o_ref[...] = _dot3(fh_ref[...], fl_ref[...], c_ref[...])


def _hy_dense_body(z_ref, h_ref, fh_ref, fl_ref, gh_ref, gl_ref, g_ref, bias_ref, o_ref, *, n):
    u = z_ref[0]
    z = _dot3(fh_ref[...], fl_ref[...], u)
    zr, zi = z[0:n], z[n:]
    hr, hi = h_ref[0:n], h_ref[n:2 * n]
    y = jnp.concatenate([zr * hr - zi * hi, zr * hi + zi * hr], axis=0)
    conv = _dot3(gh_ref[...], gl_ref[...], y)
    o_ref[0] = g_ref[0] * (conv + bias_ref[...] * u)


def hy_dense(hzc, circ, lp, cb=512):
    bsz, seq, _ = hzc.shape
    n = 2 * seq
    fwd, inv = _dense_dft_tables(seq)
    fh, fl = _np_split(fwd)
    gh, gl = _np_split(inv)
    hspec = pl.pallas_call(
        _hy_dense_spec_body,
        grid=(2 * GROUP_W // cb,),
        in_specs=[pl.BlockSpec((n, cb), lambda c: (0, c)),
                  pl.BlockSpec((2 * n, n), lambda c: (0, 0)),
                  pl.BlockSpec((2 * n, n), lambda c: (0, 0))],
        out_specs=pl.BlockSpec((2 * n, cb), lambda c: (0, c)),
        out_shape=jax.ShapeDtypeStruct((2 * n, 2 * GROUP_W), F32),
        compiler_params=_cp("arbitrary"),
        name="hy_dense_spec",
    )(circ, fh, fl)
    fh_in, fl_in = fh[:, :seq], fl[:, :seq]
    u, uc0 = hzc, 0
    ncb = GROUP_W // cb
    for o in range(2):
        gc0 = (1 + o) * ncb
        u = pl.pallas_call(
            functools.partial(_hy_dense_body, n=n),
            grid=(bsz, ncb),
            in_specs=[pl.BlockSpec((1, seq, cb), lambda b, c, uc0=uc0: (b, 0, uc0 + c)),
                      pl.BlockSpec((2 * n, cb), lambda b, c, o=o: (0, o * ncb + c)),
                      pl.BlockSpec((2 * n, seq), lambda b, c: (0, 0)),
                      pl.BlockSpec((2 * n, seq), lambda b, c: (0, 0)),
                      pl.BlockSpec((seq, 2 * n), lambda b, c: (0, 0)),
                      pl.BlockSpec((seq, 2 * n), lambda b, c: (0, 0)),
                      pl.BlockSpec((1, seq, cb), lambda b, c, gc0=gc0: (b, 0, gc0 + c)),
                      pl.BlockSpec((1, cb), lambda b, c: (0, c))],
            out_specs=pl.BlockSpec((1, seq, cb), lambda b, c: (b, 0, c)),
            out_shape=jax.ShapeDtypeStruct((bsz, seq, GROUP_W), F32),
            compiler_params=_cp("arbitrary", "arbitrary"),
            name="hy_dense_conv",
        )(u, hspec, fh_in, fl_in, gh, gl, hzc, lp["hy_bias"][o].reshape(1, GROUP_W))
        uc0 = 0
    return u


def hyena_mixer(p3, lp):
    bsz, seq, _ = p3.shape
    if 2 * seq == HY_N1 * HY_N2:
        return hyena_long(p3, lp), True
    hzc = hy_conv3(p3, lp)
    circ = hy_filter(lp, seq)
    return hy_dense(hzc, circ, lp), False


def _rms(y, g):
    return y * lax.rsqrt(jnp.mean(y * y, axis=-1, keepdims=True) + EPS) * g


def _mix_body(s5_ref, hf_ref, hb_ref, lg_ref, of_ref, ob_ref, gg_ref, hy_ref, gn_ref, mg_ref, o_ref, *, hy_t):
    w = GROUP_W
    y_hy = hy_ref[0].T if hy_t else hy_ref[...]
    o_ref[:, 0:w] = _rms(s5_ref[...], mg_ref[:, 0:w]).astype(o_ref.dtype)
    y_lru = (hf_ref[0] + hb_ref[0]) * jax.nn.gelu(lg_ref[...])
    o_ref[:, w:2 * w] = _rms(y_lru, mg_ref[:, w:2 * w]).astype(o_ref.dtype)
    o = of_ref[0] + ob_ref[0]
    gg = gg_ref[...]
    heads = []
    for h in range(GLA_HEADS):
        sl = slice(h * GLA_DV, (h + 1) * GLA_DV)
        heads.append(_rms(o[:, sl], gn_ref[...]) * (gg[:, sl] * jax.nn.sigmoid(gg[:, sl])))
    y_gla = jnp.concatenate(heads, axis=-1)
    o_ref[:, 2 * w:3 * w] = _rms(y_gla, mg_ref[:, 2 * w:3 * w]).astype(o_ref.dtype)
    o_ref[:, 3 * w:4 * w] = _rms(y_hy, mg_ref[:, 3 * w:4 * w]).astype(o_ref.dtype)


def mix_assemble(p2, y_s5, h_lru, o_gla, y_hy, hy_t, seq, lp, tm=256):
    m = p2.shape[0]
    w = GROUP_W
    row = lambda i: (i, 0)
    tpb = seq // tm
    if hy_t:
        hy_spec = pl.BlockSpec((1, w, tm), lambda i: (i // tpb, 0, i % tpb))
    else:
        hy_spec = pl.BlockSpec((tm, w), row)
        y_hy = y_hy.reshape(m, w)
    return pl.pallas_call(
        functools.partial(_mix_body, hy_t=hy_t),
        grid=(m // tm,),
        in_specs=[pl.BlockSpec((tm, w), row),
                  pl.BlockSpec((1, tm, w), lambda i: (0, i, 0)),
                  pl.BlockSpec((1, tm, w), lambda i: (1, i, 0)),
                  pl.BlockSpec((tm, w), lambda i: (i, 2)),
                  pl.BlockSpec((1, tm, w), lambda i: (0, i, 0)),
                  pl.BlockSpec((1, tm, w), lambda i: (0, i, 0)),
                  pl.BlockSpec((tm, w), lambda i: (i, 5)),
                  hy_spec,
                  pl.BlockSpec((1, GLA_DV), lambda i: (0, 0)),
                  pl.BlockSpec((1, 4 * w), lambda i: (0, 0))],
        out_specs=pl.BlockSpec((tm, 4 * w), row),
        out_shape=jax.ShapeDtypeStruct((m, 4 * w), BF16),
        compiler_params=_cp("arbitrary"),
        name="mix_assemble",
    )(y_s5.reshape(m, w), h_lru.reshape(2, m, w), h_lru.reshape(2, m, w), p2,
      o_gla[0].reshape(1, m, w), o_gla[1].reshape(1, m, w), p2, y_hy,
      lp["gla_norm_g"].reshape(1, GLA_DV), lp["mix_norm_g"].reshape(1, 4 * w))


FFN_LAG = 2
FFN_SUB_ROWS = 256


def _ffn1_body(h_ref, wg_ref, wu_ref, cw_ref, cb_ref, o_ref, g_scr, u_scr, wgb_scr, wub_scr, *, tm, gw, tps, nt):
    s = pl.program_id(1)

    @pl.when(s == 0)
    def _():
        g_scr[...] = jnp.zeros_like(g_scr)
        u_scr[...] = jnp.zeros_like(u_scr)
        wgb_scr[...] = wg_ref[...].astype(BF16)
        wub_scr[...] = wu_ref[...].astype(BF16)

    t = s - FFN_LAG
    cur = s % 3
    mid = (s + 1) % 3
    nxt = (s + 2) % 3
    g_cur, g_mid, g_nxt = g_scr.at[cur], g_scr.at[mid], g_scr.at[nxt]
    u_cur, u_mid = u_scr.at[cur], u_scr.at[mid]

    rb = min(FFN_SUB_ROWS, tm)
    nsub = tm // rb
    gpt = tm // gw
    tn = o_ref.shape[1]
    col = lax.broadcasted_iota(jnp.int32, (gw, 1), 0)
    first_col = col == 0
    last_col = col == gw - 1

    def grid_row(rho, ls):
        if rho < 0:
            return jnp.where(t % tps == 0, 0.0, g_cur[tm - gw:tm, ls])
        if rho >= gpt:
            return jnp.where(t % tps == tps - 1, 0.0, g_nxt[0:gw, ls])
        return g_mid[rho * gw:(rho + 1) * gw, ls]

    never = s < 0
    chain = [None]

    def conv_finish(r):
        for rho in range(r * rb // gw, (r + 1) * rb // gw):
            for lh in range(tn // 128):
                ls = slice(lh * 128, (lh + 1) * 128)
                acc = jnp.broadcast_to(cb_ref[:, ls], (gw, 128))
                if chain[0] is not None:
                    acc = jnp.where(never, chain[0], acc)
                for dr in range(3):
                    src = grid_row(rho + dr - 1, ls)
                    left = jnp.where(first_col, 0.0, pltpu.roll(src, 1, 0))
                    right = jnp.where(last_col, 0.0, pltpu.roll(src, gw - 1, 0))
                    acc = (acc + cw_ref[3 * dr:3 * dr + 1, ls] * left + cw_ref[3 * dr + 1:3 * dr + 2, ls] * src
                           + cw_ref[3 * dr + 2:3 * dr + 3, ls] * right)
                rows = slice(rho * gw, (rho + 1) * gw)
                o_ref[rows, ls] = (acc * jax.nn.sigmoid(acc) * u_mid[rows, ls]).astype(o_ref.dtype)
                chain[0] = acc

    @pl.when(s < nt)
    def _():
        chain[0] = None
        for r in range(nsub):
            conv_finish(r)
            h = h_ref[r * rb:(r + 1) * rb, :]
            g_cur[r * rb:(r + 1) * rb] = _dot(h, wgb_scr[...])
            u_cur[r * rb:(r + 1) * rb] = _dot(h, wub_scr[...])

    @pl.when(s >= nt)
    def _():
        chain[0] = None
        for r in range(nsub):
            conv_finish(r)


def _ffn1_one_row_body(h_ref, wg_ref, wu_ref, cw_ref, cb_ref, o_ref, g_scr, u_scr, wgb_scr, wub_scr, *, gw):
    @pl.when(pl.program_id(1) == 0)
    def _():
        wgb_scr[...] = wg_ref[...].astype(BF16)
        wub_scr[...] = wu_ref[...].astype(BF16)

    h = h_ref[...]
    g_scr[...] = _dot(h, wgb_scr[...])
    u_scr[...] = _dot(h, wub_scr[...])
    col = lax.broadcasted_iota(jnp.int32, (gw, 1), 0)
    for lh in range(o_ref.shape[1] // 128):
        ls = slice(lh * 128, (lh + 1) * 128)
        src = g_scr[:, ls]
        left = jnp.where(col == 0, 0.0, pltpu.roll(src, 1, 0))
        right = jnp.where(col == gw - 1, 0.0, pltpu.roll(src, gw - 1, 0))
        acc = cb_ref[:, ls] + cw_ref[3:4, ls] * left + cw_ref[4:5, ls] * src + cw_ref[5:6, ls] * right
        o_ref[:, ls] = (acc * jax.nn.sigmoid(acc) * u_scr[:, ls]).astype(o_ref.dtype)


def ffn1(h2, wg, wu, cw, cb, layer, *, seq, gw, tm, tn=256):
    m, d = h2.shape
    ff = wg.shape[-1]
    tps = seq // tm
    nt = m // tm
    w_specs = [pl.BlockSpec((None, d, tn), lambda j, s: (layer, 0, j)),
               pl.BlockSpec((None, d, tn), lambda j, s: (layer, 0, j)),
               pl.BlockSpec((None, 9, tn), lambda j, s: (layer, 0, j)),
               pl.BlockSpec((None, 1, tn), lambda j, s: (layer, 0, j))]
    if tps == 1 and tm == gw:
        return pl.pallas_call(
            functools.partial(_ffn1_one_row_body, gw=gw),
            grid=(ff // tn, nt),
            in_specs=[pl.BlockSpec((tm, d), lambda j, s: (s, 0))] + w_specs,
            out_specs=pl.BlockSpec((tm, tn), lambda j, s: (s, j)),
            out_shape=jax.ShapeDtypeStruct((m, ff), BF16),
            scratch_shapes=[pltpu.VMEM((tm, tn), F32), pltpu.VMEM((tm, tn), F32),
                            pltpu.VMEM((d, tn), BF16), pltpu.VMEM((d, tn), BF16)],
            compiler_params=_cp("arbitrary", "arbitrary"),
            name="ffn_gate_up_one_row",
        )(h2, wg, wu, cw, cb)
    return pl.pallas_call(
        functools.partial(_ffn1_body, tm=tm, gw=gw, tps=tps, nt=nt),
        grid=(ff // tn, nt + FFN_LAG),
        in_specs=[pl.BlockSpec((tm, d), lambda j, s: (jnp.minimum(s, nt - 1), 0)),
                  pl.BlockSpec((None, d, tn), lambda j, s: (layer, 0, j)),
                  pl.BlockSpec((None, d, tn), lambda j, s: (layer, 0, j)),
                  pl.BlockSpec((None, 9, tn), lambda j, s: (layer, 0, j)),
                  pl.BlockSpec((None, 1, tn), lambda j, s: (layer, 0, j))],
        out_specs=pl.BlockSpec((tm, tn), lambda j, s: (jnp.maximum(s - FFN_LAG, 0), j)),
        out_shape=jax.ShapeDtypeStruct((m, ff), BF16),
        scratch_shapes=[pltpu.VMEM((3, tm, tn), F32), pltpu.VMEM((3, tm, tn), F32),
                        pltpu.VMEM((d, tn), BF16), pltpu.VMEM((d, tn), BF16)],
        compiler_params=_cp("arbitrary", "arbitrary"),
        name="ffn_gate_up",
    )(h2, wg, wu, cw, cb)


def _token_mixers(p2, bsz, seq, states, lp, wglu_bf, layer, need_output):
    p3 = p2.reshape(bsz, seq, N_COL_PAD)
    y_s5, st_s5 = s5_mixer(p3, states[0], lp, wglu_bf, layer, need_output)
    h_lru, st_lru = lru_mixer(p3, states[1], lp)
    o_gla, st_gla = gla_mixer(p3, states[2], lp)
    new_states = (st_s5, st_lru, st_gla)
    if not need_output:
        return None, new_states
    y_hy, hy_t = hyena_mixer(p3, lp)
    return mix_assemble(p2, y_s5, h_lru, o_gla, y_hy, hy_t, seq, lp), new_states


def kernel(x, c, ctx, c_ctx, w_ada, b_ada, norm_mix_g, norm_mlp_g, w_in, s5_lam_re, s5_lam_im, s5_log_step, s5_b_re, s5_b_im, s5_c_re, s5_c_im, s5_d, s5_w_glu, s5_b_glu, lru_conv_w, lru_conv_b, lru_w_a, lru_b_a, lru_w_x, lru_b_x, lru_lam, gla_w_alpha, gla_b_alpha, gla_norm_g, hy_conv_w, hy_conv_b, hy_w1, hy_b1, hy_w2, hy_b2, hy_w3, hy_freq, hy_bias, mix_norm_g, w_out, mlp_w_gate, mlp_w_up, mlp_conv_w, mlp_conv_b, mlp_w_down, final_norm_g):
    bsz, seq, d = x.shape
    clen = ctx.shape[1]
    depth = w_ada.shape[0]
    grid_w = 64
    params = dict(
        s5_lam_re=s5_lam_re, s5_lam_im=s5_lam_im, s5_log_step=s5_log_step, s5_b_re=s5_b_re, s5_b_im=s5_b_im,
        s5_c_re=s5_c_re, s5_c_im=s5_c_im, s5_d=s5_d, s5_b_glu=s5_b_glu,
        lru_conv_w=lru_conv_w, lru_conv_b=lru_conv_b, lru_w_a=lru_w_a, lru_b_a=lru_b_a, lru_w_x=lru_w_x,
        lru_b_x=lru_b_x, lru_lam=lru_lam, gla_w_alpha=gla_w_alpha, gla_b_alpha=gla_b_alpha, gla_norm_g=gla_norm_g,
        hy_conv_w=hy_conv_w, hy_conv_b=hy_conv_b, hy_w1=hy_w1, hy_b1=hy_b1, hy_w2=hy_w2, hy_b2=hy_b2, hy_w3=hy_w3,
        hy_freq=hy_freq, hy_bias=hy_bias, mix_norm_g=mix_norm_g)

    w_in_bf = jnp.concatenate(
        [w_in[..., 0:6144], w_in[..., 6176:9248], w_in[..., 6144:6176],
         jnp.zeros((depth, d, N_COL_PAD - 9248), w_in.dtype)], axis=-1).astype(BF16)
    w_out_bf = w_out.astype(BF16)
    wd_bf = mlp_w_down.astype(BF16)
    wglu_bf = s5_w_glu.astype(BF16)
    conv_w9 = mlp_conv_w.reshape(depth, 9, D_FF)
    conv_b = mlp_conv_b.reshape(depth, 1, D_FF)

    cvec = jnp.zeros((8, d), F32).at[0:bsz].set(c).at[bsz].set(c_ctx)
    mod = ada_mod(cvec, w_ada, b_ada)

    x2 = x.reshape(bsz * seq, d)
    c2 = ctx.reshape(bsz * clen, d)
    zero_states = (jnp.zeros((bsz, 2, S5_JB, 1, 1024), F32),
                   jnp.zeros((bsz, 2, 1, GROUP_W), F32),
                   jnp.zeros((bsz, 2, GLA_HEADS, GLA_DK, GLA_DV), F32))

    for l in range(depth):
        last = l == depth - 1
        lp = {k: v[l] for k, v in params.items()}
        mx = mod[l, 0:bsz].reshape(bsz, 1, 6, d)
        mc = mod[l, bsz:bsz + 1].reshape(1, 1, 6, d)
        sh1, sc1, g1, sh2, sc2, g2 = (mx[:, :, i] for i in range(6))
        csh1, csc1, cg1, csh2, csc2, cg2 = (mc[:, :, i] for i in range(6))

        hc = normmod(c2, norm_mix_g[l], csh1, csc1, bsz * clen, BF16)
        pc = matmul(hc, w_in_bf, l, tm=bsz * clen, tn=512, name="in_proj_ctx")
        yc, ctx_states = _token_mixers(pc, bsz, clen, zero_states, lp, wglu_bf, l, need_output=not last)

        hx = normmod(x2, norm_mix_g[l], sh1, sc1, seq, BF16)
        px = matmul(hx, w_in_bf, l, tm=2048, tn=512, name="in_proj")
        yx, _ = _token_mixers(px, bsz, seq, ctx_states, lp, wglu_bf, l, need_output=True)
        x2 = matmul(yx, w_out_bf, l, tm=1024, tn=512, res=x2, gate=g1, rows_per_gate=seq, name="out_proj")
        h2 = normmod(x2, norm_mlp_g[l], sh2, sc2, seq, BF16)
        act = ffn1(h2, mlp_w_gate, mlp_w_up, conv_w9, conv_b, l, seq=seq, gw=grid_w, tm=1024)
        x2 = matmul(act, wd_bf, l, tm=512, tn=512, res=x2, gate=g2, rows_per_gate=seq, name="down_proj")

        if not last:
            c2 = matmul(yc, w_out_bf, l, tm=bsz * clen, tn=512, res=c2, gate=cg1, rows_per_gate=bsz * clen,
                        name="out_proj_ctx")
            hc2 = normmod(c2, norm_mlp_g[l], csh2, csc2, bsz * clen, BF16)
            actc = ffn1(hc2, mlp_w_gate, mlp_w_up, conv_w9, conv_b, l, seq=clen, gw=clen, tm=clen)
            c2 = matmul(actc, wd_bf, l, tm=bsz * clen, tn=512, res=c2, gate=cg2, rows_per_gate=bsz * clen,
                        name="down_proj_ctx")

    zeros = jnp.zeros((1, 1, d), F32)
    out = normmod(x2, final_norm_g, zeros, zeros, bsz * seq, F32)
    return out.reshape(bsz, seq, d)
```

```python
import functools
import math

import numpy as np
import jax
import jax.numpy as jnp
from jax import lax
from jax.experimental import pallas as pl
from jax.experimental.pallas import tpu as pltpu

F32 = jnp.float32
BF16 = jnp.bfloat16
HI = lax.Precision.HIGHEST

EPS = 1e-6
GROUP_W = 1024
N_COL_PAD = 9728
S5_GROUPS = 64
S5_CH = 16
S5_STATE = 64
S5_SEG = 8
S5_JB = 8
LRU_HEADS = 16
LRU_C = 8.0
GLA_HEADS = 4
GLA_DK = 128
GLA_DV = 256
GLA_CHUNK = 64
GLA_GATE_NORM = 16.0
HY_BANDS = 16
HY_FFN = 64
D_FF = 11008
V7X_VMEM_LIMIT = 56 * 1024 * 1024


def _cp(*sem):
    return pltpu.CompilerParams(dimension_semantics=sem, vmem_limit_bytes=V7X_VMEM_LIMIT)


def _dot(a, b):
    return jnp.dot(a, b, preferred_element_type=F32)


def _split(x):
    hi = x.astype(BF16)
    lo = (x - hi.astype(F32)).astype(BF16)
    return hi, lo


def _dot3(fh, fl, x):
    xh, xl = _split(x)
    return _dot(fh, xh) + _dot(fh, xl) + _dot(fl, xh)


def _np_split(a):
    bf = jnp.dtype(BF16)
    hi = np.asarray(a, np.float64).astype(bf)
    lo = (np.asarray(a, np.float64) - hi.astype(np.float64)).astype(bf)
    return jnp.asarray(hi), jnp.asarray(lo)


def _ada_body(c_ref, w_ref, b_ref, o_ref):
    c = c_ref[...]
    a = (c * jax.nn.sigmoid(c)).astype(BF16)
    o_ref[...] = _dot(a, w_ref[...].astype(BF16)) + b_ref[...]


def ada_mod(cvec, w_ada, b_ada, tn=1024):
    depth, d, n = w_ada.shape
    return pl.pallas_call(
        _ada_body,
        grid=(depth, n // tn),
        in_specs=[pl.BlockSpec((8, d), lambda l, j: (0, 0)),
                  pl.BlockSpec((None, d, tn), lambda l, j: (l, 0, j)),
                  pl.BlockSpec((None, 1, tn), lambda l, j: (l, 0, j))],
        out_specs=pl.BlockSpec((None, 8, tn), lambda l, j: (l, 0, j)),
        out_shape=jax.ShapeDtypeStruct((depth, 8, n), F32),
        compiler_params=_cp("arbitrary", "arbitrary"),
        name="ada_mod",
    )(cvec, w_ada, b_ada.reshape(depth, 1, n))


def _normmod_body(x_ref, g_ref, sh_ref, sc_ref, o_ref):
    x = x_ref[...]
    y = x * lax.rsqrt(jnp.mean(x * x, axis=-1, keepdims=True) + EPS) * g_ref[...]
    o_ref[...] = (y * (1.0 + sc_ref[0]) + sh_ref[0]).astype(o_ref.dtype)


def normmod(x2d, g, sh, sc, rows_per_mod, out_dtype, tm=256):
    m, d = x2d.shape
    tpm = rows_per_mod // tm
    return pl.pallas_call(
        _normmod_body,
        grid=(m // tm,),
        in_specs=[pl.BlockSpec((tm, d), lambda i: (i, 0)),
                  pl.BlockSpec((1, d), lambda i: (0, 0)),
                  pl.BlockSpec((1, 1, d), lambda i: (i // tpm, 0, 0)),
                  pl.BlockSpec((1, 1, d), lambda i: (i // tpm, 0, 0))],
        out_specs=pl.BlockSpec((tm, d), lambda i: (i, 0)),
        out_shape=jax.ShapeDtypeStruct((m, d), out_dtype),
        compiler_params=_cp("arbitrary"),
        name="normmod",
    )(x2d, g.reshape(1, d), sh, sc)


def _mm_body(*refs, nk, has_res):
    if has_res:
        a_ref, w_ref, res_ref, gate_ref, o_ref = refs[:5]
        scr = refs[5:]
    else:
        a_ref, w_ref, o_ref = refs[:3]
        scr = refs[3:]

    def epilogue(acc):
        if has_res:
            o_ref[...] = res_ref[...] + gate_ref[0] * acc
        else:
            o_ref[...] = acc.astype(o_ref.dtype)

    if nk == 1:
        epilogue(_dot(a_ref[...], w_ref[...]))
    else:
        acc_ref = scr[0]
        k = pl.program_id(2)

        @pl.when(k == 0)
        def _():
            acc_ref[...] = jnp.zeros_like(acc_ref)

        acc_ref[...] += _dot(a_ref[...], w_ref[...])

        @pl.when(k == nk - 1)
        def _():
            epilogue(acc_ref[...])


def matmul(a, w, layer, *, tm, tn, tk=None, res=None, gate=None, rows_per_gate=None, name="matmul"):
    m, kdim = a.shape
    n = w.shape[-1]
    tk = kdim if tk is None else tk
    nk = kdim // tk
    has_res = res is not None
    in_specs = [pl.BlockSpec((tm, tk), lambda i, j, k: (i, k)),
                pl.BlockSpec((None, tk, tn), lambda i, j, k: (layer, k, j))]
    args = [a, w]
    if has_res:
        tpg = rows_per_gate // tm
        in_specs += [pl.BlockSpec((tm, tn), lambda i, j, k: (i, j)),
                     pl.BlockSpec((1, 1, tn), lambda i, j, k: (i // tpg, 0, j))]
        args += [res, gate]
    return pl.pallas_call(
        functools.partial(_mm_body, nk=nk, has_res=has_res),
        grid=(m // tm, n // tn, nk),
        in_specs=in_specs,
        out_specs=pl.BlockSpec((tm, tn), lambda i, j, k: (i, j)),
        out_shape=jax.ShapeDtypeStruct((m, n), F32),
        scratch_shapes=[pltpu.VMEM((tm, tn), F32)] if nk > 1 else [],
        compiler_params=_cp("arbitrary", "arbitrary", "arbitrary"),
        name=name,
    )(*args)


def _s5_params(lp, lseg):
    lam_re, lam_im = lp["s5_lam_re"], lp["s5_lam_im"]
    step = jnp.exp(lp["s5_log_step"])[:, :, None]
    mag = jnp.exp(lam_re * step)
    ab_re = mag * jnp.cos(lam_im * step)
    ab_im = mag * jnp.sin(lam_im * step)
    den = lam_re * lam_re + lam_im * lam_im
    co_re = ((ab_re - 1.0) * lam_re + ab_im * lam_im) / den
    co_im = (ab_im * lam_re - (ab_re - 1.0) * lam_im) / den
    b_re, b_im = lp["s5_b_re"], lp["s5_b_im"]
    bb_re = co_re[..., None] * b_re - co_im[..., None] * b_im
    bb_im = co_re[..., None] * b_im + co_im[..., None] * b_re
    eye = jnp.eye(8, dtype=F32)

    def in_blocks(bb):
        t = bb.reshape(2, S5_JB, 8, S5_STATE, S5_CH)
        t = jnp.einsum("djgpc,gh->djgchp", t, eye)
        return t.reshape(2, S5_JB, 8 * S5_CH, 8 * S5_STATE)

    def out_blocks(cc):
        t = cc.reshape(2, S5_JB, 8, S5_CH, S5_STATE)
        t = jnp.einsum("djgcp,gh->djgphc", t, eye)
        return t.reshape(2, S5_JB, 8 * S5_STATE, 8 * S5_CH)

    wb = jnp.concatenate([in_blocks(bb_re), in_blocks(bb_im)], axis=-1).astype(BF16)
    wc = jnp.concatenate([out_blocks(lp["s5_c_re"]), -out_blocks(lp["s5_c_im"])], axis=-2).astype(BF16)

    def lanes(t):
        return t.reshape(2, S5_JB, 1, 8 * S5_STATE)

    a = jnp.concatenate([lanes(ab_re), lanes(ab_im)], axis=-1)
    pr, pi = ab_re, ab_im
    for _ in range(int(round(math.log2(lseg)))):
        pr, pi = pr * pr - pi * pi, 2.0 * pr * pi
    al = jnp.concatenate([lanes(pr), lanes(pi)], axis=-1)
    return wb, wc, a, al


S5_JP = 2


def _s5_chains(bsz):
    return [(b, jj) for b in range(bsz) for jj in range(S5_JP)]


def _s5_project_in(u_ref, wb_ref, a_ref, bu_scr, ab_scr, bsz):
    for ch, (b, jj) in enumerate(_s5_chains(bsz)):
        bu_scr[ch] = _dot(u_ref[b, :, jj * 128:(jj + 1) * 128].astype(BF16), wb_ref[0, jj])
    for jj in range(S5_JP):
        ab_scr[jj] = jnp.broadcast_to(a_ref[0, jj], (S5_SEG, 1024))


def _s5_scan_tile(d, t_steps, bu_scr, ab_scr, h_scr, store, bsz):
    chains = _s5_chains(bsz)

    def step(s, carry):
        row = jnp.where(d == 0, s, t_steps - 1 - s)
        off = pl.multiple_of(row * S5_SEG, S5_SEG)
        out = []
        for ch, (_, jj) in enumerate(chains):
            hr, hi = carry[2 * ch], carry[2 * ch + 1]
            ar = ab_scr[jj, :, 0:512]
            ai = ab_scr[jj, :, 512:1024]
            nr = ar * hr - ai * hi + bu_scr[ch, pl.ds(off, S5_SEG), 0:512]
            ni = ar * hi + ai * hr + bu_scr[ch, pl.ds(off, S5_SEG), 512:1024]
            if store:
                bu_scr[ch, pl.ds(off, S5_SEG), 0:512] = nr
                bu_scr[ch, pl.ds(off, S5_SEG), 512:1024] = ni
            out += [nr, ni]
        return tuple(out)

    init = tuple(h_scr[ch, :, lo:lo + 512] for ch in range(len(chains)) for lo in (0, 512))
    fin = lax.fori_loop(0, t_steps, step, init, unroll=2)
    for ch in range(len(chains)):
        h_scr[ch, :, 0:512] = fin[2 * ch]
        h_scr[ch, :, 512:1024] = fin[2 * ch + 1]


def _s5_p1_body(u_ref, wb_ref, a_ref, al_ref, h0_ref, hinit_ref, fin_ref, bu_scr, ab_scr, h_scr, *, t_steps, nt, bsz):
    d = pl.program_id(0)
    i = pl.program_id(2)

    @pl.when(i == 0)
    def _():
        h_scr[...] = jnp.zeros_like(h_scr)

    _s5_project_in(u_ref, wb_ref, a_ref, bu_scr, ab_scr, bsz)
    _s5_scan_tile(d, t_steps, bu_scr, ab_scr, h_scr, False, bsz)

    @pl.when(i == nt - 1)
    def _():
        for ch, (b, jj) in enumerate(_s5_chains(bsz)):
            alr = al_ref[0, jj, :, 0:512]
            ali = al_ref[0, jj, :, 512:1024]
            cr = h0_ref[b, 0, jj, :, 0:512]
            ci = h0_ref[b, 0, jj, :, 512:1024]
            for s in range(S5_SEG):
                k = jnp.where(d == 0, s, S5_SEG - 1 - s)
                hinit_ref[b, 0, jj, pl.ds(k, 1), 0:512] = cr
                hinit_ref[b, 0, jj, pl.ds(k, 1), 512:1024] = ci
                fr = h_scr[ch, pl.ds(k, 1), 0:512]
                fi = h_scr[ch, pl.ds(k, 1), 512:1024]
                cr, ci = alr * cr - ali * ci + fr, alr * ci + ali * cr + fi
            fin_ref[b, 0, jj, :, 0:512] = cr
            fin_ref[b, 0, jj, :, 512:1024] = ci


def _s5_p2_body(u_ref, wb_ref, wc_ref, a_ref, hinit_ref, y_ref, bu_scr, ab_scr, h_scr, *, t_steps, bsz):
    d = pl.program_id(0)
    i = pl.program_id(2)

    @pl.when(i == 0)
    def _():
        for ch, (b, jj) in enumerate(_s5_chains(bsz)):
            h_scr[ch] = hinit_ref[b, 0, jj]

    _s5_project_in(u_ref, wb_ref, a_ref, bu_scr, ab_scr, bsz)
    _s5_scan_tile(d, t_steps, bu_scr, ab_scr, h_scr, True, bsz)
    for ch, (b, jj) in enumerate(_s5_chains(bsz)):
        y_ref[0, b, :, jj * 128:(jj + 1) * 128] = _dot(bu_scr[ch].astype(BF16), wc_ref[0, jj])


def _s5_fin_body(u_ref, yf_ref, yb_ref, d_ref, w_ref, b_ref, o_ref):
    y = u_ref[0] * d_ref[...] + yf_ref[0, 0] + yb_ref[0, 0]
    yg = jax.nn.gelu(y)
    o_ref[0] = yg * jax.nn.sigmoid(_dot(yg.astype(BF16), w_ref[...]) + b_ref[...])


def s5_mixer(p3, h0, lp, wglu_bf, layer, need_output):
    bsz, seq, _ = p3.shape
    lseg = seq // S5_SEG
    t_steps = min(64, lseg)
    nt = lseg // t_steps
    rows = t_steps * S5_SEG
    wb, wc, a, al = _s5_params(lp, lseg)
    u_perm = p3[:, :, 0:GROUP_W].reshape(bsz, S5_SEG, lseg, GROUP_W).transpose(0, 2, 1, 3).reshape(bsz, seq, GROUP_W)

    def tile(d, i):
        return jnp.where(d == 0, i, nt - 1 - i)

    nchain = bsz * S5_JP
    grid = (2, S5_JB // S5_JP, nt)
    u_spec = pl.BlockSpec((bsz, rows, 128 * S5_JP), lambda d, j, i: (0, tile(d, i), j))
    wb_spec = pl.BlockSpec((1, S5_JP, 128, 1024), lambda d, j, i: (d, j, 0, 0))
    a_spec = pl.BlockSpec((1, S5_JP, 1, 1024), lambda d, j, i: (d, j, 0, 0))
    st1_spec = pl.BlockSpec((bsz, 1, S5_JP, 1, 1024), lambda d, j, i: (0, d, j, 0, 0))
    st8_spec = pl.BlockSpec((bsz, 1, S5_JP, S5_SEG, 1024), lambda d, j, i: (0, d, j, 0, 0))
    scratch = [pltpu.VMEM((nchain, rows, 1024), F32), pltpu.VMEM((S5_JP, S5_SEG, 1024), F32),
               pltpu.VMEM((nchain, S5_SEG, 1024), F32)]
    hinit, fin = pl.pallas_call(
        functools.partial(_s5_p1_body, t_steps=t_steps, nt=nt, bsz=bsz),
        grid=grid,
        in_specs=[u_spec, wb_spec, a_spec, a_spec, st1_spec],
        out_specs=[st8_spec, st1_spec],
        out_shape=[jax.ShapeDtypeStruct((bsz, 2, S5_JB, S5_SEG, 1024), F32),
                   jax.ShapeDtypeStruct((bsz, 2, S5_JB, 1, 1024), F32)],
        scratch_shapes=scratch,
        compiler_params=_cp("arbitrary", "arbitrary", "arbitrary"),
        name="s5_pass1",
    )(u_perm, wb, a, al, h0)
    if not need_output:
        return None, fin
    y = pl.pallas_call(
        functools.partial(_s5_p2_body, t_steps=t_steps, bsz=bsz),
        grid=grid,
        in_specs=[u_spec, wb_spec,
                  pl.BlockSpec((1, S5_JP, 1024, 128), lambda d, j, i: (d, j, 0, 0)),
                  a_spec, st8_spec],
        out_specs=pl.BlockSpec((1, bsz, rows, 128 * S5_JP), lambda d, j, i: (d, 0, tile(d, i), j)),
        out_shape=jax.ShapeDtypeStruct((2, bsz, seq, GROUP_W), F32),
        scratch_shapes=scratch,
        compiler_params=_cp("arbitrary", "arbitrary", "arbitrary"),
        name="s5_pass2",
    )(u_perm, wb, wc, a, hinit)
    tr = min(512, seq)
    out = pl.pallas_call(
        _s5_fin_body,
        grid=(bsz, seq // tr),
        in_specs=[pl.BlockSpec((1, tr, GROUP_W), lambda b, i: (b, i, 0)),
                  pl.BlockSpec((1, 1, tr, GROUP_W), lambda b, i: (0, b, i, 0)),
                  pl.BlockSpec((1, 1, tr, GROUP_W), lambda b, i: (1, b, i, 0)),
                  pl.BlockSpec((1, GROUP_W), lambda b, i: (0, 0)),
                  pl.BlockSpec((None, GROUP_W, GROUP_W), lambda b, i: (layer, 0, 0)),
                  pl.BlockSpec((1, GROUP_W), lambda b, i: (0, 0))],
        out_specs=pl.BlockSpec((1, tr, GROUP_W), lambda b, i: (b, i, 0)),
        out_shape=jax.ShapeDtypeStruct((bsz, seq, GROUP_W), F32),
        compiler_params=_cp("arbitrary", "arbitrary"),
        name="s5_finalize",
    )(u_perm, y, y, lp["s5_d"].reshape(1, GROUP_W), wglu_bf, lp["s5_b_glu"].reshape(1, GROUP_W))
    out = out.reshape(bsz, lseg, S5_SEG, GROUP_W).transpose(0, 2, 1, 3).reshape(bsz, seq, GROUP_W)
    return out, fin


def _lru_body(xp_ref, xm_ref, xn_ref, cw_ref, cb_ref, wg_ref, bg_ref, sp_ref, h0_ref, h_ref, fin_ref,
              a_scr, b_scr, hc_scr, *, tile_rows, nt, seq):
    d = pl.program_id(0)
    i = pl.program_id(1)
    ti = jnp.where(d == 0, i, nt - 1 - i)
    bsz = xm_ref.shape[0]

    @pl.when(i == 0)
    def _():
        hc_scr[...] = h0_ref[:, 0]

    n = tile_rows + 16
    rowid = lax.broadcasted_iota(jnp.int32, (n, 1), 0) + (ti * tile_rows - 8)
    valid = (rowid >= 0) & (rowid < seq)
    cw = cw_ref[...]
    for b in range(bsz):
        xe = jnp.where(valid, jnp.concatenate([xp_ref[b], xm_ref[b], xn_ref[b]], axis=0), 0.0)
        xc = (cb_ref[...]
              + cw[0:1] * pltpu.roll(xe, 2, 0)[8:8 + tile_rows]
              + cw[1:2] * pltpu.roll(xe, 1, 0)[8:8 + tile_rows]
              + cw[2:3] * xe[8:8 + tile_rows]
              + cw[3:4] * pltpu.roll(xe, n - 1, 0)[8:8 + tile_rows])
        for cb in range(4):
            lo, hi = cb * 256, (cb + 1) * 256
            xcb = xc[:, lo:hi]
            pre = _dot(xcb.astype(BF16), wg_ref[0, cb])
            r = jax.nn.sigmoid(pre[:, 0:256] + bg_ref[0, :, lo:hi])
            ig = jax.nn.sigmoid(pre[:, 256:512] + bg_ref[0, :, GROUP_W + lo:GROUP_W + hi])
            log_a = -LRU_C * sp_ref[0, :, lo:hi] * r
            a = jnp.exp(log_a)
            a_scr[b, :, lo:hi] = a
            b_scr[b, :, lo:hi] = jnp.sqrt(jnp.tanh(-log_a) * (a * a + 1.0)) * (ig * xcb)

    def step(s, hs):
        t = jnp.where(d == 0, s, tile_rows - 1 - s)
        out = []
        for b in range(bsz):
            h = a_scr[b, pl.ds(t, 1), :] * hs[b] + b_scr[b, pl.ds(t, 1), :]
            b_scr[b, pl.ds(t, 1), :] = h
            out.append(h)
        return tuple(out)

    hs = lax.fori_loop(0, tile_rows, step, tuple(hc_scr[b] for b in range(bsz)), unroll=8)
    for b in range(bsz):
        hc_scr[b] = hs[b]
        fin_ref[b, 0] = hs[b]
    h_ref[0] = b_scr[...]


def _lru_params(lp):
    def blockdiag(w):
        t = w.reshape(2, 4, 4, 64, 64)
        t = jnp.einsum("dcgij,gh->dcgihj", t, jnp.eye(4, dtype=F32))
        return t.reshape(2, 4, 256, 256)

    wg = jnp.concatenate([blockdiag(lp["lru_w_a"]), blockdiag(lp["lru_w_x"])], axis=-1).astype(BF16)
    bg = jnp.concatenate([lp["lru_b_a"], lp["lru_b_x"]], axis=-1).reshape(2, 1, 2 * GROUP_W)
    sp = jax.nn.softplus(-lp["lru_lam"]).reshape(2, 1, GROUP_W)
    return wg, bg, sp


def lru_mixer(p3, h0, lp):
    bsz, seq, _ = p3.shape
    tr = min(512, seq)
    nt = seq // tr
    wg, bg, sp = _lru_params(lp)
    nb8 = seq // 8

    def tile(d, i):
        return jnp.where(d == 0, i, nt - 1 - i)

    return pl.pallas_call(
        functools.partial(_lru_body, tile_rows=tr, nt=nt, seq=seq),
        grid=(2, nt),
        in_specs=[pl.BlockSpec((bsz, 8, GROUP_W), lambda d, i: (0, jnp.maximum(tile(d, i) * (tr // 8) - 1, 0), 1)),
                  pl.BlockSpec((bsz, tr, GROUP_W), lambda d, i: (0, tile(d, i), 1)),
                  pl.BlockSpec((bsz, 8, GROUP_W), lambda d, i: (0, jnp.minimum((tile(d, i) + 1) * (tr // 8), nb8 - 1), 1)),
                  pl.BlockSpec((4, GROUP_W), lambda d, i: (0, 0)),
                  pl.BlockSpec((1, GROUP_W), lambda d, i: (0, 0)),
                  pl.BlockSpec((1, 4, 256, 512), lambda d, i: (d, 0, 0, 0)),
                  pl.BlockSpec((1, 1, 2 * GROUP_W), lambda d, i: (d, 0, 0)),
                  pl.BlockSpec((1, 1, GROUP_W), lambda d, i: (d, 0, 0)),
                  pl.BlockSpec((bsz, 1, 1, GROUP_W), lambda d, i: (0, d, 0, 0))],
        out_specs=[pl.BlockSpec((1, bsz, tr, GROUP_W), lambda d, i: (d, 0, tile(d, i), 0)),
                   pl.BlockSpec((bsz, 1, 1, GROUP_W), lambda d, i: (0, d, 0, 0))],
        out_shape=[jax.ShapeDtypeStruct((2, bsz, seq, GROUP_W), F32),
                   jax.ShapeDtypeStruct((bsz, 2, 1, GROUP_W), F32)],
        scratch_shapes=[pltpu.VMEM((bsz, tr, GROUP_W), F32), pltpu.VMEM((bsz, tr, GROUP_W), F32),
                        pltpu.VMEM((bsz, 1, GROUP_W), F32)],
        compiler_params=_cp("arbitrary", "arbitrary"),
        name="lru_scan",
    )(p3, p3, p3, lp["lru_conv_w"], lp["lru_conv_b"].reshape(1, GROUP_W), wg, bg, sp, h0)


def _log_sigmoid(z):
    return jnp.minimum(z, 0.0) - jnp.log1p(jnp.exp(-jnp.abs(z)))


def _gla_body(qf_ref, kf_ref, vf_ref, lf_ref, qb_ref, kb_ref, vb_ref, lb_ref, wa_ref, ba_ref, tri_ref, s0_ref,
              of_ref, ob_ref, sfin_ref, s_scr, *, nch):
    i = pl.program_id(1)
    c = GLA_CHUNK

    @pl.when(i == 0)
    def _():
        s_scr[...] = s0_ref[0]

    def one_chunk(d, r0, q_ref, k_ref, v_ref, lr_ref, o_ref):
        tri = tri_ref[d]
        z = _dot(lr_ref[0, pl.ds(r0, c), :].astype(BF16), wa_ref[d]) + ba_ref[d]
        la = _log_sigmoid(z) / GLA_GATE_NORM
        tri_bf = tri.astype(BF16)
        la_hi = la.astype(BF16)
        la_r = la - la_hi.astype(F32)
        la_mid = la_r.astype(BF16)
        la_lo = (la_r - la_mid.astype(F32)).astype(BF16)
        b_all = _dot(tri_bf, la_hi) + _dot(tri_bf, la_mid) + _dot(tri_bf, la_lo)
        q_all = q_ref[0, pl.ds(r0, c), :] * (GLA_DK ** -0.5)
        k_all = k_ref[0, pl.ds(r0, c), :]
        v_all = v_ref[0, pl.ds(r0, c), :].astype(BF16)
        mid = c // 2 - 1 if d == 0 else c // 2
        last = c - 1 if d == 0 else 0
        for h in range(GLA_HEADS):
            ks = slice(h * GLA_DK, (h + 1) * GLA_DK)
            vs = slice(h * GLA_DV, (h + 1) * GLA_DV)
            q, k, v, b = q_all[:, ks], k_all[:, ks], v_all[:, vs], b_all[:, ks]
            b_mid = b[mid:mid + 1]
            qd = (q * jnp.exp(b - b_mid)).astype(BF16)
            kd = (k * jnp.exp(b_mid - b)).astype(BF16)
            sc = lax.dot_general(qd, kd, (((1,), (1,)), ((), ())), preferred_element_type=F32) * tri
            intra = _dot(sc.astype(BF16), v)
            st = s_scr[d, h]
            inter = _dot((q * jnp.exp(b)).astype(BF16), st.astype(BF16))
            o_ref[0, 0, pl.ds(r0, c), vs] = intra + inter
            kt = k.T
            bt = b.T
            bt_last = bt[:, last:last + 1]
            k2t = (kt * jnp.exp(bt_last - bt)).astype(BF16)
            s_scr[d, h] = jnp.exp(bt_last) * st + _dot(k2t, v)

    def chunk(s, carry):
        one_chunk(0, pl.multiple_of(s * c, c), qf_ref, kf_ref, vf_ref, lf_ref, of_ref)
        one_chunk(1, pl.multiple_of((nch - 1 - s) * c, c), qb_ref, kb_ref, vb_ref, lb_ref, ob_ref)
        return carry

    lax.fori_loop(0, nch, chunk, 0)
    sfin_ref[0] = s_scr[...]


def gla_mixer(p3, s0, lp):
    bsz, seq, _ = p3.shape
    tr = min(512, seq)
    nt = seq // tr
    qkw = GLA_HEADS * GLA_DK
    wa_pad = jnp.zeros((2, 128, qkw), F32)
    wa_pad = wa_pad.at[0, 0:16].set(lp["gla_w_alpha"][0]).at[1, 16:32].set(lp["gla_w_alpha"][1]).astype(BF16)
    ba = lp["gla_b_alpha"].reshape(2, 1, qkw)
    lower = np.tril(np.ones((GLA_CHUNK, GLA_CHUNK), np.float32))
    tri = jnp.asarray(np.stack([lower, lower.T]))

    def in_specs(tile):
        return [pl.BlockSpec((1, tr, qkw), lambda b, i: (b, tile(i), 3072 // qkw)),
                pl.BlockSpec((1, tr, qkw), lambda b, i: (b, tile(i), 3584 // qkw)),
                pl.BlockSpec((1, tr, GROUP_W), lambda b, i: (b, tile(i), 4)),
                pl.BlockSpec((1, tr, 128), lambda b, i: (b, tile(i), 72))]

    fwd = lambda i: i
    bwd = lambda i: nt - 1 - i
    o_f, o_b, sfin = pl.pallas_call(
        functools.partial(_gla_body, nch=tr // GLA_CHUNK),
        grid=(bsz, nt),
        in_specs=in_specs(fwd) + in_specs(bwd) + [
            _full_spec((2, 128, qkw)), _full_spec((2, 1, qkw)), _full_spec((2, GLA_CHUNK, GLA_CHUNK)),
            pl.BlockSpec((1, 2, GLA_HEADS, GLA_DK, GLA_DV), lambda b, i: (b, 0, 0, 0, 0))],
        out_specs=[pl.BlockSpec((1, 1, tr, GROUP_W), lambda b, i: (0, b, fwd(i), 0)),
                   pl.BlockSpec((1, 1, tr, GROUP_W), lambda b, i: (0, b, bwd(i), 0)),
                   pl.BlockSpec((1, 2, GLA_HEADS, GLA_DK, GLA_DV), lambda b, i: (b, 0, 0, 0, 0))],
        out_shape=[jax.ShapeDtypeStruct((1, bsz, seq, GROUP_W), F32),
                   jax.ShapeDtypeStruct((1, bsz, seq, GROUP_W), F32),
                   jax.ShapeDtypeStruct((bsz, 2, GLA_HEADS, GLA_DK, GLA_DV), F32)],
        scratch_shapes=[pltpu.VMEM((2, GLA_HEADS, GLA_DK, GLA_DV), F32)],
        compiler_params=_cp("arbitrary", "arbitrary"),
        name="gla_scan",
    )(p3, p3, p3, p3, p3, p3, p3, p3, wa_pad, ba, tri, s0)
    return (o_f, o_b), sfin


def _hy_conv3_body(x_ref, w_ref, b_ref, o_ref, *, seq):
    x = x_ref[0]
    w = w_ref[...]
    row = lax.broadcasted_iota(jnp.int32, (seq, 1), 0)
    xm = jnp.where(row == 0, 0.0, pltpu.roll(x, 1, 0))
    xp = jnp.where(row == seq - 1, 0.0, pltpu.roll(x, seq - 1, 0))
    o_ref[0] = w[0:1] * xm + w[1:2] * x + w[2:3] * xp + b_ref[...]


def hy_conv3(p3, lp):
    bsz, seq, _ = p3.shape
    w3 = 3 * GROUP_W
    return pl.pallas_call(
        functools.partial(_hy_conv3_body, seq=seq),
        grid=(bsz, w3 // 128),
        in_specs=[pl.BlockSpec((1, seq, 128), lambda b, c: (b, 0, 48 + c)),
                  pl.BlockSpec((3, 128), lambda b, c: (0, c)),
                  pl.BlockSpec((1, 128), lambda b, c: (0, c))],
        out_specs=pl.BlockSpec((1, seq, 128), lambda b, c: (b, 0, c)),
        out_shape=jax.ShapeDtypeStruct((bsz, seq, w3), F32),
        compiler_params=_cp("arbitrary", "arbitrary"),
        name="hy_conv3",
    )(p3, lp["hy_conv_w"], lp["hy_conv_b"].reshape(1, w3))


def _hy_filter_body(bv_ref, w1_ref, b1_ref, f0_ref, w2_ref, b2_ref, f1_ref, w3_ref, dl_ref, o_ref, *, tr, seq):
    i = pl.program_id(0)
    n = 2 * seq
    t = lax.broadcasted_iota(jnp.int32, (tr, 1), 0) + i * tr
    pos = jnp.where(t < seq, t, n - t).astype(F32)
    tt = pos / seq
    w = (2.0 * math.pi) * pos / seq
    lane = lax.broadcasted_iota(jnp.int32, (tr, 128), 1)
    arg = w * bv_ref[...]
    feats = jnp.where(lane == 0, tt,
                      jnp.where(lane <= HY_BANDS, jnp.cos(arg),
                                jnp.where(lane <= 2 * HY_BANDS, -jnp.sin(arg), 0.0)))
    h = jnp.sin(f0_ref[...] * (jnp.dot(feats, w1_ref[...], preferred_element_type=F32, precision=HI) + b1_ref[...]))
    h = jnp.sin(f1_ref[...] * (jnp.dot(h, w2_ref[...], preferred_element_type=F32, precision=HI) + b2_ref[...]))
    out = jnp.dot(h, w3_ref[0], preferred_element_type=F32, precision=HI)
    out = out * jnp.exp(-tt * dl_ref[...])
    o_ref[...] = jnp.where(t == seq, 0.0, out)


def hy_filter(lp, seq):
    n = 2 * seq
    tr = min(512, seq)
    nt = n // tr
    bands = np.linspace(1e-4, HY_BANDS - 1, HY_BANDS, dtype=np.float32)
    bv = np.zeros((1, 128), np.float32)
    bv[0, 1:1 + HY_BANDS] = bands
    bv[0, 1 + HY_BANDS:1 + 2 * HY_BANDS] = bands
    deltas = np.abs(np.linspace(math.log(1e-2) / 0.3, math.log(1e-2) / 1.5, GROUP_W, dtype=np.float32))
    dl = np.concatenate([deltas, deltas])[None, :]

    def pad2(w, r, c):
        return jnp.zeros((r, c), F32).at[:w.shape[0], :w.shape[1]].set(w)

    w1 = pad2(lp["hy_w1"], 128, 128)
    b1 = pad2(lp["hy_b1"][None, :], 1, 128)
    f0 = pad2(lp["hy_freq"][0][None, :], 1, 128)
    w2 = pad2(lp["hy_w2"], 128, 128)
    b2 = pad2(lp["hy_b2"][None, :], 1, 128)
    f1 = pad2(lp["hy_freq"][1][None, :], 1, 128)
    w3 = lp["hy_w3"].reshape(HY_FFN, 2, 2 * GROUP_W).transpose(1, 0, 2)
    w3 = jnp.zeros((2, 128, 2 * GROUP_W), F32).at[:, :HY_FFN].set(w3)
    half = nt // 2
    vec = lambda i: (0, 0)
    return pl.pallas_call(
        functools.partial(_hy_filter_body, tr=tr, seq=seq),
        grid=(nt,),
        in_specs=[pl.BlockSpec((1, 128), vec), pl.BlockSpec((128, 128), vec), pl.BlockSpec((1, 128), vec),
                  pl.BlockSpec((1, 128), vec), pl.BlockSpec((128, 128), vec), pl.BlockSpec((1, 128), vec),
                  pl.BlockSpec((1, 128), vec),
                  pl.BlockSpec((1, 128, 2 * GROUP_W), lambda i: (jnp.where(i < half, 0, 1), 0, 0)),
                  pl.BlockSpec((1, 2 * GROUP_W), vec)],
        out_specs=pl.BlockSpec((tr, 2 * GROUP_W), lambda i: (i, 0)),
        out_shape=jax.ShapeDtypeStruct((n, 2 * GROUP_W), F32),
        compiler_params=_cp("arbitrary"),
        name="hy_filter",
    )(jnp.asarray(bv), w1, b1, f0, w2, b2, f1, w3, jnp.asarray(dl))


HY_N1 = 128
HY_N2 = 128


@functools.lru_cache(maxsize=None)
def _dft_tables_t():
    n = HY_N1 * HY_N2
    a = np.arange(128)
    ph = 2.0 * np.pi * ((a[:, None] * a[None, :]) % 128) / 128
    cm, sm = np.cos(ph), np.sin(ph)
    f1 = np.concatenate([cm, -sm], axis=1)
    pht = 2.0 * np.pi * (a[:, None] * a[None, :]) / n
    twr, twi = np.cos(pht), -np.sin(pht)
    f2 = np.block([[cm, -sm], [sm, cm]])
    g2 = np.block([[cm, sm], [-sm, cm]])
    g1 = np.concatenate([cm, -sm], axis=0) / n
    return f1, twr, twi, f2, g2, g1


def _hy_tables_args():
    f1, twr, twi, f2, g2, g1 = _dft_tables_t()
    out = [_np_split(m)[0] for m in (f1, f2, g2, g1)]
    out += [jnp.asarray(twr, F32), jnp.asarray(twi, F32)]
    return out


def _full_spec(shape):
    nd = len(shape)
    return pl.BlockSpec(shape, lambda *_: (0,) * nd)


def _dotr(x, f):
    return _dot(x.astype(BF16), f)


def _hy_fwd_t(z3, f1, f2, twr, twi):
    cb = z3.shape[0]
    x = jnp.swapaxes(z3, 1, 2).reshape(cb * 128, 128)
    a3 = _dotr(x, f1).reshape(cb, 128, 256)
    ar, ai = a3[:, :, 0:128], a3[:, :, 128:256]
    br = ar * twr - ai * twi
    bi = ar * twi + ai * twr
    x2 = jnp.concatenate([jnp.swapaxes(br, 1, 2), jnp.swapaxes(bi, 1, 2)], axis=2)
    return _dotr(x2.reshape(cb * 128, 256), f2)


def _hy_inv_t(y, cb, g2, g1, twr, twi):
    b3 = _dotr(y, g2).reshape(cb, 128, 256)
    br = jnp.swapaxes(b3[:, :, 0:128], 1, 2)
    bi = jnp.swapaxes(b3[:, :, 128:256], 1, 2)
    cr = br * twr + bi * twi
    ci = bi * twr - br * twi
    x4 = jnp.concatenate([cr, ci], axis=2).reshape(cb * 128, 256)
    yv = _dotr(x4, g1).reshape(cb, 128, 128)
    return jnp.swapaxes(yv, 1, 2)


HY_SUB = 8


def _hy_spec_t_body(c_ref, f1, f2, g2, g1, twr, twi, o_ref):
    for s in range(c_ref.shape[0] // HY_SUB):
        cs = slice(s * HY_SUB, (s + 1) * HY_SUB)
        z = _hy_fwd_t(c_ref[cs], f1[...], f2[...], twr[...], twi[...])
        o_ref[cs] = z.reshape(HY_SUB, 128, 256)


def _hy_long_t_body(u_ref, g_ref, bias_ref, h_ref, f1, f2, g2, g1, twr, twi, o_ref):
    t1u = u_ref.shape[2]
    cb = HY_SUB
    for s in range(u_ref.shape[1] // cb):
        cs = slice(s * cb, (s + 1) * cb)
        u = u_ref[0, cs]
        z3 = jnp.concatenate([u, jnp.zeros((cb, HY_N1 - t1u, HY_N2), F32)], axis=1)
        z = _hy_fwd_t(z3, f1[...], f2[...], twr[...], twi[...])
        hs = h_ref[cs].reshape(cb * 128, 256)
        zr, zi, hr, hi = z[:, 0:128], z[:, 128:256], hs[:, 0:128], hs[:, 128:256]
        y = jnp.concatenate([zr * hr - zi * hi, zr * hi + zi * hr], axis=1)
        conv = _hy_inv_t(y, cb, g2[...], g1[...], twr[...], twi[...])[:, 0:t1u, :]
        o_ref[0, cs] = g_ref[0, cs] * (conv + bias_ref[cs] * u)


def hy_conv3_t(p3, lp):
    bsz, seq, _ = p3.shape
    w3 = 3 * GROUP_W

    def body(x_ref, w_ref, b_ref, o_ref):
        x = x_ref[0]
        w = w_ref[...]
        row = lax.broadcasted_iota(jnp.int32, (seq, 1), 0)
        xm = jnp.where(row == 0, 0.0, pltpu.roll(x, 1, 0))
        xp = jnp.where(row == seq - 1, 0.0, pltpu.roll(x, seq - 1, 0))
        o_ref[0] = (w[0:1] * xm + w[1:2] * x + w[2:3] * xp + b_ref[...]).T

    return pl.pallas_call(
        body,
        grid=(bsz, w3 // 128),
        in_specs=[pl.BlockSpec((1, seq, 128), lambda b, c: (b, 0, 48 + c)),
                  pl.BlockSpec((3, 128), lambda b, c: (0, c)),
                  pl.BlockSpec((1, 128), lambda b, c: (0, c))],
        out_specs=pl.BlockSpec((1, 128, seq), lambda b, c: (b, c, 0)),
        out_shape=jax.ShapeDtypeStruct((bsz, w3, seq), F32),
        compiler_params=_cp("arbitrary", "arbitrary"),
        name="hy_conv3_t",
    )(p3, lp["hy_conv_w"], lp["hy_conv_b"].reshape(1, w3))


def _hy_filter_t_body(bc_ref, w1_ref, b1_ref, f0_ref, w2_ref, b2_ref, f1_ref, w3_ref, dl_ref, o_ref, *, tr, seq):
    i = pl.program_id(0)
    n = 2 * seq
    t = lax.broadcasted_iota(jnp.int32, (1, tr), 1) + i * tr
    pos = jnp.where(t < seq, t, n - t).astype(F32)
    tt = pos / seq
    w = (2.0 * math.pi) * pos / seq
    row = lax.broadcasted_iota(jnp.int32, (128, tr), 0)
    arg = bc_ref[...] * w
    feats = jnp.where(row == 0, tt,
                      jnp.where(row <= HY_BANDS, jnp.cos(arg),
                                jnp.where(row <= 2 * HY_BANDS, -jnp.sin(arg), 0.0)))
    h = jnp.sin(f0_ref[...] * (jnp.dot(w1_ref[...], feats, preferred_element_type=F32, precision=HI) + b1_ref[...]))
    h = jnp.sin(f1_ref[...] * (jnp.dot(w2_ref[...], h, preferred_element_type=F32, precision=HI) + b2_ref[...]))
    out = _dot(w3_ref[0], h.astype(BF16))
    out = out * jnp.exp(-dl_ref[...] * tt)
    o_ref[...] = jnp.where(t == seq, 0.0, out)


def hy_filter_t(lp, seq):
    n = 2 * seq
    tr = 512
    nt = n // tr
    bands = np.linspace(1e-4, HY_BANDS - 1, HY_BANDS, dtype=np.float32)
    bc = np.zeros((128, 1), np.float32)
    bc[1:1 + HY_BANDS, 0] = bands
    bc[1 + HY_BANDS:1 + 2 * HY_BANDS, 0] = bands
    deltas = np.abs(np.linspace(math.log(1e-2) / 0.3, math.log(1e-2) / 1.5, GROUP_W, dtype=np.float32))
    dl = np.concatenate([deltas, deltas])[:, None]

    def pad2(w, r, c):
        return jnp.zeros((r, c), F32).at[:w.shape[0], :w.shape[1]].set(w)

    w1 = pad2(lp["hy_w1"].T, 128, 128)
    b1 = pad2(lp["hy_b1"][:, None], 128, 1)
    f0 = pad2(lp["hy_freq"][0][:, None], 128, 1)
    w2 = pad2(lp["hy_w2"].T, 128, 128)
    b2 = pad2(lp["hy_b2"][:, None], 128, 1)
    f1 = pad2(lp["hy_freq"][1][:, None], 128, 1)
    w3 = lp["hy_w3"].reshape(HY_FFN, 2, 2 * GROUP_W).transpose(1, 2, 0)
    w3 = jnp.zeros((2, 2 * GROUP_W, 128), F32).at[:, :, :HY_FFN].set(w3).astype(BF16)
    half = nt // 2
    return pl.pallas_call(
        functools.partial(_hy_filter_t_body, tr=tr, seq=seq),
        grid=(nt,),
        in_specs=[_full_spec((128, 1)), _full_spec((128, 128)), _full_spec((128, 1)), _full_spec((128, 1)),
                  _full_spec((128, 128)), _full_spec((128, 1)), _full_spec((128, 1)),
                  pl.BlockSpec((1, 2 * GROUP_W, 128), lambda i: (jnp.where(i < half, 0, 1), 0, 0)),
                  _full_spec((2 * GROUP_W, 1))],
        out_specs=pl.BlockSpec((2 * GROUP_W, tr), lambda i: (0, i)),
        out_shape=jax.ShapeDtypeStruct((2 * GROUP_W, n), F32),
        compiler_params=_cp("arbitrary"),
        name="hy_filter_t",
    )(jnp.asarray(bc), w1, b1, f0, w2, b2, f1, w3, jnp.asarray(dl))


HY_CB = 64


def hyena_long(p3, lp):
    bsz, seq, _ = p3.shape
    t1u = seq // HY_N2
    tabs = _hy_tables_args()
    tab_specs = [_full_spec(t.shape) for t in tabs]
    hzt = hy_conv3_t(p3, lp).reshape(bsz, 3 * GROUP_W, t1u, HY_N2)
    circ = hy_filter_t(lp, seq).reshape(2 * GROUP_W, HY_N1, HY_N2)
    cb = HY_CB
    hspec = pl.pallas_call(
        _hy_spec_t_body,
        grid=(2 * GROUP_W // cb,),
        in_specs=[pl.BlockSpec((cb, HY_N1, HY_N2), lambda c: (c, 0, 0))] + tab_specs,
        out_specs=pl.BlockSpec((cb, HY_N1, 2 * HY_N2), lambda c: (c, 0, 0)),
        out_shape=jax.ShapeDtypeStruct((2 * GROUP_W, HY_N1, 2 * HY_N2), F32),
        compiler_params=_cp("arbitrary"),
        name="hy_spec_t",
    )(circ, *tabs)
    ncb = GROUP_W // cb
    u, ub0 = hzt, 0
    for o in range(2):
        gb0 = (1 + o) * ncb
        u = pl.pallas_call(
            _hy_long_t_body,
            grid=(ncb, bsz),
            in_specs=[pl.BlockSpec((1, cb, t1u, HY_N2), lambda c, b, ub0=ub0: (b, ub0 + c, 0, 0)),
                      pl.BlockSpec((1, cb, t1u, HY_N2), lambda c, b, gb0=gb0: (b, gb0 + c, 0, 0)),
                      pl.BlockSpec((cb, 1, 1), lambda c, b: (c, 0, 0)),
                      pl.BlockSpec((cb, HY_N1, 2 * HY_N2), lambda c, b, o=o: (o * ncb + c, 0, 0))] + tab_specs,
            out_specs=pl.BlockSpec((1, cb, t1u, HY_N2), lambda c, b: (b, c, 0, 0)),
            out_shape=jax.ShapeDtypeStruct((bsz, GROUP_W, t1u, HY_N2), F32),
            compiler_params=_cp("arbitrary", "arbitrary"),
            name="hy_long_t",
        )(u, hzt, lp["hy_bias"][o].reshape(GROUP_W, 1, 1), hspec, *tabs)
        ub0 = 0
    return u.reshape(bsz, GROUP_W, seq)


@functools.lru_cache(maxsize=None)
def _dense_dft_tables(seq):
    n = 2 * seq
    k = np.arange(n)[:, None]
    t = np.arange(n)[None, :]
    ph = 2.0 * np.pi * ((k * t) % n) / n
    fwd = np.concatenate([np.cos(ph), -np.sin(ph)], axis=0)
    inv = np.concatenate([np.cos(ph).T, -np.sin(ph).T], axis=1)[:seq] / n
    return fwd, inv


def _hy_dense_spec_body(c_ref, fh_ref, fl_ref, o_ref):
    o_ref[...] = _dot3(fh_ref[...], fl_ref[...], c_ref[...])


def _hy_dense_body(z_ref, h_ref, fh_ref, fl_ref, gh_ref, gl_ref, g_ref, bias_ref, o_ref, *, n):
    u = z_ref[0]
    z = _dot3(fh_ref[...], fl_ref[...], u)
    zr, zi = z[0:n], z[n:]
    hr, hi = h_ref[0:n], h_ref[n:2 * n]
    y = jnp.concatenate([zr * hr - zi * hi, zr * hi + zi * hr], axis=0)
    conv = _dot3(gh_ref[...], gl_ref[...], y)
    o_ref[0] = g_ref[0] * (conv + bias_ref[...] * u)


def hy_dense(hzc, circ, lp, cb=512):
    bsz, seq, _ = hzc.shape
    n = 2 * seq
    fwd, inv = _dense_dft_tables(seq)
    fh, fl = _np_split(fwd)
    gh, gl = _np_split(inv)
    hspec = pl.pallas_call(
        _hy_dense_spec_body,
        grid=(2 * GROUP_W // cb,),
        in_specs=[pl.BlockSpec((n, cb), lambda c: (0, c)),
                  pl.BlockSpec((2 * n, n), lambda c: (0, 0)),
                  pl.BlockSpec((2 * n, n), lambda c: (0, 0))],
        out_specs=pl.BlockSpec((2 * n, cb), lambda c: (0, c)),
        out_shape=jax.ShapeDtypeStruct((2 * n, 2 * GROUP_W), F32),
        compiler_params=_cp("arbitrary"),
        name="hy_dense_spec",
    )(circ, fh, fl)
    fh_in, fl_in = fh[:, :seq], fl[:, :seq]
    u, uc0 = hzc, 0
    ncb = GROUP_W // cb
    for o in range(2):
        gc0 = (1 + o) * ncb
        u = pl.pallas_call(
            functools.partial(_hy_dense_body, n=n),
            grid=(bsz, ncb),
            in_specs=[pl.BlockSpec((1, seq, cb), lambda b, c, uc0=uc0: (b, 0, uc0 + c)),
                      pl.BlockSpec((2 * n, cb), lambda b, c, o=o: (0, o * ncb + c)),
                      pl.BlockSpec((2 * n, seq), lambda b, c: (0, 0)),
                      pl.BlockSpec((2 * n, seq), lambda b, c: (0, 0)),
                      pl.BlockSpec((seq, 2 * n), lambda b, c: (0, 0)),
                      pl.BlockSpec((seq, 2 * n), lambda b, c: (0, 0)),
                      pl.BlockSpec((1, seq, cb), lambda b, c, gc0=gc0: (b, 0, gc0 + c)),
                      pl.BlockSpec((1, cb), lambda b, c: (0, c))],
            out_specs=pl.BlockSpec((1, seq, cb), lambda b, c: (b, 0, c)),
            out_shape=jax.ShapeDtypeStruct((bsz, seq, GROUP_W), F32),
            compiler_params=_cp("arbitrary", "arbitrary"),
            name="hy_dense_conv",
        )(u, hspec, fh_in, fl_in, gh, gl, hzc, lp["hy_bias"][o].reshape(1, GROUP_W))
        uc0 = 0
    return u


def hyena_mixer(p3, lp):
    bsz, seq, _ = p3.shape
    if 2 * seq == HY_N1 * HY_N2:
        return hyena_long(p3, lp), True
    hzc = hy_conv3(p3, lp)
    circ = hy_filter(lp, seq)
    return hy_dense(hzc, circ, lp), False


def _rms(y, g):
    return y * lax.rsqrt(jnp.mean(y * y, axis=-1, keepdims=True) + EPS) * g


def _mix_body(s5_ref, hf_ref, hb_ref, lg_ref, of_ref, ob_ref, gg_ref, hy_ref, gn_ref, mg_ref, o_ref, *, hy_t):
    w = GROUP_W
    y_hy = hy_ref[0].T if hy_t else hy_ref[...]
    o_ref[:, 0:w] = _rms(s5_ref[...], mg_ref[:, 0:w]).astype(o_ref.dtype)
    y_lru = (hf_ref[0] + hb_ref[0]) * jax.nn.gelu(lg_ref[...])
    o_ref[:, w:2 * w] = _rms(y_lru, mg_ref[:, w:2 * w]).astype(o_ref.dtype)
    o = of_ref[0] + ob_ref[0]
    gg = gg_ref[...]
    heads = []
    for h in range(GLA_HEADS):
        sl = slice(h * GLA_DV, (h + 1) * GLA_DV)
        heads.append(_rms(o[:, sl], gn_ref[...]) * (gg[:, sl] * jax.nn.sigmoid(gg[:, sl])))
    y_gla = jnp.concatenate(heads, axis=-1)
    o_ref[:, 2 * w:3 * w] = _rms(y_gla, mg_ref[:, 2 * w:3 * w]).astype(o_ref.dtype)
    o_ref[:, 3 * w:4 * w] = _rms(y_hy, mg_ref[:, 3 * w:4 * w]).astype(o_ref.dtype)


def mix_assemble(p2, y_s5, h_lru, o_gla, y_hy, hy_t, seq, lp, tm=256):
    m = p2.shape[0]
    w = GROUP_W
    row = lambda i: (i, 0)
    tpb = seq // tm
    if hy_t:
        hy_spec = pl.BlockSpec((1, w, tm), lambda i: (i // tpb, 0, i % tpb))
    else:
        hy_spec = pl.BlockSpec((tm, w), row)
        y_hy = y_hy.reshape(m, w)
    return pl.pallas_call(
        functools.partial(_mix_body, hy_t=hy_t),
        grid=(m // tm,),
        in_specs=[pl.BlockSpec((tm, w), row),
                  pl.BlockSpec((1, tm, w), lambda i: (0, i, 0)),
                  pl.BlockSpec((1, tm, w), lambda i: (1, i, 0)),
                  pl.BlockSpec((tm, w), lambda i: (i, 2)),
                  pl.BlockSpec((1, tm, w), lambda i: (0, i, 0)),
                  pl.BlockSpec((1, tm, w), lambda i: (0, i, 0)),
                  pl.BlockSpec((tm, w), lambda i: (i, 5)),
                  hy_spec,
                  pl.BlockSpec((1, GLA_DV), lambda i: (0, 0)),
                  pl.BlockSpec((1, 4 * w), lambda i: (0, 0))],
        out_specs=pl.BlockSpec((tm, 4 * w), row),
        out_shape=jax.ShapeDtypeStruct((m, 4 * w), BF16),
        compiler_params=_cp("arbitrary"),
        name="mix_assemble",
    )(y_s5.reshape(m, w), h_lru.reshape(2, m, w), h_lru.reshape(2, m, w), p2,
      o_gla[0].reshape(1, m, w), o_gla[1].reshape(1, m, w), p2, y_hy,
      lp["gla_norm_g"].reshape(1, GLA_DV), lp["mix_norm_g"].reshape(1, 4 * w))


FFN_LAG = 2
FFN_SUB_ROWS = 128


def _ffn1_body(h_ref, wg_ref, wu_ref, cw_ref, cb_ref, o_ref, g_scr, u_scr, wgb_scr, wub_scr, *, tm, gw, tps, nt):
    s = pl.program_id(1)

    @pl.when(s == 0)
    def _():
        g_scr[...] = jnp.zeros_like(g_scr)
        u_scr[...] = jnp.zeros_like(u_scr)
        wgb_scr[...] = wg_ref[...].astype(BF16)
        wub_scr[...] = wu_ref[...].astype(BF16)

    t = s - FFN_LAG
    cur = s % 3
    mid = (s + 1) % 3
    nxt = (s + 2) % 3
    g_cur, g_mid, g_nxt = g_scr.at[cur], g_scr.at[mid], g_scr.at[nxt]
    u_cur, u_mid = u_scr.at[cur], u_scr.at[mid]

    rb = min(FFN_SUB_ROWS, tm)
    nsub = tm // rb
    gpt = tm // gw
    tn = o_ref.shape[1]
    col = lax.broadcasted_iota(jnp.int32, (gw, 1), 0)
    first_col = col == 0
    last_col = col == gw - 1

    def grid_row(rho, ls):
        if rho < 0:
            return jnp.where(t % tps == 0, 0.0, g_cur[tm - gw:tm, ls])
        if rho >= gpt:
            return jnp.where(t % tps == tps - 1, 0.0, g_nxt[0:gw, ls])
        return g_mid[rho * gw:(rho + 1) * gw, ls]

    never = s < 0
    chain = [None]

    def conv_finish(r):
        lo, hi = r * rb // gw, (r + 1) * rb // gw
        for lh in range(tn // 128):
            ls = slice(lh * 128, (lh + 1) * 128)
            accs = {}
            for sig in range(lo - 1, hi + 1):
                src = grid_row(sig, ls)
                left = jnp.where(first_col, 0.0, pltpu.roll(src, 1, 0))
                right = jnp.where(last_col, 0.0, pltpu.roll(src, gw - 1, 0))
                for dr in range(3):
                    rho = sig + 1 - dr
                    if not lo <= rho < hi:
                        continue
                    if rho not in accs:
                        accs[rho] = jnp.broadcast_to(cb_ref[:, ls], (gw, 128))
                        if chain[0] is not None:
                            accs[rho] = jnp.where(never, chain[0], accs[rho])
                    accs[rho] = (accs[rho] + cw_ref[3 * dr:3 * dr + 1, ls] * left
                                 + cw_ref[3 * dr + 1:3 * dr + 2, ls] * src + cw_ref[3 * dr + 2:3 * dr + 3, ls] * right)
                done = sig - 1
                if done in accs:
                    acc = accs.pop(done)
                    rows = slice(done * gw, (done + 1) * gw)
                    o_ref[rows, ls] = (acc * jax.nn.sigmoid(acc) * u_mid[rows, ls]).astype(o_ref.dtype)
                    chain[0] = acc

    @pl.when(s < nt)
    def _():
        chain[0] = None
        for r in range(nsub):
            conv_finish(r)
            h = h_ref[r * rb:(r + 1) * rb, :]
            g_cur[r * rb:(r + 1) * rb] = _dot(h, wgb_scr[...])
            u_cur[r * rb:(r + 1) * rb] = _dot(h, wub_scr[...])

    @pl.when(s >= nt)
    def _():
        chain[0] = None
        for r in range(nsub):
            conv_finish(r)


def _ffn1_one_row_body(h_ref, wg_ref, wu_ref, cw_ref, cb_ref, o_ref, g_scr, u_scr, wgb_scr, wub_scr, *, gw):
    @pl.when(pl.program_id(1) == 0)
    def _():
        wgb_scr[...] = wg_ref[...].astype(BF16)
        wub_scr[...] = wu_ref[...].astype(BF16)

    h = h_ref[...]
    g_scr[...] = _dot(h, wgb_scr[...])
    u_scr[...] = _dot(h, wub_scr[...])
    col = lax.broadcasted_iota(jnp.int32, (gw, 1), 0)
    for lh in range(o_ref.shape[1] // 128):
        ls = slice(lh * 128, (lh + 1) * 128)
        src = g_scr[:, ls]
        left = jnp.where(col == 0, 0.0, pltpu.roll(src, 1, 0))
        right = jnp.where(col == gw - 1, 0.0, pltpu.roll(src, gw - 1, 0))
        acc = cb_ref[:, ls] + cw_ref[3:4, ls] * left + cw_ref[4:5, ls] * src + cw_ref[5:6, ls] * right
        o_ref[:, ls] = (acc * jax.nn.sigmoid(acc) * u_scr[:, ls]).astype(o_ref.dtype)


def ffn1(h2, wg, wu, cw, cb, layer, *, seq, gw, tm, tn=256):
    m, d = h2.shape
    ff = wg.shape[-1]
    tps = seq // tm
    nt = m // tm
    w_specs = [pl.BlockSpec((None, d, tn), lambda j, s: (layer, 0, j)),
               pl.BlockSpec((None, d, tn), lambda j, s: (layer, 0, j)),
               pl.BlockSpec((None, 9, tn), lambda j, s: (layer, 0, j)),
               pl.BlockSpec((None, 1, tn), lambda j, s: (layer, 0, j))]
    if tps == 1 and tm == gw:
        return pl.pallas_call(
            functools.partial(_ffn1_one_row_body, gw=gw),
            grid=(ff // tn, nt),
            in_specs=[pl.BlockSpec((tm, d), lambda j, s: (s, 0))] + w_specs,
            out_specs=pl.BlockSpec((tm, tn), lambda j, s: (s, j)),
            out_shape=jax.ShapeDtypeStruct((m, ff), BF16),
            scratch_shapes=[pltpu.VMEM((tm, tn), F32), pltpu.VMEM((tm, tn), F32),
                            pltpu.VMEM((d, tn), BF16), pltpu.VMEM((d, tn), BF16)],
            compiler_params=_cp("arbitrary", "arbitrary"),
            name="ffn_gate_up_one_row",
        )(h2, wg, wu, cw, cb)
    return pl.pallas_call(
        functools.partial(_ffn1_body, tm=tm, gw=gw, tps=tps, nt=nt),
        grid=(ff // tn, nt + FFN_LAG),
        in_specs=[pl.BlockSpec((tm, d), lambda j, s: (jnp.minimum(s, nt - 1), 0)),
                  pl.BlockSpec((None, d, tn), lambda j, s: (layer, 0, j)),
                  pl.BlockSpec((None, d, tn), lambda j, s: (layer, 0, j)),
                  pl.BlockSpec((None, 9, tn), lambda j, s: (layer, 0, j)),
                  pl.BlockSpec((None, 1, tn), lambda j, s: (layer, 0, j))],
        out_specs=pl.BlockSpec((tm, tn), lambda j, s: (jnp.maximum(s - FFN_LAG, 0), j)),
        out_shape=jax.ShapeDtypeStruct((m, ff), BF16),
        scratch_shapes=[pltpu.VMEM((3, tm, tn), F32), pltpu.VMEM((3, tm, tn), F32),
                        pltpu.VMEM((d, tn), BF16), pltpu.VMEM((d, tn), BF16)],
        compiler_params=_cp("arbitrary", "arbitrary"),
        name="ffn_gate_up",
    )(h2, wg, wu, cw, cb)


def _token_mixers(p2, bsz, seq, states, lp, wglu_bf, layer, need_output):
    p3 = p2.reshape(bsz, seq, N_COL_PAD)
    y_s5, st_s5 = s5_mixer(p3, states[0], lp, wglu_bf, layer, need_output)
    h_lru, st_lru = lru_mixer(p3, states[1], lp)
    o_gla, st_gla = gla_mixer(p3, states[2], lp)
    new_states = (st_s5, st_lru, st_gla)
    if not need_output:
        return None, new_states
    y_hy, hy_t = hyena_mixer(p3, lp)
    return mix_assemble(p2, y_s5, h_lru, o_gla, y_hy, hy_t, seq, lp), new_states


def kernel(x, c, ctx, c_ctx, w_ada, b_ada, norm_mix_g, norm_mlp_g, w_in, s5_lam_re, s5_lam_im, s5_log_step, s5_b_re, s5_b_im, s5_c_re, s5_c_im, s5_d, s5_w_glu, s5_b_glu, lru_conv_w, lru_conv_b, lru_w_a, lru_b_a, lru_w_x, lru_b_x, lru_lam, gla_w_alpha, gla_b_alpha, gla_norm_g, hy_conv_w, hy_conv_b, hy_w1, hy_b1, hy_w2, hy_b2, hy_w3, hy_freq, hy_bias, mix_norm_g, w_out, mlp_w_gate, mlp_w_up, mlp_conv_w, mlp_conv_b, mlp_w_down, final_norm_g):
    bsz, seq, d = x.shape
    clen = ctx.shape[1]
    depth = w_ada.shape[0]
    grid_w = 64
    params = dict(
        s5_lam_re=s5_lam_re, s5_lam_im=s5_lam_im, s5_log_step=s5_log_step, s5_b_re=s5_b_re, s5_b_im=s5_b_im,
        s5_c_re=s5_c_re, s5_c_im=s5_c_im, s5_d=s5_d, s5_b_glu=s5_b_glu,
        lru_conv_w=lru_conv_w, lru_conv_b=lru_conv_b, lru_w_a=lru_w_a, lru_b_a=lru_b_a, lru_w_x=lru_w_x,
        lru_b_x=lru_b_x, lru_lam=lru_lam, gla_w_alpha=gla_w_alpha, gla_b_alpha=gla_b_alpha, gla_norm_g=gla_norm_g,
        hy_conv_w=hy_conv_w, hy_conv_b=hy_conv_b, hy_w1=hy_w1, hy_b1=hy_b1, hy_w2=hy_w2, hy_b2=hy_b2, hy_w3=hy_w3,
        hy_freq=hy_freq, hy_bias=hy_bias, mix_norm_g=mix_norm_g)

    w_in_bf = jnp.concatenate(
        [w_in[..., 0:6144], w_in[..., 6176:9248], w_in[..., 6144:6176],
         jnp.zeros((depth, d, N_COL_PAD - 9248), w_in.dtype)], axis=-1).astype(BF16)
    w_out_bf = w_out.astype(BF16)
    wd_bf = mlp_w_down.astype(BF16)
    wglu_bf = s5_w_glu.astype(BF16)
    conv_w9 = mlp_conv_w.reshape(depth, 9, D_FF)
    conv_b = mlp_conv_b.reshape(depth, 1, D_FF)

    cvec = jnp.zeros((8, d), F32).at[0:bsz].set(c).at[bsz].set(c_ctx)
    mod = ada_mod(cvec, w_ada, b_ada)

    x2 = x.reshape(bsz * seq, d)
    c2 = ctx.reshape(bsz * clen, d)
    zero_states = (jnp.zeros((bsz, 2, S5_JB, 1, 1024), F32),
                   jnp.zeros((bsz, 2, 1, GROUP_W), F32),
                   jnp.zeros((bsz, 2, GLA_HEADS, GLA_DK, GLA_DV), F32))

    for l in range(depth):
        last = l == depth - 1
        lp = {k: v[l] for k, v in params.items()}
        mx = mod[l, 0:bsz].reshape(bsz, 1, 6, d)
        mc = mod[l, bsz:bsz + 1].reshape(1, 1, 6, d)
        sh1, sc1, g1, sh2, sc2, g2 = (mx[:, :, i] for i in range(6))
        csh1, csc1, cg1, csh2, csc2, cg2 = (mc[:, :, i] for i in range(6))

        hc = normmod(c2, norm_mix_g[l], csh1, csc1, bsz * clen, BF16)
        pc = matmul(hc, w_in_bf, l, tm=bsz * clen, tn=512, name="in_proj_ctx")
        yc, ctx_states = _token_mixers(pc, bsz, clen, zero_states, lp, wglu_bf, l, need_output=not last)

        hx = normmod(x2, norm_mix_g[l], sh1, sc1, seq, BF16)
        px = matmul(hx, w_in_bf, l, tm=2048, tn=512, name="in_proj")
        yx, _ = _token_mixers(px, bsz, seq, ctx_states, lp, wglu_bf, l, need_output=True)
        x2 = matmul(yx, w_out_bf, l, tm=1024, tn=512, res=x2, gate=g1, rows_per_gate=seq, name="out_proj")
        h2 = normmod(x2, norm_mlp_g[l], sh2, sc2, seq, BF16)
        act = ffn1(h2, mlp_w_gate, mlp_w_up, conv_w9, conv_b, l, seq=seq, gw=grid_w, tm=1024)
        x2 = matmul(act, wd_bf, l, tm=512, tn=512, res=x2, gate=g2, rows_per_gate=seq, name="down_proj")

        if not last:
            c2 = matmul(yc, w_out_bf, l, tm=bsz * clen, tn=512, res=c2, gate=cg1, rows_per_gate=bsz * clen,
                        name="out_proj_ctx")
            hc2 = normmod(c2, norm_mlp_g[l], csh2, csc2, bsz * clen, BF16)
            actc = ffn1(hc2, mlp_w_gate, mlp_w_up, conv_w9, conv_b, l, seq=clen, gw=clen, tm=clen)
            c2 = matmul(actc, wd_bf, l, tm=bsz * clen, tn=512, res=c2, gate=cg2, rows_per_gate=bsz * clen,
                        name="down_proj_ctx")

    zeros = jnp.zeros((1, 1, d), F32)
    out = normmod(x2, final_norm_g, zeros, zeros, bsz * seq, F32)
    return out.reshape(bsz, seq, d)
```

```python
import functools
import math

import numpy as np
import jax
import jax.numpy as jnp
from jax import lax
from jax.experimental import pallas as pl
from jax.experimental.pallas import tpu as pltpu

F32 = jnp.float32
BF16 = jnp.bfloat16
HI = lax.Precision.HIGHEST

EPS = 1e-6
GROUP_W = 1024
N_COL_PAD = 9728
S5_GROUPS = 64
S5_CH = 16
S5_STATE = 64
S5_SEG = 8
S5_JB = 8
LRU_HEADS = 16
LRU_C = 8.0
GLA_HEADS = 4
GLA_DK = 128
GLA_DV = 256
GLA_CHUNK = 64
GLA_GATE_NORM = 16.0
HY_BANDS = 16
HY_FFN = 64
D_FF = 11008
V7X_VMEM_LIMIT = 56 * 1024 * 1024


def _cp(*sem):
    return pltpu.CompilerParams(dimension_semantics=sem, vmem_limit_bytes=V7X_VMEM_LIMIT)


def _dot(a, b):
    return jnp.dot(a, b, preferred_element_type=F32)


def _split(x):
    hi = x.astype(BF16)
    lo = (x - hi.astype(F32)).astype(BF16)
    return hi, lo


def _dot3(fh, fl, x):
    xh, xl = _split(x)
    return _dot(fh, xh) + _dot(fh, xl) + _dot(fl, xh)


def _np_split(a):
    bf = jnp.dtype(BF16)
    hi = np.asarray(a, np.float64).astype(bf)
    lo = (np.asarray(a, np.float64) - hi.astype(np.float64)).astype(bf)
    return jnp.asarray(hi), jnp.asarray(lo)


def _ada_body(c_ref, w_ref, b_ref, o_ref):
    c = c_ref[...]
    a = (c * jax.nn.sigmoid(c)).astype(BF16)
    o_ref[...] = _dot(a, w_ref[...].astype(BF16)) + b_ref[...]


def ada_mod(cvec, w_ada, b_ada, tn=1024):
    depth, d, n = w_ada.shape
    return pl.pallas_call(
        _ada_body,
        grid=(depth, n // tn),
        in_specs=[pl.BlockSpec((8, d), lambda l, j: (0, 0)),
                  pl.BlockSpec((None, d, tn), lambda l, j: (l, 0, j)),
                  pl.BlockSpec((None, 1, tn), lambda l, j: (l, 0, j))],
        out_specs=pl.BlockSpec((None, 8, tn), lambda l, j: (l, 0, j)),
        out_shape=jax.ShapeDtypeStruct((depth, 8, n), F32),
        compiler_params=_cp("arbitrary", "arbitrary"),
        name="ada_mod",
    )(cvec, w_ada, b_ada.reshape(depth, 1, n))


def _normmod_body(x_ref, g_ref, sh_ref, sc_ref, o_ref):
    x = x_ref[...]
    y = x * lax.rsqrt(jnp.mean(x * x, axis=-1, keepdims=True) + EPS) * g_ref[...]
    o_ref[...] = (y * (1.0 + sc_ref[0]) + sh_ref[0]).astype(o_ref.dtype)


def normmod(x2d, g, sh, sc, rows_per_mod, out_dtype, tm=256):
    m, d = x2d.shape
    tpm = rows_per_mod // tm
    return pl.pallas_call(
        _normmod_body,
        grid=(m // tm,),
        in_specs=[pl.BlockSpec((tm, d), lambda i: (i, 0)),
                  pl.BlockSpec((1, d), lambda i: (0, 0)),
                  pl.BlockSpec((1, 1, d), lambda i: (i // tpm, 0, 0)),
                  pl.BlockSpec((1, 1, d), lambda i: (i // tpm, 0, 0))],
        out_specs=pl.BlockSpec((tm, d), lambda i: (i, 0)),
        out_shape=jax.ShapeDtypeStruct((m, d), out_dtype),
        compiler_params=_cp("arbitrary"),
        name="normmod",
    )(x2d, g.reshape(1, d), sh, sc)


def _mm_body(*refs, nk, has_res):
    if has_res:
        a_ref, w_ref, res_ref, gate_ref, o_ref = refs[:5]
        scr = refs[5:]
    else:
        a_ref, w_ref, o_ref = refs[:3]
        scr = refs[3:]

    def epilogue(acc):
        if has_res:
            o_ref[...] = res_ref[...] + gate_ref[0] * acc
        else:
            o_ref[...] = acc.astype(o_ref.dtype)

    if nk == 1:
        epilogue(_dot(a_ref[...], w_ref[...]))
    else:
        acc_ref = scr[0]
        k = pl.program_id(2)

        @pl.when(k == 0)
        def _():
            acc_ref[...] = jnp.zeros_like(acc_ref)

        acc_ref[...] += _dot(a_ref[...], w_ref[...])

        @pl.when(k == nk - 1)
        def _():
            epilogue(acc_ref[...])


def matmul(a, w, layer, *, tm, tn, tk=None, res=None, gate=None, rows_per_gate=None, name="matmul"):
    m, kdim = a.shape
    n = w.shape[-1]
    tk = kdim if tk is None else tk
    nk = kdim // tk
    has_res = res is not None
    in_specs = [pl.BlockSpec((tm, tk), lambda i, j, k: (i, k)),
                pl.BlockSpec((None, tk, tn), lambda i, j, k: (layer, k, j))]
    args = [a, w]
    if has_res:
        tpg = rows_per_gate // tm
        in_specs += [pl.BlockSpec((tm, tn), lambda i, j, k: (i, j)),
                     pl.BlockSpec((1, 1, tn), lambda i, j, k: (i // tpg, 0, j))]
        args += [res, gate]
    return pl.pallas_call(
        functools.partial(_mm_body, nk=nk, has_res=has_res),
        grid=(m // tm, n // tn, nk),
        in_specs=in_specs,
        out_specs=pl.BlockSpec((tm, tn), lambda i, j, k: (i, j)),
        out_shape=jax.ShapeDtypeStruct((m, n), F32),
        scratch_shapes=[pltpu.VMEM((tm, tn), F32)] if nk > 1 else [],
        compiler_params=_cp("arbitrary", "arbitrary", "arbitrary"),
        name=name,
    )(*args)


def _s5_params(lp, lseg):
    lam_re, lam_im = lp["s5_lam_re"], lp["s5_lam_im"]
    step = jnp.exp(lp["s5_log_step"])[:, :, None]
    mag = jnp.exp(lam_re * step)
    ab_re = mag * jnp.cos(lam_im * step)
    ab_im = mag * jnp.sin(lam_im * step)
    den = lam_re * lam_re + lam_im * lam_im
    co_re = ((ab_re - 1.0) * lam_re + ab_im * lam_im) / den
    co_im = (ab_im * lam_re - (ab_re - 1.0) * lam_im) / den
    b_re, b_im = lp["s5_b_re"], lp["s5_b_im"]
    bb_re = co_re[..., None] * b_re - co_im[..., None] * b_im
    bb_im = co_re[..., None] * b_im + co_im[..., None] * b_re
    eye = jnp.eye(8, dtype=F32)

    def in_blocks(bb):
        t = bb.reshape(2, S5_JB, 8, S5_STATE, S5_CH)
        t = jnp.einsum("djgpc,gh->djgchp", t, eye)
        return t.reshape(2, S5_JB, 8 * S5_CH, 8 * S5_STATE)

    def out_blocks(cc):
        t = cc.reshape(2, S5_JB, 8, S5_CH, S5_STATE)
        t = jnp.einsum("djgcp,gh->djgphc", t, eye)
        return t.reshape(2, S5_JB, 8 * S5_STATE, 8 * S5_CH)

    wb = jnp.concatenate([in_blocks(bb_re), in_blocks(bb_im)], axis=-1).astype(BF16)
    wc = jnp.concatenate([out_blocks(lp["s5_c_re"]), -out_blocks(lp["s5_c_im"])], axis=-2).astype(BF16)

    def lanes(t):
        return t.reshape(2, S5_JB, 1, 8 * S5_STATE)

    a = jnp.concatenate([lanes(ab_re), lanes(ab_im)], axis=-1)
    pr, pi = ab_re, ab_im
    for _ in range(int(round(math.log2(lseg)))):
        pr, pi = pr * pr - pi * pi, 2.0 * pr * pi
    al = jnp.concatenate([lanes(pr), lanes(pi)], axis=-1)
    return wb, wc, a, al


S5_JP = 2


def _s5_chains(bsz):
    return [(b, jj) for b in range(bsz) for jj in range(S5_JP)]


def _s5_project_in(u_ref, wb_ref, a_ref, bu_scr, ab_scr, bsz):
    for ch, (b, jj) in enumerate(_s5_chains(bsz)):
        bu_scr[ch] = _dot(u_ref[b, :, jj * 128:(jj + 1) * 128].astype(BF16), wb_ref[0, jj])
    for jj in range(S5_JP):
        ab_scr[jj] = jnp.broadcast_to(a_ref[0, jj], (S5_SEG, 1024))


def _s5_scan_tile(d, t_steps, bu_scr, ab_scr, h_scr, store, bsz):
    chains = _s5_chains(bsz)

    def step(s, carry):
        row = jnp.where(d == 0, s, t_steps - 1 - s)
        off = pl.multiple_of(row * S5_SEG, S5_SEG)
        out = []
        for ch, (_, jj) in enumerate(chains):
            hr, hi = carry[2 * ch], carry[2 * ch + 1]
            ar = ab_scr[jj, :, 0:512]
            ai = ab_scr[jj, :, 512:1024]
            nr = ar * hr - ai * hi + bu_scr[ch, pl.ds(off, S5_SEG), 0:512]
            ni = ar * hi + ai * hr + bu_scr[ch, pl.ds(off, S5_SEG), 512:1024]
            if store:
                bu_scr[ch, pl.ds(off, S5_SEG), 0:512] = nr
                bu_scr[ch, pl.ds(off, S5_SEG), 512:1024] = ni
            out += [nr, ni]
        return tuple(out)

    init = tuple(h_scr[ch, :, lo:lo + 512] for ch in range(len(chains)) for lo in (0, 512))
    fin = lax.fori_loop(0, t_steps, step, init, unroll=2)
    for ch in range(len(chains)):
        h_scr[ch, :, 0:512] = fin[2 * ch]
        h_scr[ch, :, 512:1024] = fin[2 * ch + 1]


def _s5_p1_body(u_ref, wb_ref, a_ref, al_ref, h0_ref, hinit_ref, fin_ref, bu_scr, ab_scr, h_scr, *, t_steps, nt, bsz):
    d = pl.program_id(0)
    i = pl.program_id(2)

    @pl.when(i == 0)
    def _():
        h_scr[...] = jnp.zeros_like(h_scr)

    _s5_project_in(u_ref, wb_ref, a_ref, bu_scr, ab_scr, bsz)
    _s5_scan_tile(d, t_steps, bu_scr, ab_scr, h_scr, False, bsz)

    @pl.when(i == nt - 1)
    def _():
        for ch, (b, jj) in enumerate(_s5_chains(bsz)):
            alr = al_ref[0, jj, :, 0:512]
            ali = al_ref[0, jj, :, 512:1024]
            cr = h0_ref[b, 0, jj, :, 0:512]
            ci = h0_ref[b, 0, jj, :, 512:1024]
            for s in range(S5_SEG):
                k = jnp.where(d == 0, s, S5_SEG - 1 - s)
                hinit_ref[b, 0, jj, pl.ds(k, 1), 0:512] = cr
                hinit_ref[b, 0, jj, pl.ds(k, 1), 512:1024] = ci
                fr = h_scr[ch, pl.ds(k, 1), 0:512]
                fi = h_scr[ch, pl.ds(k, 1), 512:1024]
                cr, ci = alr * cr - ali * ci + fr, alr * ci + ali * cr + fi
            fin_ref[b, 0, jj, :, 0:512] = cr
            fin_ref[b, 0, jj, :, 512:1024] = ci


def _s5_p2_body(u_ref, wb_ref, wc_ref, a_ref, hinit_ref, y_ref, bu_scr, ab_scr, h_scr, *, t_steps, bsz):
    d = pl.program_id(0)
    i = pl.program_id(2)

    @pl.when(i == 0)
    def _():
        for ch, (b, jj) in enumerate(_s5_chains(bsz)):
            h_scr[ch] = hinit_ref[b, 0, jj]

    _s5_project_in(u_ref, wb_ref, a_ref, bu_scr, ab_scr, bsz)
    _s5_scan_tile(d, t_steps, bu_scr, ab_scr, h_scr, True, bsz)
    for ch, (b, jj) in enumerate(_s5_chains(bsz)):
        y_ref[0, b, :, jj * 128:(jj + 1) * 128] = _dot(bu_scr[ch].astype(BF16), wc_ref[0, jj])


def _s5_fin_body(u_ref, yf_ref, yb_ref, d_ref, w_ref, b_ref, o_ref):
    y = u_ref[0] * d_ref[...] + yf_ref[0, 0] + yb_ref[0, 0]
    yg = jax.nn.gelu(y)
    o_ref[0] = yg * jax.nn.sigmoid(_dot(yg.astype(BF16), w_ref[...]) + b_ref[...])


def s5_mixer(p3, h0, lp, wglu_bf, layer, need_output):
    bsz, seq, _ = p3.shape
    lseg = seq // S5_SEG
    t_steps = min(64, lseg)
    nt = lseg // t_steps
    rows = t_steps * S5_SEG
    wb, wc, a, al = _s5_params(lp, lseg)
    u_perm = p3[:, :, 0:GROUP_W].reshape(bsz, S5_SEG, lseg, GROUP_W).transpose(0, 2, 1, 3).reshape(bsz, seq, GROUP_W)

    def tile(d, i):
        return jnp.where(d == 0, i, nt - 1 - i)

    nchain = bsz * S5_JP
    grid = (2, S5_JB // S5_JP, nt)
    u_spec = pl.BlockSpec((bsz, rows, 128 * S5_JP), lambda d, j, i: (0, tile(d, i), j))
    wb_spec = pl.BlockSpec((1, S5_JP, 128, 1024), lambda d, j, i: (d, j, 0, 0))
    a_spec = pl.BlockSpec((1, S5_JP, 1, 1024), lambda d, j, i: (d, j, 0, 0))
    st1_spec = pl.BlockSpec((bsz, 1, S5_JP, 1, 1024), lambda d, j, i: (0, d, j, 0, 0))
    st8_spec = pl.BlockSpec((bsz, 1, S5_JP, S5_SEG, 1024), lambda d, j, i: (0, d, j, 0, 0))
    scratch = [pltpu.VMEM((nchain, rows, 1024), F32), pltpu.VMEM((S5_JP, S5_SEG, 1024), F32),
               pltpu.VMEM((nchain, S5_SEG, 1024), F32)]
    hinit, fin = pl.pallas_call(
        functools.partial(_s5_p1_body, t_steps=t_steps, nt=nt, bsz=bsz),
        grid=grid,
        in_specs=[u_spec, wb_spec, a_spec, a_spec, st1_spec],
        out_specs=[st8_spec, st1_spec],
        out_shape=[jax.ShapeDtypeStruct((bsz, 2, S5_JB, S5_SEG, 1024), F32),
                   jax.ShapeDtypeStruct((bsz, 2, S5_JB, 1, 1024), F32)],
        scratch_shapes=scratch,
        compiler_params=_cp("arbitrary", "arbitrary", "arbitrary"),
        name="s5_pass1",
    )(u_perm, wb, a, al, h0)
    if not need_output:
        return None, fin
    y = pl.pallas_call(
        functools.partial(_s5_p2_body, t_steps=t_steps, bsz=bsz),
        grid=grid,
        in_specs=[u_spec, wb_spec,
                  pl.BlockSpec((1, S5_JP, 1024, 128), lambda d, j, i: (d, j, 0, 0)),
                  a_spec, st8_spec],
        out_specs=pl.BlockSpec((1, bsz, rows, 128 * S5_JP), lambda d, j, i: (d, 0, tile(d, i), j)),
        out_shape=jax.ShapeDtypeStruct((2, bsz, seq, GROUP_W), F32),
        scratch_shapes=scratch,
        compiler_params=_cp("arbitrary", "arbitrary", "arbitrary"),
        name="s5_pass2",
    )(u_perm, wb, wc, a, hinit)
    tr = min(512, seq)
    out = pl.pallas_call(
        _s5_fin_body,
        grid=(bsz, seq // tr),
        in_specs=[pl.BlockSpec((1, tr, GROUP_W), lambda b, i: (b, i, 0)),
                  pl.BlockSpec((1, 1, tr, GROUP_W), lambda b, i: (0, b, i, 0)),
                  pl.BlockSpec((1, 1, tr, GROUP_W), lambda b, i: (1, b, i, 0)),
                  pl.BlockSpec((1, GROUP_W), lambda b, i: (0, 0)),
                  pl.BlockSpec((None, GROUP_W, GROUP_W), lambda b, i: (layer, 0, 0)),
                  pl.BlockSpec((1, GROUP_W), lambda b, i: (0, 0))],
        out_specs=pl.BlockSpec((1, tr, GROUP_W), lambda b, i: (b, i, 0)),
        out_shape=jax.ShapeDtypeStruct((bsz, seq, GROUP_W), F32),
        compiler_params=_cp("arbitrary", "arbitrary"),
        name="s5_finalize",
    )(u_perm, y, y, lp["s5_d"].reshape(1, GROUP_W), wglu_bf, lp["s5_b_glu"].reshape(1, GROUP_W))
    out = out.reshape(bsz, lseg, S5_SEG, GROUP_W).transpose(0, 2, 1, 3).reshape(bsz, seq, GROUP_W)
    return out, fin


def _lru_body(xp_ref, xm_ref, xn_ref, cw_ref, cb_ref, wg_ref, bg_ref, sp_ref, h0_ref, h_ref, fin_ref,
              a_scr, b_scr, hc_scr, *, tile_rows, nt, seq):
    d = pl.program_id(0)
    i = pl.program_id(1)
    ti = jnp.where(d == 0, i, nt - 1 - i)
    bsz = xm_ref.shape[0]

    @pl.when(i == 0)
    def _():
        hc_scr[...] = h0_ref[:, 0]

    n = tile_rows + 16
    rowid = lax.broadcasted_iota(jnp.int32, (n, 1), 0) + (ti * tile_rows - 8)
    valid = (rowid >= 0) & (rowid < seq)
    cw = cw_ref[...]
    for b in range(bsz):
        xe = jnp.where(valid, jnp.concatenate([xp_ref[b], xm_ref[b], xn_ref[b]], axis=0), 0.0)
        xc = (cb_ref[...]
              + cw[0:1] * pltpu.roll(xe, 2, 0)[8:8 + tile_rows]
              + cw[1:2] * pltpu.roll(xe, 1, 0)[8:8 + tile_rows]
              + cw[2:3] * xe[8:8 + tile_rows]
              + cw[3:4] * pltpu.roll(xe, n - 1, 0)[8:8 + tile_rows])
        for cb in range(4):
            lo, hi = cb * 256, (cb + 1) * 256
            xcb = xc[:, lo:hi]
            pre = _dot(xcb.astype(BF16), wg_ref[0, cb])
            r = jax.nn.sigmoid(pre[:, 0:256] + bg_ref[0, :, lo:hi])
            ig = jax.nn.sigmoid(pre[:, 256:512] + bg_ref[0, :, GROUP_W + lo:GROUP_W + hi])
            log_a = -LRU_C * sp_ref[0, :, lo:hi] * r
            a = jnp.exp(log_a)
            a_scr[b, :, lo:hi] = a
            b_scr[b, :, lo:hi] = jnp.sqrt(jnp.tanh(-log_a) * (a * a + 1.0)) * (ig * xcb)

    def step(s, hs):
        t = jnp.where(d == 0, s, tile_rows - 1 - s)
        out = []
        for b in range(bsz):
            h = a_scr[b, pl.ds(t, 1), :] * hs[b] + b_scr[b, pl.ds(t, 1), :]
            b_scr[b, pl.ds(t, 1), :] = h
            out.append(h)
        return tuple(out)

    hs = lax.fori_loop(0, tile_rows, step, tuple(hc_scr[b] for b in range(bsz)), unroll=8)
    for b in range(bsz):
        hc_scr[b] = hs[b]
        fin_ref[b, 0] = hs[b]
    h_ref[0] = b_scr[...]


def _lru_params(lp):
    def blockdiag(w):
        t = w.reshape(2, 4, 4, 64, 64)
        t = jnp.einsum("dcgij,gh->dcgihj", t, jnp.eye(4, dtype=F32))
        return t.reshape(2, 4, 256, 256)

    wg = jnp.concatenate([blockdiag(lp["lru_w_a"]), blockdiag(lp["lru_w_x"])], axis=-1).astype(BF16)
    bg = jnp.concatenate([lp["lru_b_a"], lp["lru_b_x"]], axis=-1).reshape(2, 1, 2 * GROUP_W)
    sp = jax.nn.softplus(-lp["lru_lam"]).reshape(2, 1, GROUP_W)
    return wg, bg, sp


def lru_mixer(p3, h0, lp):
    bsz, seq, _ = p3.shape
    tr = min(512, seq)
    nt = seq // tr
    wg, bg, sp = _lru_params(lp)
    nb8 = seq // 8

    def tile(d, i):
        return jnp.where(d == 0, i, nt - 1 - i)

    return pl.pallas_call(
        functools.partial(_lru_body, tile_rows=tr, nt=nt, seq=seq),
        grid=(2, nt),
        in_specs=[pl.BlockSpec((bsz, 8, GROUP_W), lambda d, i: (0, jnp.maximum(tile(d, i) * (tr // 8) - 1, 0), 1)),
                  pl.BlockSpec((bsz, tr, GROUP_W), lambda d, i: (0, tile(d, i), 1)),
                  pl.BlockSpec((bsz, 8, GROUP_W), lambda d, i: (0, jnp.minimum((tile(d, i) + 1) * (tr // 8), nb8 - 1), 1)),
                  pl.BlockSpec((4, GROUP_W), lambda d, i: (0, 0)),
                  pl.BlockSpec((1, GROUP_W), lambda d, i: (0, 0)),
                  pl.BlockSpec((1, 4, 256, 512), lambda d, i: (d, 0, 0, 0)),
                  pl.BlockSpec((1, 1, 2 * GROUP_W), lambda d, i: (d, 0, 0)),
                  pl.BlockSpec((1, 1, GROUP_W), lambda d, i: (d, 0, 0)),
                  pl.BlockSpec((bsz, 1, 1, GROUP_W), lambda d, i: (0, d, 0, 0))],
        out_specs=[pl.BlockSpec((1, bsz, tr, GROUP_W), lambda d, i: (d, 0, tile(d, i), 0)),
                   pl.BlockSpec((bsz, 1, 1, GROUP_W), lambda d, i: (0, d, 0, 0))],
        out_shape=[jax.ShapeDtypeStruct((2, bsz, seq, GROUP_W), F32),
                   jax.ShapeDtypeStruct((bsz, 2, 1, GROUP_W), F32)],
        scratch_shapes=[pltpu.VMEM((bsz, tr, GROUP_W), F32), pltpu.VMEM((bsz, tr, GROUP_W), F32),
                        pltpu.VMEM((bsz, 1, GROUP_W), F32)],
        compiler_params=_cp("arbitrary", "arbitrary"),
        name="lru_scan",
    )(p3, p3, p3, lp["lru_conv_w"], lp["lru_conv_b"].reshape(1, GROUP_W), wg, bg, sp, h0)


def _log_sigmoid(z):
    return jnp.minimum(z, 0.0) - jnp.log1p(jnp.exp(-jnp.abs(z)))


def _gla_body(qf_ref, kf_ref, vf_ref, lf_ref, qb_ref, kb_ref, vb_ref, lb_ref, wa_ref, ba_ref, tri_ref, s0_ref,
              of_ref, ob_ref, sfin_ref, s_scr, *, nch):
    i = pl.program_id(1)
    c = GLA_CHUNK

    @pl.when(i == 0)
    def _():
        s_scr[...] = s0_ref[0]

    def one_chunk(d, r0, q_ref, k_ref, v_ref, lr_ref, o_ref):
        tri = tri_ref[d]
        z = _dot(lr_ref[0, pl.ds(r0, c), :].astype(BF16), wa_ref[d]) + ba_ref[d]
        la = _log_sigmoid(z) / GLA_GATE_NORM
        tri_bf = tri.astype(BF16)
        la_hi = la.astype(BF16)
        la_r = la - la_hi.astype(F32)
        la_mid = la_r.astype(BF16)
        la_lo = (la_r - la_mid.astype(F32)).astype(BF16)
        b_all = _dot(tri_bf, la_hi) + _dot(tri_bf, la_mid) + _dot(tri_bf, la_lo)
        q_all = q_ref[0, pl.ds(r0, c), :] * (GLA_DK ** -0.5)
        k_all = k_ref[0, pl.ds(r0, c), :]
        v_all = v_ref[0, pl.ds(r0, c), :].astype(BF16)
        mid = c // 2 - 1 if d == 0 else c // 2
        last = c - 1 if d == 0 else 0
        for h in range(GLA_HEADS):
            ks = slice(h * GLA_DK, (h + 1) * GLA_DK)
            vs = slice(h * GLA_DV, (h + 1) * GLA_DV)
            q, k, v, b = q_all[:, ks], k_all[:, ks], v_all[:, vs], b_all[:, ks]
            b_mid = b[mid:mid + 1]
            qd = (q * jnp.exp(b - b_mid)).astype(BF16)
            kd = (k * jnp.exp(b_mid - b)).astype(BF16)
            sc = lax.dot_general(qd, kd, (((1,), (1,)), ((), ())), preferred_element_type=F32) * tri
            intra = _dot(sc.astype(BF16), v)
            st = s_scr[d, h]
            inter = _dot((q * jnp.exp(b)).astype(BF16), st.astype(BF16))
            o_ref[0, 0, pl.ds(r0, c), vs] = intra + inter
            kt = k.T
            bt = b.T
            bt_last = bt[:, last:last + 1]
            k2t = (kt * jnp.exp(bt_last - bt)).astype(BF16)
            s_scr[d, h] = jnp.exp(bt_last) * st + _dot(k2t, v)

    def chunk(s, carry):
        one_chunk(0, pl.multiple_of(s * c, c), qf_ref, kf_ref, vf_ref, lf_ref, of_ref)
        one_chunk(1, pl.multiple_of((nch - 1 - s) * c, c), qb_ref, kb_ref, vb_ref, lb_ref, ob_ref)
        return carry

    lax.fori_loop(0, nch, chunk, 0)
    sfin_ref[0] = s_scr[...]


def gla_mixer(p3, s0, lp):
    bsz, seq, _ = p3.shape
    tr = min(512, seq)
    nt = seq // tr
    qkw = GLA_HEADS * GLA_DK
    wa_pad = jnp.zeros((2, 128, qkw), F32)
    wa_pad = wa_pad.at[0, 0:16].set(lp["gla_w_alpha"][0]).at[1, 16:32].set(lp["gla_w_alpha"][1]).astype(BF16)
    ba = lp["gla_b_alpha"].reshape(2, 1, qkw)
    lower = np.tril(np.ones((GLA_CHUNK, GLA_CHUNK), np.float32))
    tri = jnp.asarray(np.stack([lower, lower.T]))

    def in_specs(tile):
        return [pl.BlockSpec((1, tr, qkw), lambda b, i: (b, tile(i), 3072 // qkw)),
                pl.BlockSpec((1, tr, qkw), lambda b, i: (b, tile(i), 3584 // qkw)),
                pl.BlockSpec((1, tr, GROUP_W), lambda b, i: (b, tile(i), 4)),
                pl.BlockSpec((1, tr, 128), lambda b, i: (b, tile(i), 72))]

    fwd = lambda i: i
    bwd = lambda i: nt - 1 - i
    o_f, o_b, sfin = pl.pallas_call(
        functools.partial(_gla_body, nch=tr // GLA_CHUNK),
        grid=(bsz, nt),
        in_specs=in_specs(fwd) + in_specs(bwd) + [
            _full_spec((2, 128, qkw)), _full_spec((2, 1, qkw)), _full_spec((2, GLA_CHUNK, GLA_CHUNK)),
            pl.BlockSpec((1, 2, GLA_HEADS, GLA_DK, GLA_DV), lambda b, i: (b, 0, 0, 0, 0))],
        out_specs=[pl.BlockSpec((1, 1, tr, GROUP_W), lambda b, i: (0, b, fwd(i), 0)),
                   pl.BlockSpec((1, 1, tr, GROUP_W), lambda b, i: (0, b, bwd(i), 0)),
                   pl.BlockSpec((1, 2, GLA_HEADS, GLA_DK, GLA_DV), lambda b, i: (b, 0, 0, 0, 0))],
        out_shape=[jax.ShapeDtypeStruct((1, bsz, seq, GROUP_W), F32),
                   jax.ShapeDtypeStruct((1, bsz, seq, GROUP_W), F32),
                   jax.ShapeDtypeStruct((bsz, 2, GLA_HEADS, GLA_DK, GLA_DV), F32)],
        scratch_shapes=[pltpu.VMEM((2, GLA_HEADS, GLA_DK, GLA_DV), F32)],
        compiler_params=_cp("arbitrary", "arbitrary"),
        name="gla_scan",
    )(p3, p3, p3, p3, p3, p3, p3, p3, wa_pad, ba, tri, s0)
    return (o_f, o_b), sfin


def _hy_conv3_body(x_ref, w_ref, b_ref, o_ref, *, seq):
    x = x_ref[0]
    w = w_ref[...]
    row = lax.broadcasted_iota(jnp.int32, (seq, 1), 0)
    xm = jnp.where(row == 0, 0.0, pltpu.roll(x, 1, 0))
    xp = jnp.where(row == seq - 1, 0.0, pltpu.roll(x, seq - 1, 0))
    o_ref[0] = w[0:1] * xm + w[1:2] * x + w[2:3] * xp + b_ref[...]


def hy_conv3(p3, lp):
    bsz, seq, _ = p3.shape
    w3 = 3 * GROUP_W
    return pl.pallas_call(
        functools.partial(_hy_conv3_body, seq=seq),
        grid=(bsz, w3 // 128),
        in_specs=[pl.BlockSpec((1, seq, 128), lambda b, c: (b, 0, 48 + c)),
                  pl.BlockSpec((3, 128), lambda b, c: (0, c)),
                  pl.BlockSpec((1, 128), lambda b, c: (0, c))],
        out_specs=pl.BlockSpec((1, seq, 128), lambda b, c: (b, 0, c)),
        out_shape=jax.ShapeDtypeStruct((bsz, seq, w3), F32),
        compiler_params=_cp("arbitrary", "arbitrary"),
        name="hy_conv3",
    )(p3, lp["hy_conv_w"], lp["hy_conv_b"].reshape(1, w3))


def _hy_filter_body(bv_ref, w1_ref, b1_ref, f0_ref, w2_ref, b2_ref, f1_ref, w3_ref, dl_ref, o_ref, *, tr, seq):
    i = pl.program_id(0)
    n = 2 * seq
    t = lax.broadcasted_iota(jnp.int32, (tr, 1), 0) + i * tr
    pos = jnp.where(t < seq, t, n - t).astype(F32)
    tt = pos / seq
    w = (2.0 * math.pi) * pos / seq
    lane = lax.broadcasted_iota(jnp.int32, (tr, 128), 1)
    arg = w * bv_ref[...]
    feats = jnp.where(lane == 0, tt,
                      jnp.where(lane <= HY_BANDS, jnp.cos(arg),
                                jnp.where(lane <= 2 * HY_BANDS, -jnp.sin(arg), 0.0)))
    h = jnp.sin(f0_ref[...] * (jnp.dot(feats, w1_ref[...], preferred_element_type=F32, precision=HI) + b1_ref[...]))
    h = jnp.sin(f1_ref[...] * (jnp.dot(h, w2_ref[...], preferred_element_type=F32, precision=HI) + b2_ref[...]))
    out = jnp.dot(h, w3_ref[0], preferred_element_type=F32, precision=HI)
    out = out * jnp.exp(-tt * dl_ref[...])
    o_ref[...] = jnp.where(t == seq, 0.0, out)


def hy_filter(lp, seq):
    n = 2 * seq
    tr = min(512, seq)
    nt = n // tr
    bands = np.linspace(1e-4, HY_BANDS - 1, HY_BANDS, dtype=np.float32)
    bv = np.zeros((1, 128), np.float32)
    bv[0, 1:1 + HY_BANDS] = bands
    bv[0, 1 + HY_BANDS:1 + 2 * HY_BANDS] = bands
    deltas = np.abs(np.linspace(math.log(1e-2) / 0.3, math.log(1e-2) / 1.5, GROUP_W, dtype=np.float32))
    dl = np.concatenate([deltas, deltas])[None, :]

    def pad2(w, r, c):
        return jnp.zeros((r, c), F32).at[:w.shape[0], :w.shape[1]].set(w)

    w1 = pad2(lp["hy_w1"], 128, 128)
    b1 = pad2(lp["hy_b1"][None, :], 1, 128)
    f0 = pad2(lp["hy_freq"][0][None, :], 1, 128)
    w2 = pad2(lp["hy_w2"], 128, 128)
    b2 = pad2(lp["hy_b2"][None, :], 1, 128)
    f1 = pad2(lp["hy_freq"][1][None, :], 1, 128)
    w3 = lp["hy_w3"].reshape(HY_FFN, 2, 2 * GROUP_W).transpose(1, 0, 2)
    w3 = jnp.zeros((2, 128, 2 * GROUP_W), F32).at[:, :HY_FFN].set(w3)
    half = nt // 2
    vec = lambda i: (0, 0)
    return pl.pallas_call(
        functools.partial(_hy_filter_body, tr=tr, seq=seq),
        grid=(nt,),
        in_specs=[pl.BlockSpec((1, 128), vec), pl.BlockSpec((128, 128), vec), pl.BlockSpec((1, 128), vec),
                  pl.BlockSpec((1, 128), vec), pl.BlockSpec((128, 128), vec), pl.BlockSpec((1, 128), vec),
                  pl.BlockSpec((1, 128), vec),
                  pl.BlockSpec((1, 128, 2 * GROUP_W), lambda i: (jnp.where(i < half, 0, 1), 0, 0)),
                  pl.BlockSpec((1, 2 * GROUP_W), vec)],
        out_specs=pl.BlockSpec((tr, 2 * GROUP_W), lambda i: (i, 0)),
        out_shape=jax.ShapeDtypeStruct((n, 2 * GROUP_W), F32),
        compiler_params=_cp("arbitrary"),
        name="hy_filter",
    )(jnp.asarray(bv), w1, b1, f0, w2, b2, f1, w3, jnp.asarray(dl))


HY_N1 = 128
HY_N2 = 128


@functools.lru_cache(maxsize=None)
def _dft_tables_t():
    n = HY_N1 * HY_N2
    a = np.arange(128)
    ph = 2.0 * np.pi * ((a[:, None] * a[None, :]) % 128) / 128
    cm, sm = np.cos(ph), np.sin(ph)
    f1 = np.concatenate([cm, -sm], axis=1)
    pht = 2.0 * np.pi * (a[:, None] * a[None, :]) / n
    twr, twi = np.cos(pht), -np.sin(pht)
    f2 = np.block([[cm, -sm], [sm, cm]])
    g2 = np.block([[cm, sm], [-sm, cm]])
    g1 = np.concatenate([cm, -sm], axis=0) / n
    return f1, twr, twi, f2, g2, g1


def _hy_tables_args():
    f1, twr, twi, f2, g2, g1 = _dft_tables_t()
    out = [_np_split(m)[0] for m in (f1, f2, g2, g1)]
    out += [jnp.asarray(twr, F32), jnp.asarray(twi, F32)]
    return out


def _full_spec(shape):
    nd = len(shape)
    return pl.BlockSpec(shape, lambda *_: (0,) * nd)


def _dotr(x, f):
    return _dot(x.astype(BF16), f)


def _hy_fwd_t(z3, f1, f2, twr, twi):
    cb = z3.shape[0]
    x = jnp.swapaxes(z3, 1, 2).reshape(cb * 128, 128)
    a3 = _dotr(x, f1).reshape(cb, 128, 256)
    ar, ai = a3[:, :, 0:128], a3[:, :, 128:256]
    br = ar * twr - ai * twi
    bi = ar * twi + ai * twr
    x2 = jnp.concatenate([jnp.swapaxes(br, 1, 2), jnp.swapaxes(bi, 1, 2)], axis=2)
    return _dotr(x2.reshape(cb * 128, 256), f2)


def _hy_inv_t(y, cb, g2, g1, twr, twi):
    b3 = _dotr(y, g2).reshape(cb, 128, 256)
    br = jnp.swapaxes(b3[:, :, 0:128], 1, 2)
    bi = jnp.swapaxes(b3[:, :, 128:256], 1, 2)
    cr = br * twr + bi * twi
    ci = bi * twr - br * twi
    x4 = jnp.concatenate([cr, ci], axis=2).reshape(cb * 128, 256)
    yv = _dotr(x4, g1).reshape(cb, 128, 128)
    return jnp.swapaxes(yv, 1, 2)


HY_SUB = 8


def _hy_spec_t_body(c_ref, f1, f2, g2, g1, twr, twi, o_ref):
    for s in range(c_ref.shape[0] // HY_SUB):
        cs = slice(s * HY_SUB, (s + 1) * HY_SUB)
        z = _hy_fwd_t(c_ref[cs], f1[...], f2[...], twr[...], twi[...])
        o_ref[cs] = z.reshape(HY_SUB, 128, 256)


def _hy_long_t_body(u_ref, g_ref, bias_ref, h_ref, f1, f2, g2, g1, twr, twi, o_ref):
    t1u = u_ref.shape[2]
    cb = HY_SUB
    for s in range(u_ref.shape[1] // cb):
        cs = slice(s * cb, (s + 1) * cb)
        u = u_ref[0, cs]
        z3 = jnp.concatenate([u, jnp.zeros((cb, HY_N1 - t1u, HY_N2), F32)], axis=1)
        z = _hy_fwd_t(z3, f1[...], f2[...], twr[...], twi[...])
        hs = h_ref[cs].reshape(cb * 128, 256)
        zr, zi, hr, hi = z[:, 0:128], z[:, 128:256], hs[:, 0:128], hs[:, 128:256]
        y = jnp.concatenate([zr * hr - zi * hi, zr * hi + zi * hr], axis=1)
        conv = _hy_inv_t(y, cb, g2[...], g1[...], twr[...], twi[...])[:, 0:t1u, :]
        o_ref[0, cs] = g_ref[0, cs] * (conv + bias_ref[cs] * u)


def hy_conv3_t(p3, lp):
    bsz, seq, _ = p3.shape
    w3 = 3 * GROUP_W

    def body(x_ref, w_ref, b_ref, o_ref):
        x = x_ref[0]
        w = w_ref[...]
        row = lax.broadcasted_iota(jnp.int32, (seq, 1), 0)
        xm = jnp.where(row == 0, 0.0, pltpu.roll(x, 1, 0))
        xp = jnp.where(row == seq - 1, 0.0, pltpu.roll(x, seq - 1, 0))
        o_ref[0] = (w[0:1] * xm + w[1:2] * x + w[2:3] * xp + b_ref[...]).T

    return pl.pallas_call(
        body,
        grid=(bsz, w3 // 128),
        in_specs=[pl.BlockSpec((1, seq, 128), lambda b, c: (b, 0, 48 + c)),
                  pl.BlockSpec((3, 128), lambda b, c: (0, c)),
                  pl.BlockSpec((1, 128), lambda b, c: (0, c))],
        out_specs=pl.BlockSpec((1, 128, seq), lambda b, c: (b, c, 0)),
        out_shape=jax.ShapeDtypeStruct((bsz, w3, seq), F32),
        compiler_params=_cp("arbitrary", "arbitrary"),
        name="hy_conv3_t",
    )(p3, lp["hy_conv_w"], lp["hy_conv_b"].reshape(1, w3))


def _hy_filter_t_body(bc_ref, w1_ref, b1_ref, f0_ref, w2_ref, b2_ref, f1_ref, w3_ref, dl_ref, o_ref, *, tr, seq):
    i = pl.program_id(0)
    n = 2 * seq
    t = lax.broadcasted_iota(jnp.int32, (1, tr), 1) + i * tr
    pos = jnp.where(t < seq, t, n - t).astype(F32)
    tt = pos / seq
    w = (2.0 * math.pi) * pos / seq
    row = lax.broadcasted_iota(jnp.int32, (128, tr), 0)
    arg = bc_ref[...] * w
    feats = jnp.where(row == 0, tt,
                      jnp.where(row <= HY_BANDS, jnp.cos(arg),
                                jnp.where(row <= 2 * HY_BANDS, -jnp.sin(arg), 0.0)))
    h = jnp.sin(f0_ref[...] * (jnp.dot(w1_ref[...], feats, preferred_element_type=F32, precision=HI) + b1_ref[...]))
    h = jnp.sin(f1_ref[...] * (jnp.dot(w2_ref[...], h, preferred_element_type=F32, precision=HI) + b2_ref[...]))
    out = _dot(w3_ref[0], h.astype(BF16))
    out = out * jnp.exp(-dl_ref[...] * tt)
    o_ref[...] = jnp.where(t == seq, 0.0, out)


def hy_filter_t(lp, seq):
    n = 2 * seq
    tr = 512
    nt = n // tr
    bands = np.linspace(1e-4, HY_BANDS - 1, HY_BANDS, dtype=np.float32)
    bc = np.zeros((128, 1), np.float32)
    bc[1:1 + HY_BANDS, 0] = bands
    bc[1 + HY_BANDS:1 + 2 * HY_BANDS, 0] = bands
    deltas = np.abs(np.linspace(math.log(1e-2) / 0.3, math.log(1e-2) / 1.5, GROUP_W, dtype=np.float32))
    dl = np.concatenate([deltas, deltas])[:, None]

    def pad2(w, r, c):
        return jnp.zeros((r, c), F32).at[:w.shape[0], :w.shape[1]].set(w)

    w1 = pad2(lp["hy_w1"].T, 128, 128)
    b1 = pad2(lp["hy_b1"][:, None], 128, 1)
    f0 = pad2(lp["hy_freq"][0][:, None], 128, 1)
    w2 = pad2(lp["hy_w2"].T, 128, 128)
    b2 = pad2(lp["hy_b2"][:, None], 128, 1)
    f1 = pad2(lp["hy_freq"][1][:, None], 128, 1)
    w3 = lp["hy_w3"].reshape(HY_FFN, 2, 2 * GROUP_W).transpose(1, 2, 0)
    w3 = jnp.zeros((2, 2 * GROUP_W, 128), F32).at[:, :, :HY_FFN].set(w3).astype(BF16)
    half = nt // 2
    return pl.pallas_call(
        functools.partial(_hy_filter_t_body, tr=tr, seq=seq),
        grid=(nt,),
        in_specs=[_full_spec((128, 1)), _full_spec((128, 128)), _full_spec((128, 1)), _full_spec((128, 1)),
                  _full_spec((128, 128)), _full_spec((128, 1)), _full_spec((128, 1)),
                  pl.BlockSpec((1, 2 * GROUP_W, 128), lambda i: (jnp.where(i < half, 0, 1), 0, 0)),
                  _full_spec((2 * GROUP_W, 1))],
        out_specs=pl.BlockSpec((2 * GROUP_W, tr), lambda i: (0, i)),
        out_shape=jax.ShapeDtypeStruct((2 * GROUP_W, n), F32),
        compiler_params=_cp("arbitrary"),
        name="hy_filter_t",
    )(jnp.asarray(bc), w1, b1, f0, w2, b2, f1, w3, jnp.asarray(dl))


HY_CB = 64


def hyena_long(p3, lp):
    bsz, seq, _ = p3.shape
    t1u = seq // HY_N2
    tabs = _hy_tables_args()
    tab_specs = [_full_spec(t.shape) for t in tabs]
    hzt = hy_conv3_t(p3, lp).reshape(bsz, 3 * GROUP_W, t1u, HY_N2)
    circ = hy_filter_t(lp, seq).reshape(2 * GROUP_W, HY_N1, HY_N2)
    cb = HY_CB
    hspec = pl.pallas_call(
        _hy_spec_t_body,
        grid=(2 * GROUP_W // cb,),
        in_specs=[pl.BlockSpec((cb, HY_N1, HY_N2), lambda c: (c, 0, 0))] + tab_specs,
        out_specs=pl.BlockSpec((cb, HY_N1, 2 * HY_N2), lambda c: (c, 0, 0)),
        out_shape=jax.ShapeDtypeStruct((2 * GROUP_W, HY_N1, 2 * HY_N2), F32),
        compiler_params=_cp("arbitrary"),
        name="hy_spec_t",
    )(circ, *tabs)
    ncb = GROUP_W // cb
    u, ub0 = hzt, 0
    for o in range(2):
        gb0 = (1 + o) * ncb
        u = pl.pallas_call(
            _hy_long_t_body,
            grid=(ncb, bsz),
            in_specs=[pl.BlockSpec((1, cb, t1u, HY_N2), lambda c, b, ub0=ub0: (b, ub0 + c, 0, 0)),
                      pl.BlockSpec((1, cb, t1u, HY_N2), lambda c, b, gb0=gb0: (b, gb0 + c, 0, 0)),
                      pl.BlockSpec((cb, 1, 1), lambda c, b: (c, 0, 0)),
                      pl.BlockSpec((cb, HY_N1, 2 * HY_N2), lambda c, b, o=o: (o * ncb + c, 0, 0))] + tab_specs,
            out_specs=pl.BlockSpec((1, cb, t1u, HY_N2), lambda c, b: (b, c, 0, 0)),
            out_shape=jax.ShapeDtypeStruct((bsz, GROUP_W, t1u, HY_N2), F32),
            compiler_params=_cp("arbitrary", "arbitrary"),
            name="hy_long_t",
        )(u, hzt, lp["hy_bias"][o].reshape(GROUP_W, 1, 1), hspec, *tabs)
        ub0 = 0
    return u.reshape(bsz, GROUP_W, seq)


@functools.lru_cache(maxsize=None)
def _dense_dft_tables(seq):
    n = 2 * seq
    k = np.arange(n)[:, None]
    t = np.arange(n)[None, :]
    ph = 2.0 * np.pi * ((k * t) % n) / n
    fwd = np.concatenate([np.cos(ph), -np.sin(ph)], axis=0)
    inv = np.concatenate([np.cos(ph).T, -np.sin(ph).T], axis=1)[:seq] / n
    return fwd, inv


def _hy_dense_spec_body(c_ref, fh_ref, fl_ref, o_ref):
    o_ref[...] = _dot3(fh_ref[...], fl_ref[...], c_ref[...])


def _hy_dense_body(z_ref, h_ref, fh_ref, fl_ref, gh_ref, gl_ref, g_ref, bias_ref, o_ref, *, n):
    u = z_ref[0]
    z = _dot3(fh_ref[...], fl_ref[...], u)
    zr, zi = z[0:n], z[n:]
    hr, hi = h_ref[0:n], h_ref[n:2 * n]
    y = jnp.concatenate([zr * hr - zi * hi, zr * hi + zi * hr], axis=0)
    conv = _dot3(gh_ref[...], gl_ref[...], y)
    o_ref[0] = g_ref[0] * (conv + bias_ref[...] * u)


def hy_dense(hzc, circ, lp, cb=512):
    bsz, seq, _ = hzc.shape
    n = 2 * seq
    fwd, inv = _dense_dft_tables(seq)
    fh, fl = _np_split(fwd)
    gh, gl = _np_split(inv)
    hspec = pl.pallas_call(
        _hy_dense_spec_body,
        grid=(2 * GROUP_W // cb,),
        in_specs=[pl.BlockSpec((n, cb), lambda c: (0, c)),
                  pl.BlockSpec((2 * n, n), lambda c: (0, 0)),
                  pl.BlockSpec((2 * n, n), lambda c: (0, 0))],
        out_specs=pl.BlockSpec((2 * n, cb), lambda c: (0, c)),
        out_shape=jax.ShapeDtypeStruct((2 * n, 2 * GROUP_W), F32),
        compiler_params=_cp("arbitrary"),
        name="hy_dense_spec",
    )(circ, fh, fl)
    fh_in, fl_in = fh[:, :seq], fl[:, :seq]
    u, uc0 = hzc, 0
    ncb = GROUP_W // cb
    for o in range(2):
        gc0 = (1 + o) * ncb
        u = pl.pallas_call(
            functools.partial(_hy_dense_body, n=n),
            grid=(bsz, ncb),
            in_specs=[pl.BlockSpec((1, seq, cb), lambda b, c, uc0=uc0: (b, 0, uc0 + c)),
                      pl.BlockSpec((2 * n, cb), lambda b, c, o=o: (0, o * ncb + c)),
                      pl.BlockSpec((2 * n, seq), lambda b, c: (0, 0)),
                      pl.BlockSpec((2 * n, seq), lambda b, c: (0, 0)),
                      pl.BlockSpec((seq, 2 * n), lambda b, c: (0, 0)),
                      pl.BlockSpec((seq, 2 * n), lambda b, c: (0, 0)),
                      pl.BlockSpec((1, seq, cb), lambda b, c, gc0=gc0: (b, 0, gc0 + c)),
                      pl.BlockSpec((1, cb), lambda b, c: (0, c))],
            out_specs=pl.BlockSpec((1, seq, cb), lambda b, c: (b, 0, c)),
            out_shape=jax.ShapeDtypeStruct((bsz, seq, GROUP_W), F32),
            compiler_params=_cp("arbitrary", "arbitrary"),
            name="hy_dense_conv",
        )(u, hspec, fh_in, fl_in, gh, gl, hzc, lp["hy_bias"][o].reshape(1, GROUP_W))
        uc0 = 0
    return u


def hyena_mixer(p3, lp):
    bsz, seq, _ = p3.shape
    if 2 * seq == HY_N1 * HY_N2:
        return hyena_long(p3, lp), True
    hzc = hy_conv3(p3, lp)
    circ = hy_filter(lp, seq)
    return hy_dense(hzc, circ, lp), False


def _rms(y, g):
    return y * lax.rsqrt(jnp.mean(y * y, axis=-1, keepdims=True) + EPS) * g


def _mix_body(s5_ref, hf_ref, hb_ref, lg_ref, of_ref, ob_ref, gg_ref, hy_ref, gn_ref, mg_ref, o_ref, *, hy_t):
    w = GROUP_W
    y_hy = hy_ref[0].T if hy_t else hy_ref[...]
    o_ref[:, 0:w] = _rms(s5_ref[...], mg_ref[:, 0:w]).astype(o_ref.dtype)
    y_lru = (hf_ref[0] + hb_ref[0]) * jax.nn.gelu(lg_ref[...])
    o_ref[:, w:2 * w] = _rms(y_lru, mg_ref[:, w:2 * w]).astype(o_ref.dtype)
    o = of_ref[0] + ob_ref[0]
    gg = gg_ref[...]
    heads = []
    for h in range(GLA_HEADS):
        sl = slice(h * GLA_DV, (h + 1) * GLA_DV)
        heads.append(_rms(o[:, sl], gn_ref[...]) * (gg[:, sl] * jax.nn.sigmoid(gg[:, sl])))
    y_gla = jnp.concatenate(heads, axis=-1)
    o_ref[:, 2 * w:3 * w] = _rms(y_gla, mg_ref[:, 2 * w:3 * w]).astype(o_ref.dtype)
    o_ref[:, 3 * w:4 * w] = _rms(y_hy, mg_ref[:, 3 * w:4 * w]).astype(o_ref.dtype)


def mix_assemble(p2, y_s5, h_lru, o_gla, y_hy, hy_t, seq, lp, tm=256):
    m = p2.shape[0]
    w = GROUP_W
    row = lambda i: (i, 0)
    tpb = seq // tm
    if hy_t:
        hy_spec = pl.BlockSpec((1, w, tm), lambda i: (i // tpb, 0, i % tpb))
    else:
        hy_spec = pl.BlockSpec((tm, w), row)
        y_hy = y_hy.reshape(m, w)
    return pl.pallas_call(
        functools.partial(_mix_body, hy_t=hy_t),
        grid=(m // tm,),
        in_specs=[pl.BlockSpec((tm, w), row),
                  pl.BlockSpec((1, tm, w), lambda i: (0, i, 0)),
                  pl.BlockSpec((1, tm, w), lambda i: (1, i, 0)),
                  pl.BlockSpec((tm, w), lambda i: (i, 2)),
                  pl.BlockSpec((1, tm, w), lambda i: (0, i, 0)),
                  pl.BlockSpec((1, tm, w), lambda i: (0, i, 0)),
                  pl.BlockSpec((tm, w), lambda i: (i, 5)),
                  hy_spec,
                  pl.BlockSpec((1, GLA_DV), lambda i: (0, 0)),
                  pl.BlockSpec((1, 4 * w), lambda i: (0, 0))],
        out_specs=pl.BlockSpec((tm, 4 * w), row),
        out_shape=jax.ShapeDtypeStruct((m, 4 * w), BF16),
        compiler_params=_cp("arbitrary"),
        name="mix_assemble",
    )(y_s5.reshape(m, w), h_lru.reshape(2, m, w), h_lru.reshape(2, m, w), p2,
      o_gla[0].reshape(1, m, w), o_gla[1].reshape(1, m, w), p2, y_hy,
      lp["gla_norm_g"].reshape(1, GLA_DV), lp["mix_norm_g"].reshape(1, 4 * w))


FFN_LAG = 2
FFN_SUB_ROWS = 256


def _ffn1_body(h_ref, wg_ref, wu_ref, cw_ref, cb_ref, o_ref, g_scr, u_scr, wgb_scr, wub_scr, *, tm, gw, tps, nt):
    s = pl.program_id(1)

    @pl.when(s == 0)
    def _():
        g_scr[...] = jnp.zeros_like(g_scr)
        u_scr[...] = jnp.zeros_like(u_scr)
        wgb_scr[...] = wg_ref[...].astype(BF16)
        wub_scr[...] = wu_ref[...].astype(BF16)

    t = s - FFN_LAG
    cur = s % 3
    mid = (s + 1) % 3
    nxt = (s + 2) % 3
    g_cur, g_mid, g_nxt = g_scr.at[cur], g_scr.at[mid], g_scr.at[nxt]
    u_cur, u_mid = u_scr.at[cur], u_scr.at[mid]

    rb = min(FFN_SUB_ROWS, tm)
    nsub = tm // rb
    gpt = tm // gw
    tn = o_ref.shape[1]
    col = lax.broadcasted_iota(jnp.int32, (gw, 1), 0)
    first_col = col == 0
    last_col = col == gw - 1

    def grid_row(rho, ls):
        if rho < 0:
            return jnp.where(t % tps == 0, 0.0, g_cur[tm - gw:tm, ls])
        if rho >= gpt:
            return jnp.where(t % tps == tps - 1, 0.0, g_nxt[0:gw, ls])
        return g_mid[rho * gw:(rho + 1) * gw, ls]

    never = s < 0
    chain = [None]

    def conv_finish(r):
        lo, hi = r * rb // gw, (r + 1) * rb // gw
        for lh in range(tn // 128):
            ls = slice(lh * 128, (lh + 1) * 128)
            accs = {}
            for sig in range(lo - 1, hi + 1):
                src = grid_row(sig, ls)
                left = jnp.where(first_col, 0.0, pltpu.roll(src, 1, 0))
                right = jnp.where(last_col, 0.0, pltpu.roll(src, gw - 1, 0))
                for dr in range(3):
                    rho = sig + 1 - dr
                    if not lo <= rho < hi:
                        continue
                    if rho not in accs:
                        accs[rho] = jnp.broadcast_to(cb_ref[:, ls], (gw, 128))
                        if chain[0] is not None:
                            accs[rho] = jnp.where(never, chain[0], accs[rho])
                    accs[rho] = (accs[rho] + cw_ref[3 * dr:3 * dr + 1, ls] * left
                                 + cw_ref[3 * dr + 1:3 * dr + 2, ls] * src + cw_ref[3 * dr + 2:3 * dr + 3, ls] * right)
                done = sig - 1
                if done in accs:
                    acc = accs.pop(done)
                    rows = slice(done * gw, (done + 1) * gw)
                    o_ref[rows, ls] = (acc * jax.nn.sigmoid(acc) * u_mid[rows, ls]).astype(o_ref.dtype)
                    chain[0] = acc

    @pl.when(s < nt)
    def _():
        chain[0] = None
        for r in range(nsub):
            conv_finish(r)
            h = h_ref[r * rb:(r + 1) * rb, :]
            g_cur[r * rb:(r + 1) * rb] = _dot(h, wgb_scr[...])
            u_cur[r * rb:(r + 1) * rb] = _dot(h, wub_scr[...])

    @pl.when(s >= nt)
    def _():
        chain[0] = None
        for r in range(nsub):
            conv_finish(r)


def _ffn1_one_row_body(h_ref, wg_ref, wu_ref, cw_ref, cb_ref, o_ref, g_scr, u_scr, wgb_scr, wub_scr, *, gw):
    @pl.when(pl.program_id(1) == 0)
    def _():
        wgb_scr[...] = wg_ref[...].astype(BF16)
        wub_scr[...] = wu_ref[...].astype(BF16)

    h = h_ref[...]
    g_scr[...] = _dot(h, wgb_scr[...])
    u_scr[...] = _dot(h, wub_scr[...])
    col = lax.broadcasted_iota(jnp.int32, (gw, 1), 0)
    for lh in range(o_ref.shape[1] // 128):
        ls = slice(lh * 128, (lh + 1) * 128)
        src = g_scr[:, ls]
        left = jnp.where(col == 0, 0.0, pltpu.roll(src, 1, 0))
        right = jnp.where(col == gw - 1, 0.0, pltpu.roll(src, gw - 1, 0))
        acc = cb_ref[:, ls] + cw_ref[3:4, ls] * left + cw_ref[4:5, ls] * src + cw_ref[5:6, ls] * right
        o_ref[:, ls] = (acc * jax.nn.sigmoid(acc) * u_scr[:, ls]).astype(o_ref.dtype)


def ffn1(h2, wg, wu, cw, cb, layer, *, seq, gw, tm, tn=256):
    m, d = h2.shape
    ff = wg.shape[-1]
    tps = seq // tm
    nt = m // tm
    w_specs = [pl.BlockSpec((None, d, tn), lambda j, s: (layer, 0, j)),
               pl.BlockSpec((None, d, tn), lambda j, s: (layer, 0, j)),
               pl.BlockSpec((None, 9, tn), lambda j, s: (layer, 0, j)),
               pl.BlockSpec((None, 1, tn), lambda j, s: (layer, 0, j))]
    if tps == 1 and tm == gw:
        return pl.pallas_call(
            functools.partial(_ffn1_one_row_body, gw=gw),
            grid=(ff // tn, nt),
            in_specs=[pl.BlockSpec((tm, d), lambda j, s: (s, 0))] + w_specs,
            out_specs=pl.BlockSpec((tm, tn), lambda j, s: (s, j)),
            out_shape=jax.ShapeDtypeStruct((m, ff), BF16),
            scratch_shapes=[pltpu.VMEM((tm, tn), F32), pltpu.VMEM((tm, tn), F32),
                            pltpu.VMEM((d, tn), BF16), pltpu.VMEM((d, tn), BF16)],
            compiler_params=_cp("arbitrary", "arbitrary"),
            name="ffn_gate_up_one_row",
        )(h2, wg, wu, cw, cb)
    return pl.pallas_call(
        functools.partial(_ffn1_body, tm=tm, gw=gw, tps=tps, nt=nt),
        grid=(ff // tn, nt + FFN_LAG),
        in_specs=[pl.BlockSpec((tm, d), lambda j, s: (jnp.minimum(s, nt - 1), 0)),
                  pl.BlockSpec((None, d, tn), lambda j, s: (layer, 0, j)),
                  pl.BlockSpec((None, d, tn), lambda j, s: (layer, 0, j)),
                  pl.BlockSpec((None, 9, tn), lambda j, s: (layer, 0, j)),
                  pl.BlockSpec((None, 1, tn), lambda j, s: (layer, 0, j))],
        out_specs=pl.BlockSpec((tm, tn), lambda j, s: (jnp.maximum(s - FFN_LAG, 0), j)),
        out_shape=jax.ShapeDtypeStruct((m, ff), BF16),
        scratch_shapes=[pltpu.VMEM((3, tm, tn), F32), pltpu.VMEM((3, tm, tn), F32),
                        pltpu.VMEM((d, tn), BF16), pltpu.VMEM((d, tn), BF16)],
        compiler_params=_cp("arbitrary", "arbitrary"),
        name="ffn_gate_up",
    )(h2, wg, wu, cw, cb)


def _token_mixers(p2, bsz, seq, states, lp, wglu_bf, layer, need_output):
    p3 = p2.reshape(bsz, seq, N_COL_PAD)
    y_s5, st_s5 = s5_mixer(p3, states[0], lp, wglu_bf, layer, need_output)
    h_lru, st_lru = lru_mixer(p3, states[1], lp)
    o_gla, st_gla = gla_mixer(p3, states[2], lp)
    new_states = (st_s5, st_lru, st_gla)
    if not need_output:
        return None, new_states
    y_hy, hy_t = hyena_mixer(p3, lp)
    return mix_assemble(p2, y_s5, h_lru, o_gla, y_hy, hy_t, seq, lp), new_states


def kernel(x, c, ctx, c_ctx, w_ada, b_ada, norm_mix_g, norm_mlp_g, w_in, s5_lam_re, s5_lam_im, s5_log_step, s5_b_re, s5_b_im, s5_c_re, s5_c_im, s5_d, s5_w_glu, s5_b_glu, lru_conv_w, lru_conv_b, lru_w_a, lru_b_a, lru_w_x, lru_b_x, lru_lam, gla_w_alpha, gla_b_alpha, gla_norm_g, hy_conv_w, hy_conv_b, hy_w1, hy_b1, hy_w2, hy_b2, hy_w3, hy_freq, hy_bias, mix_norm_g, w_out, mlp_w_gate, mlp_w_up, mlp_conv_w, mlp_conv_b, mlp_w_down, final_norm_g):
    bsz, seq, d = x.shape
    clen = ctx.shape[1]
    depth = w_ada.shape[0]
    grid_w = 64
    params = dict(
        s5_lam_re=s5_lam_re, s5_lam_im=s5_lam_im, s5_log_step=s5_log_step, s5_b_re=s5_b_re, s5_b_im=s5_b_im,
        s5_c_re=s5_c_re, s5_c_im=s5_c_im, s5_d=s5_d, s5_b_glu=s5_b_glu,
        lru_conv_w=lru_conv_w, lru_conv_b=lru_conv_b, lru_w_a=lru_w_a, lru_b_a=lru_b_a, lru_w_x=lru_w_x,
        lru_b_x=lru_b_x, lru_lam=lru_lam, gla_w_alpha=gla_w_alpha, gla_b_alpha=gla_b_alpha, gla_norm_g=gla_norm_g,
        hy_conv_w=hy_conv_w, hy_conv_b=hy_conv_b, hy_w1=hy_w1, hy_b1=hy_b1, hy_w2=hy_w2, hy_b2=hy_b2, hy_w3=hy_w3,
        hy_freq=hy_freq, hy_bias=hy_bias, mix_norm_g=mix_norm_g)

    w_in_bf = jnp.concatenate(
        [w_in[..., 0:6144], w_in[..., 6176:9248], w_in[..., 6144:6176],
         jnp.zeros((depth, d, N_COL_PAD - 9248), w_in.dtype)], axis=-1).astype(BF16)
    w_out_bf = w_out.astype(BF16)
    wd_bf = mlp_w_down.astype(BF16)
    wglu_bf = s5_w_glu.astype(BF16)
    conv_w9 = mlp_conv_w.reshape(depth, 9, D_FF)
    conv_b = mlp_conv_b.reshape(depth, 1, D_FF)

    cvec = jnp.zeros((8, d), F32).at[0:bsz].set(c).at[bsz].set(c_ctx)
    mod = ada_mod(cvec, w_ada, b_ada)

    x2 = x.reshape(bsz * seq, d)
    c2 = ctx.reshape(bsz * clen, d)
    zero_states = (jnp.zeros((bsz, 2, S5_JB, 1, 1024), F32),
                   jnp.zeros((bsz, 2, 1, GROUP_W), F32),
                   jnp.zeros((bsz, 2, GLA_HEADS, GLA_DK, GLA_DV), F32))

    for l in range(depth):
        last = l == depth - 1
        lp = {k: v[l] for k, v in params.items()}
        mx = mod[l, 0:bsz].reshape(bsz, 1, 6, d)
        mc = mod[l, bsz:bsz + 1].reshape(1, 1, 6, d)
        sh1, sc1, g1, sh2, sc2, g2 = (mx[:, :, i] for i in range(6))
        csh1, csc1, cg1, csh2, csc2, cg2 = (mc[:, :, i] for i in range(6))

        hc = normmod(c2, norm_mix_g[l], csh1, csc1, bsz * clen, BF16)
        pc = matmul(hc, w_in_bf, l, tm=bsz * clen, tn=512, name="in_proj_ctx")
        yc, ctx_states = _token_mixers(pc, bsz, clen, zero_states, lp, wglu_bf, l, need_output=not last)

        hx = normmod(x2, norm_mix_g[l], sh1, sc1, seq, BF16)
        px = matmul(hx, w_in_bf, l, tm=2048, tn=512, name="in_proj")
        yx, _ = _token_mixers(px, bsz, seq, ctx_states, lp, wglu_bf, l, need_output=True)
        x2 = matmul(yx, w_out_bf, l, tm=1024, tn=512, res=x2, gate=g1, rows_per_gate=seq, name="out_proj")
        h2 = normmod(x2, norm_mlp_g[l], sh2, sc2, seq, BF16)
        act = ffn1(h2, mlp_w_gate, mlp_w_up, conv_w9, conv_b, l, seq=seq, gw=grid_w, tm=1024)
        x2 = matmul(act, wd_bf, l, tm=512, tn=512, res=x2, gate=g2, rows_per_gate=seq, name="down_proj")

        if not last:
            c2 = matmul(yc, w_out_bf, l, tm=bsz * clen, tn=512, res=c2, gate=cg1, rows_per_gate=bsz * clen,
                        name="out_proj_ctx")
            hc2 = normmod(c2, norm_mlp_g[l], csh2, csc2, bsz * clen, BF16)
            actc = ffn1(hc2, mlp_w_gate, mlp_w_up, conv_w9, conv_b, l, seq=clen, gw=clen, tm=clen)
            c2 = matmul(actc, wd_bf, l, tm=bsz * clen, tn=512, res=c2, gate=cg2, rows_per_gate=bsz * clen,
                        name="down_proj_ctx")

    zeros = jnp.zeros((1, 1, d), F32)
    out = normmod(x2, final_norm_g, zeros, zeros, bsz * seq, F32)
    return out.reshape(bsz, seq, d)
```
